```python
import jax, jax.numpy as jnp
from jax import lax
import numpy as np

D_MODEL = 1024
BATCH = 4
SEQ = 8192
DEPTH = 4

GRID_W = 64
CTX_LEN = 256
N_BRANCH = 4
BRANCH_W = 512

RW_HEADS = 8
RW_HEAD = 64
RW_C = RW_HEADS * RW_HEAD
RW_DECAY_LORA = 64
RW_ICLR_LORA = 64
RW_GN_EPS = 64e-5
RW_SHIFT_W = 3 * RW_C + RW_DECAY_LORA + RW_ICLR_LORA
CONV_CH = 512
CONV_K = 31
CONV_LN_EPS = 1e-5
WA_HEADS = 8
WA_KV_HEADS = 2
WA_HEAD = 64
WA_GROUP = WA_HEADS // WA_KV_HEADS
WINDOW = 128
BLOCK = 128
WA_SCALE = WA_HEAD ** -0.5
MLA_HEADS = 8
MLA_Q_RANK = 256
MLA_KV_RANK = 128
MLA_NOPE = 64
MLA_ROPE = 32
MLA_V = 64
MLA_SCALE = (MLA_NOPE + MLA_ROPE) ** -0.5

ROPE_BASE = 10000.0
NORM_EPS = 1e-6

IN_SPLITS = (
    ("rw", RW_SHIFT_W),
    ("cv_glu", 2 * CONV_CH),
    ("wa_q", WA_HEADS * WA_HEAD),
    ("wa_k", WA_KV_HEADS * WA_HEAD),
    ("wa_v", WA_KV_HEADS * WA_HEAD),
    ("mla_dq", MLA_Q_RANK),
    ("mla_dkv", MLA_KV_RANK),
    ("mla_kr", MLA_ROPE),
    ("z", N_BRANCH * BRANCH_W),
    ("gate", N_BRANCH * D_MODEL),
)
N_IN = sum(size for _, size in IN_SPLITS)

kernel_name = "hybrid_rwkv7_conformer_swa_mla_diffusion_trunk"


def rmsnorm(t, w):
    tf = t.astype(jnp.float32)
    tf = tf * lax.rsqrt(jnp.mean(tf * tf, axis=-1, keepdims=True) + NORM_EPS)
    return tf.astype(t.dtype) * w


def heads(t, n):
    return t.reshape(t.shape[:-1] + (n, t.shape[-1] // n))


def split_proj(p):
    parts = {}
    off = 0
    for name, size in IN_SPLITS:
        parts[name] = p[..., off:off + size]
        off += size
    return parts


def axial_rope(t, row_pos, col_pos):
    d = t.shape[-1]
    q4 = d // 4
    freqs = ROPE_BASE ** (-jnp.arange(q4, dtype=jnp.float32) / q4)
    mid = (1,) * (t.ndim - 3)
    pieces = []
    for part, pos in ((t[..., : d // 2], row_pos), (t[..., d // 2:], col_pos)):
        ang = pos.astype(jnp.float32)[:, None] * freqs[None, :]
        cos = jnp.cos(ang).reshape((ang.shape[0],) + mid + (q4,))
        sin = jnp.sin(ang).reshape((ang.shape[0],) + mid + (q4,))
        p1 = part[..., :q4].astype(jnp.float32)
        p2 = part[..., q4:].astype(jnp.float32)
        pieces += [p1 * cos - p2 * sin, p1 * sin + p2 * cos]
    return jnp.concatenate(pieces, axis=-1).astype(t.dtype)


def centred_shift_mix(f, mu):
    fp = jnp.pad(f, ((0, 0), (1, 1), (0, 0)))
    nb = 0.5 * (fp[:, :-2] + fp[:, 2:])
    return f + mu * (nb - f)


def rwkv_prepare(rw, mu, w0, w_up, a0, a_up, k_k, k_a):
    rw = centred_shift_mix(rw, mu)
    r, k, v, wd, ad = jnp.split(rw, [RW_C, 2 * RW_C, 3 * RW_C, 3 * RW_C + RW_DECAY_LORA], axis=-1)
    w_pre = (w0[:, None, None, :] + jnp.einsum('btr,grc->gbtc', jnp.tanh(wd), w_up)).astype(jnp.float32)
    w_log = -jax.nn.softplus(-w_pre) - 0.5
    decay = heads(jnp.exp(-jnp.exp(w_log)), RW_HEADS)
    a = heads(jax.nn.sigmoid((a0[:, None, None, :] + jnp.einsum('btr,grc->gbtc', ad, a_up)).astype(jnp.float32)), RW_HEADS)
    kk = heads(k * k_k, RW_HEADS).astype(jnp.float32)
    kk = kk / jnp.maximum(jnp.sqrt(jnp.sum(kk * kk, axis=-1, keepdims=True)), 1e-12)
    k_dir = heads(k, RW_HEADS)[None] * (1.0 + (a - 1.0) * heads(k_a, RW_HEADS))
    return heads(r, RW_HEADS), heads(v, RW_HEADS), kk, decay, a, k_dir


def wkv_scan(s0, w, kk, b, k, v, r, reverse):
    seq = [w, kk, b, k, v] + ([] if r is None else [r])
    xs = tuple(jnp.swapaxes(t, 0, 1).astype(jnp.float32) for t in seq)

    def step(s, inp):
        w_t, kk_t, b_t, k_t, v_t = inp[:5]
        sa = jnp.einsum('bhvk,bhk->bhv', s, kk_t)
        s = s * w_t[:, :, None, :] - sa[..., None] * b_t[:, :, None, :] + v_t[..., None] * k_t[:, :, None, :]
        o = None if r is None else jnp.einsum('bhvk,bhk->bhv', s, inp[5])
        return s, o

    s, o = lax.scan(step, s0, xs, reverse=reverse)
    return s, (None if o is None else jnp.swapaxes(o, 0, 1))


def rwkv_mix(prep, s0, r_k, gn_w, gn_b, readout):
    r, v, kk, decay, a, k_dir = prep
    finals, outs = [], []
    for d, rev in ((0, False), (1, True)):
        s, o = wkv_scan(s0[d], decay[d], kk, kk * a[d], k_dir[d], v, r if readout else None, rev)
        finals.append(s)
        outs.append(o)
    s_fin = jnp.stack(finals)
    if not readout:
        return None, s_fin
    o = outs[0] + outs[1]
    mu = jnp.mean(o, axis=-1, keepdims=True)
    var = jnp.mean(jnp.square(o - mu), axis=-1, keepdims=True)
    bsz, t = o.shape[:2]
    on = ((o - mu) * lax.rsqrt(var + RW_GN_EPS)).reshape(bsz, t, RW_C) * gn_w + gn_b
    rf, vf = r.astype(jnp.float32), v.astype(jnp.float32)
    bonus = sum(jnp.sum(rf * k_dir[d] * r_k, axis=-1, keepdims=True) * vf for d in (0, 1))
    y = on + bonus.reshape(bsz, t, RW_C)
    return y.astype(r.dtype), s_fin


def conv_module(glu, conv_w, conv_b, ln_w, ln_b):
    u = glu[..., :CONV_CH] * jax.nn.sigmoid(glu[..., CONV_CH:])
    u = lax.conv_general_dilated(u, conv_w[:, None, :], window_strides=(1,),
                                 padding=((CONV_K // 2, CONV_K // 2),),
                                 dimension_numbers=('NWC', 'WIO', 'NWC'),
                                 feature_group_count=CONV_CH) + conv_b
    uf = u.astype(jnp.float32)
    mu = jnp.mean(uf, axis=-1, keepdims=True)
    var = jnp.mean(jnp.square(uf - mu), axis=-1, keepdims=True)
    un = ((uf - mu) * lax.rsqrt(var + CONV_LN_EPS)).astype(u.dtype) * ln_w + ln_b
    return jax.nn.silu(un)


def band_mask(nb):
    qpos = jnp.arange(nb)[:, None, None] * BLOCK + jnp.arange(BLOCK)[None, :, None]
    kpos = (jnp.arange(nb)[:, None, None] - 1) * BLOCK + jnp.arange(3 * BLOCK)[None, None, :]
    return (jnp.abs(kpos - qpos) <= WINDOW) & (kpos >= 0) & (kpos < nb * BLOCK)


def window_attn(q, k, v, k_c, v_c, sink):
    bsz, t = q.shape[:2]
    nb = t // BLOCK
    qb = q.reshape(bsz, nb, BLOCK, WA_KV_HEADS, WA_GROUP, WA_HEAD)

    def band(u):
        up = jnp.pad(u, ((0, 0), (BLOCK, BLOCK), (0, 0), (0, 0))).reshape(bsz, nb + 2, BLOCK, WA_KV_HEADS, WA_HEAD)
        return jnp.concatenate([up[:, :-2], up[:, 1:-1], up[:, 2:]], axis=2)

    kb, vb = band(k), band(v)
    s_lat = jnp.einsum('bnqhgd,bnkhd->bnhgqk', qb, kb).astype(jnp.float32) * WA_SCALE
    s_lat = jnp.where(band_mask(nb)[:, None, None], s_lat, -jnp.inf)
    s_ctx = jnp.einsum('bnqhgd,bchd->bnhgqc', qb, k_c).astype(jnp.float32) * WA_SCALE
    sk = sink.astype(jnp.float32).reshape(WA_KV_HEADS, WA_GROUP)[:, :, None, None]
    m = jnp.maximum(jnp.maximum(jnp.max(s_lat, -1, keepdims=True), jnp.max(s_ctx, -1, keepdims=True)), sk)
    e_lat = jnp.exp(s_lat - m)
    e_ctx = jnp.exp(s_ctx - m)
    inv = 1.0 / (jnp.sum(e_lat, -1, keepdims=True) + jnp.sum(e_ctx, -1, keepdims=True) + jnp.exp(sk - m))
    o = (jnp.einsum('bnhgqk,bnkhd->bnqhgd', (e_lat * inv).astype(v.dtype), vb)
         + jnp.einsum('bnhgqc,bchd->bnqhgd', (e_ctx * inv).astype(v.dtype), v_c))
    return o.reshape(bsz, t, WA_HEADS * WA_HEAD)


def gqa_sink_dense(q, k, v, sink):
    bsz, nq = q.shape[:2]
    qg = q.reshape(bsz, nq, WA_KV_HEADS, WA_GROUP, WA_HEAD)
    s = jnp.einsum('bqhgd,bkhd->bhgqk', qg, k).astype(jnp.float32) * WA_SCALE
    sk = sink.astype(jnp.float32).reshape(WA_KV_HEADS, WA_GROUP)[:, :, None, None]
    m = jnp.maximum(jnp.max(s, -1, keepdims=True), sk)
    e = jnp.exp(s - m)
    p = e / (jnp.sum(e, -1, keepdims=True) + jnp.exp(sk - m))
    o = jnp.einsum('bhgqk,bkhd->bqhgd', p.astype(v.dtype), v)
    return o.reshape(bsz, nq, WA_HEADS * WA_HEAD)


def mla_q(dq, q_norm, q_up):
    q = heads(rmsnorm(dq, q_norm) @ q_up, MLA_HEADS)
    return q[..., :MLA_NOPE], q[..., MLA_NOPE:]


def mla_kv(dkv, kv_norm, kv_up):
    kv = heads(rmsnorm(dkv, kv_norm) @ kv_up, MLA_HEADS)
    return kv[..., :MLA_NOPE], kv[..., MLA_NOPE:]


def mla_attend(qn, qr, kn, kr, v):
    s = (jnp.einsum('bqhd,bkhd->bhqk', qn, kn) + jnp.einsum('bqhr,bkr->bhqk', qr, kr)).astype(jnp.float32) * MLA_SCALE
    p = jax.nn.softmax(s, axis=-1).astype(v.dtype)
    return jnp.einsum('bhqk,bkhd->bqhd', p, v)


def mla_latent(qn, qr, kn, kr, v, kn_c, kr_c, v_c):
    kn_all = jnp.concatenate([kn, kn_c], axis=1)
    kr_all = jnp.concatenate([kr, kr_c], axis=1)
    v_all = jnp.concatenate([v, v_c], axis=1)
    bsz, t = qn.shape[:2]
    nb = t // BLOCK

    def blocks(u):
        return jnp.swapaxes(u.reshape((bsz, nb, BLOCK) + u.shape[2:]), 0, 1)

    o = lax.map(lambda qs: mla_attend(qs[0], qs[1], kn_all, kr_all, v_all), (blocks(qn), blocks(qr)))
    return jnp.swapaxes(o, 0, 1).reshape(bsz, t, MLA_HEADS * MLA_V)


def merge(ys, z, gate, w_branch, w_out):
    m = 0.0
    for i, y in enumerate(ys):
        zi = z[..., i * BRANCH_W:(i + 1) * BRANCH_W]
        gi = gate[..., i * D_MODEL:(i + 1) * D_MODEL]
        m = m + jax.nn.sigmoid(gi) * ((y * jax.nn.silu(zi)) @ w_branch[i])
    return m @ w_out


def setup_inputs(seed: int = 0) -> dict:
    key = jax.random.key(seed)
    ks = jax.random.split(key, 32)
    D = D_MODEL

    def nrm(k, shape, s):
        return jax.random.normal(k, shape, jnp.float32) * s

    return {
        "x": nrm(ks[0], (BATCH, SEQ, D), 1.0),
        "c": nrm(ks[1], (BATCH, D), 1.0),
        "ctx": nrm(ks[2], (BATCH, CTX_LEN, D), 1.0),
        "c_ctx": nrm(ks[3], (D,), 1.0),
        "norm_w": 1.0 + nrm(ks[4], (DEPTH, D), 0.02),
        "ada_w": nrm(ks[5], (DEPTH, D, 3 * D), 0.5 * D ** -0.5),
        "ada_b": nrm(ks[6], (DEPTH, 3 * D), 0.02),
        "w_in": nrm(ks[7], (DEPTH, D, N_IN), D ** -0.5),
        "rwkv_mu": jax.random.uniform(ks[8], (DEPTH, RW_SHIFT_W), jnp.float32),
        "rwkv_w0": jax.random.uniform(ks[9], (DEPTH, 2, RW_C), jnp.float32, minval=-6.5, maxval=-1.0),
        "rwkv_w_up": nrm(ks[10], (DEPTH, 2, RW_DECAY_LORA, RW_C), 0.1 * RW_DECAY_LORA ** -0.5),
        "rwkv_a0": nrm(ks[11], (DEPTH, 2, RW_C), 0.1),
        "rwkv_a_up": nrm(ks[12], (DEPTH, 2, RW_ICLR_LORA, RW_C), 0.5 * RW_ICLR_LORA ** -0.5),
        "rwkv_k_k": 0.85 + nrm(ks[13], (DEPTH, RW_C), 0.05),
        "rwkv_k_a": 1.0 + nrm(ks[14], (DEPTH, RW_C), 0.05),
        "rwkv_r_k": nrm(ks[15], (DEPTH, RW_HEADS, RW_HEAD), 0.1),
        "rwkv_gn_w": 1.0 + nrm(ks[16], (DEPTH, RW_C), 0.02),
        "rwkv_gn_b": nrm(ks[17], (DEPTH, RW_C), 0.02),
        "conv_w": nrm(ks[18], (DEPTH, CONV_K, CONV_CH), CONV_K ** -0.5),
        "conv_b": nrm(ks[19], (DEPTH, CONV_CH), 0.02),
        "conv_ln_w": 1.0 + nrm(ks[20], (DEPTH, CONV_CH), 0.02),
        "conv_ln_b": nrm(ks[21], (DEPTH, CONV_CH), 0.02),
        "attn_sink": nrm(ks[22], (DEPTH, WA_HEADS), 0.5),
        "mla_q_norm": 1.0 + nrm(ks[23], (DEPTH, MLA_Q_RANK), 0.02),
        "mla_q_up": nrm(ks[24], (DEPTH, MLA_Q_RANK, MLA_HEADS * (MLA_NOPE + MLA_ROPE)), MLA_Q_RANK ** -0.5),
        "mla_kv_norm": 1.0 + nrm(ks[25], (DEPTH, MLA_KV_RANK), 0.02),
        "mla_kv_up": nrm(ks[26], (DEPTH, MLA_KV_RANK, MLA_HEADS * (MLA_NOPE + MLA_V)), MLA_KV_RANK ** -0.5),
        "w_branch": nrm(ks[27], (DEPTH, N_BRANCH, BRANCH_W, D), BRANCH_W ** -0.5),
        "w_out": nrm(ks[28], (DEPTH, D, D), D ** -0.5),
        "final_norm_w": 1.0 + nrm(ks[29], (D,), 0.02),
    }


def reference(x, c, ctx, c_ctx, norm_w, ada_w, ada_b, w_in, rwkv_mu, rwkv_w0, rwkv_w_up, rwkv_a0,
              rwkv_a_up, rwkv_k_k, rwkv_k_a, rwkv_r_k, rwkv_gn_w, rwkv_gn_b, conv_w, conv_b, conv_ln_w,
              conv_ln_b, attn_sink, mla_q_norm, mla_q_up, mla_kv_norm, mla_kv_up, w_branch, w_out,
              final_norm_w):
    bsz, t, _ = x.shape
    n_ctx = ctx.shape[1]
    rows = t // GRID_W
    row_pos = jnp.repeat(jnp.arange(rows), GRID_W)
    col_pos = jnp.arange(rows * GRID_W) % GRID_W
    s_zero = jnp.zeros((2, bsz, RW_HEADS, RW_HEAD, RW_HEAD), jnp.float32)
    silu_c = jax.nn.silu(c)
    silu_cc = jax.nn.silu(c_ctx)

    for l in range(DEPTH):
        last = l == DEPTH - 1
        mod_x = silu_c @ ada_w[l] + ada_b[l]
        mod_c = silu_cc @ ada_w[l] + ada_b[l]
        sh_x, sc_x, g_x = jnp.split(mod_x[:, None, :], 3, axis=-1)
        sh_c, sc_c, g_c = jnp.split(mod_c, 3)
        hx = rmsnorm(x, norm_w[l]) * (1.0 + sc_x) + sh_x
        hc = rmsnorm(ctx, norm_w[l]) * (1.0 + sc_c) + sh_c
        px = split_proj(hx @ w_in[l])
        pc = split_proj(hc @ w_in[l])

        rw_args = (rwkv_mu[l], rwkv_w0[l], rwkv_w_up[l], rwkv_a0[l], rwkv_a_up[l], rwkv_k_k[l], rwkv_k_a[l])
        out_args = (rwkv_r_k[l], rwkv_gn_w[l], rwkv_gn_b[l])
        ya_c, s_ctx = rwkv_mix(rwkv_prepare(pc["rw"], *rw_args), s_zero, *out_args, readout=not last)
        ya_x, _ = rwkv_mix(rwkv_prepare(px["rw"], *rw_args), s_ctx, *out_args, readout=True)

        conv_args = (conv_w[l], conv_b[l], conv_ln_w[l], conv_ln_b[l])
        yb_x = conv_module(px["cv_glu"], *conv_args)

        qw_x = axial_rope(heads(px["wa_q"], WA_HEADS), row_pos, col_pos)
        kw_x = axial_rope(heads(px["wa_k"], WA_KV_HEADS), row_pos, col_pos)
        vw_x = heads(px["wa_v"], WA_KV_HEADS)
        kw_c = heads(pc["wa_k"], WA_KV_HEADS)
        vw_c = heads(pc["wa_v"], WA_KV_HEADS)
        yc_x = window_attn(qw_x, kw_x, vw_x, kw_c, vw_c, attn_sink[l])

        qn_x, qr_x = mla_q(px["mla_dq"], mla_q_norm[l], mla_q_up[l])
        qr_x = axial_rope(qr_x, row_pos, col_pos)
        kn_x, vd_x = mla_kv(px["mla_dkv"], mla_kv_norm[l], mla_kv_up[l])
        kr_x = axial_rope(px["mla_kr"], row_pos, col_pos)
        kn_c, vd_c = mla_kv(pc["mla_dkv"], mla_kv_norm[l], mla_kv_up[l])
        kr_c = pc["mla_kr"]
        yd_x = mla_latent(qn_x, qr_x, kn_x, kr_x, vd_x, kn_c, kr_c, vd_c)

        x = x + g_x * merge((ya_x, yb_x, yc_x, yd_x), px["z"], px["gate"], w_branch[l], w_out[l])

        if not last:
            yb_c = conv_module(pc["cv_glu"], *conv_args)
            yc_c = gqa_sink_dense(heads(pc["wa_q"], WA_HEADS), kw_c, vw_c, attn_sink[l])
            qn_c, qr_c = mla_q(pc["mla_dq"], mla_q_norm[l], mla_q_up[l])
            yd_c = mla_attend(qn_c, qr_c, kn_c, kr_c, vd_c).reshape(bsz, n_ctx, MLA_HEADS * MLA_V)
            ctx = ctx + g_c * merge((ya_c, yb_c, yc_c, yd_c), pc["z"], pc["gate"], w_branch[l], w_out[l])

    return rmsnorm(x, final_norm_w)
```

```python
import functools

import jax
import jax.numpy as jnp
from jax import lax
from jax.experimental import pallas as pl
from jax.experimental.pallas import tpu as pltpu

F32 = jnp.float32
BF16 = jnp.bfloat16

D_MODEL = 1024
GRID_W = 64
N_BRANCH = 4
BRANCH_W = 512
RW_HEADS = 8
RW_HEAD = 64
RW_C = RW_HEADS * RW_HEAD
RW_LORA = 64
RW_GN_EPS = 64e-5
CONV_CH = 512
CONV_K = 31
CONV_LN_EPS = 1e-5
WA_HEADS = 8
WA_KV_HEADS = 2
WA_HEAD = 64
WA_GROUP = WA_HEADS // WA_KV_HEADS
WINDOW = 128
BLOCK = 128
WA_SCALE = WA_HEAD ** -0.5
MLA_HEADS = 8
MLA_Q_RANK = 256
MLA_KV_RANK = 128
MLA_NOPE = 64
MLA_ROPE = 32
MLA_V = 64
MLA_SCALE = (MLA_NOPE + MLA_ROPE) ** -0.5
ROPE_BASE = 10000.0
NORM_EPS = 1e-6

LANE = 128
SUBLANE = 8
VMEM_LIMIT = 48 * 1024 * 1024

_SRC = {}
_off = 0
for _name, _size in (("r", RW_C), ("k", RW_C), ("v", RW_C), ("wd", RW_LORA), ("ad", RW_LORA),
                     ("cv", 2 * CONV_CH), ("wq", WA_HEADS * WA_HEAD), ("wk", WA_KV_HEADS * WA_HEAD),
                     ("wv", WA_KV_HEADS * WA_HEAD), ("dq", MLA_Q_RANK), ("dkv", MLA_KV_RANK),
                     ("kr", MLA_ROPE), ("z", N_BRANCH * BRANCH_W), ("gate", N_BRANCH * D_MODEL)):
    _SRC[_name] = (_off, _size)
    _off += _size
N_IN = _off

_PACK = (("gate", 4096), ("z", 2048), ("cv", 1024), ("wq", 512), ("r", 512), ("k", 512), ("v", 512),
         ("dq", 256), ("wdad", 128), ("wk", 128), ("wv", 128), ("dkv", 128), ("kr", 128), ("pad", 128))
COL = {}
_off = 0
for _name, _size in _PACK:
    assert _off % _size == 0
    COL[_name] = (_off, _size)
    _off += _size
N_PACK = _off


def _cblk(name):
    off, size = COL[name]
    return off // size


def _pack_cols(w):
    def src(name):
        o, s = _SRC[name]
        return w[..., o:o + s]
    pieces = []
    for name, size in _PACK:
        if name == "wdad":
            pieces += [src("wd"), src("ad")]
        elif name == "kr":
            pieces += [src("kr"), jnp.zeros(w.shape[:-1] + (size - MLA_ROPE,), w.dtype)]
        elif name == "pad":
            pieces.append(jnp.zeros(w.shape[:-1] + (size,), w.dtype))
        else:
            pieces.append(src(name))
    return jnp.concatenate(pieces, axis=-1)


def _params(*sem):
    return pltpu.CompilerParams(dimension_semantics=sem, vmem_limit_bytes=VMEM_LIMIT)


def _split2(x):
    hi = x.astype(BF16)
    lo = (x - hi.astype(F32)).astype(BF16)
    return hi, lo


def _split3(x):
    hi = x.astype(BF16)
    r1 = x - hi.astype(F32)
    mid = r1.astype(BF16)
    lo = (r1 - mid.astype(F32)).astype(BF16)
    return hi, mid, lo


def _ein(spec, a, b, passes=1):
    def e(x, y):
        return jnp.einsum(spec, x, y, preferred_element_type=F32)
    if passes == 1:
        return e(a.astype(BF16), b.astype(BF16))
    ah, al = _split2(a)
    bh, bl = _split2(b)
    return e(ah, bh) + (e(ah, bl) + e(al, bh))


def _mod_kernel(c_ref, w_ref, b_ref, o_ref):
    cv = c_ref[...]
    sc = cv * jax.nn.sigmoid(cv)
    o_ref[0] = jnp.dot(sc, w_ref[0], preferred_element_type=F32,
                       precision=lax.Precision.HIGHEST) + b_ref[0]


def _modulation(cvec, ada_w, ada_b):
    depth, d, d3 = ada_w.shape
    rows = cvec.shape[0]
    tn = 512
    return pl.pallas_call(
        _mod_kernel,
        grid=(depth, d3 // tn),
        in_specs=[pl.BlockSpec((rows, d), lambda l, j: (0, 0)),
                  pl.BlockSpec((1, d, tn), lambda l, j: (l, 0, j)),
                  pl.BlockSpec((1, 1, tn), lambda l, j: (l, 0, j))],
        out_specs=pl.BlockSpec((1, rows, tn), lambda l, j: (l, 0, j)),
        out_shape=jax.ShapeDtypeStruct((depth, rows, d3), F32),
        compiler_params=_params("parallel", "parallel"),
        name="modulation",
    )(cvec, ada_w, ada_b.reshape(depth, 1, d3))


def _inproj_kernel(x_ref, nw_ref, sc_ref, sh_ref, w_ref, o_ref, h_ref):
    @pl.when(pl.program_id(2) == 0)
    def _():
        xf = x_ref[0]
        ms = jnp.mean(xf * xf, axis=-1, keepdims=True)
        hn = (xf * lax.rsqrt(ms + NORM_EPS)) * nw_ref[...]
        h_ref[...] = (hn * (1.0 + sc_ref[0]) + sh_ref[0]).astype(BF16)
    o_ref[0] = jnp.dot(h_ref[...], w_ref[...], preferred_element_type=F32)


def _in_proj(x, norm_w, scale, shift, w_packed):
    bsz, t, d = x.shape
    tm = min(512, t)
    tn = 1024
    return pl.pallas_call(
        _inproj_kernel,
        grid=(bsz, t // tm, N_PACK // tn),
        in_specs=[pl.BlockSpec((1, tm, d), lambda b, i, j: (b, i, 0)),
                  pl.BlockSpec((1, d), lambda b, i, j: (0, 0)),
                  pl.BlockSpec((1, 1, d), lambda b, i, j: (b, 0, 0)),
                  pl.BlockSpec((1, 1, d), lambda b, i, j: (b, 0, 0)),
                  pl.BlockSpec((d, tn), lambda b, i, j: (0, j))],
        out_specs=pl.BlockSpec((1, tm, tn), lambda b, i, j: (b, i, j)),
        out_shape=jax.ShapeDtypeStruct((bsz, t, N_PACK), F32),
        scratch_shapes=[pltpu.VMEM((tm, d), BF16)],
        compiler_params=_params("parallel", "parallel", "arbitrary"),
        name="in_proj",
    )(x, norm_w.reshape(1, d), scale, shift, w_packed)


RW_CHUNK = 64
P_DATA = 1
P_STATE = 3


def _heads_major(x):
    return jnp.stack([x[:, RW_HEAD * h:RW_HEAD * (h + 1)] for h in range(RW_HEADS)], axis=0)


def _rwkv_p1_kernel(r_ref, k_ref, v_ref, wa_ref, rp_ref, kp_ref, vp_ref, wap_ref,
                    rn_ref, kn_ref, vn_ref, wan_ref,
                    mur_ref, muk_ref, muv_ref, muwa_ref, w0_ref, wup_ref, a0_ref, aup_ref,
                    kkw_ref, kaw_ref, rkw_ref,
                    qt_ref, o0_ref, g_ref, h_ref, bonus_ref, *, nc):
    L = RW_CHUNK
    c = pl.program_id(1)

    def shift_mix(cur_ref, p_ref, n_ref, mu_ref):
        cur = cur_ref[0]
        prow = jnp.where(c > 0, p_ref[0, SUBLANE - 1:SUBLANE, :], 0.0)
        nrow = jnp.where(c < nc - 1, n_ref[0, 0:1, :], 0.0)
        ridx = lax.broadcasted_iota(jnp.int32, cur.shape, 0)
        fprev = jnp.where(ridx == 0, prow, pltpu.roll(cur, 1, 0))
        fnext = jnp.where(ridx == L - 1, nrow, pltpu.roll(cur, L - 1, 0))
        return cur + mu_ref[...] * (0.5 * (fprev + fnext) - cur)

    r = shift_mix(r_ref, rp_ref, rn_ref, mur_ref)
    k = shift_mix(k_ref, kp_ref, kn_ref, muk_ref)
    v = shift_mix(v_ref, vp_ref, vn_ref, muv_ref)
    wa = shift_mix(wa_ref, wap_ref, wan_ref, muwa_ref)
    wd_t = jnp.tanh(wa[:, :RW_LORA]).astype(BF16)
    ad = wa[:, RW_LORA:].astype(BF16)

    r_h = _heads_major(r)
    k_h = _heads_major(k)
    v_h = _heads_major(v)
    kk_h = k_h * _heads_major(kkw_ref[...])
    kk_h = kk_h / jnp.maximum(jnp.sqrt(jnp.sum(kk_h * kk_h, axis=-1, keepdims=True)), 1e-12)
    ka_h = _heads_major(kaw_ref[...])
    rk_h = _heads_major(rkw_ref[...])

    row = lax.broadcasted_iota(jnp.int32, (L, L), 0)
    col = lax.broadcasted_iota(jnp.int32, (L, L), 1)
    eye = (row == col)[None]
    kdir_sum = None
    for d in range(2):
        incl2 = (col <= row) if d == 0 else (col >= row)
        strict = ((col < row) if d == 0 else (col > row))[None]
        incl = incl2[None]
        last = L - 1 if d == 0 else 0

        w_pre = w0_ref[d:d + 1, :] + jnp.dot(wd_t, wup_ref[d].astype(BF16), preferred_element_type=F32)
        w_log = -(jnp.maximum(-w_pre, 0.0) + jnp.log1p(jnp.exp(-jnp.abs(w_pre)))) - 0.5
        logw = -jnp.exp(w_log)
        lh, lm, ll = _split3(logw)
        tri = incl2.astype(BF16)
        cum = (jnp.dot(tri, lh, preferred_element_type=F32)
               + (jnp.dot(tri, lm, preferred_element_type=F32) + jnp.dot(tri, ll, preferred_element_type=F32)))
        a = jax.nn.sigmoid(a0_ref[d:d + 1, :] + jnp.dot(ad, aup_ref[d].astype(BF16), preferred_element_type=F32))

        logw_h = _heads_major(logw)
        cum_h = _heads_major(cum)
        a_h = _heads_major(a)
        cum_l = cum_h[:, last:last + 1, :]
        e_pos = jnp.exp(cum_h)
        e_in = jnp.exp(cum_h - logw_h)
        e_neg = jnp.exp(-cum_h)
        e_l = jnp.exp(cum_l - cum_h)
        w_l = jnp.exp(cum_l)

        kdir = k_h * (1.0 + (a_h - 1.0) * ka_h)
        kdir_sum = kdir if kdir_sum is None else kdir_sum + kdir
        b_h = kk_h * a_h
        al = kk_h * e_in
        rho = r_h * e_pos
        be = b_h * e_neg
        ka = kdir * e_neg
        be_l = b_h * e_l
        ka_l = kdir * e_l

        ar = jnp.concatenate([al, rho], axis=1)
        mb = _ein('hld,hmd->hlm', ar, be, P_DATA)
        mk = _ein('hld,hmd->hlm', ar, ka, P_DATA)
        nmat = jnp.where(strict, mb[:, :L], 0.0)
        aks = jnp.where(strict, mk[:, :L], 0.0)
        rbi = jnp.where(incl, mb[:, L:], 0.0)
        rki = jnp.where(incl, mk[:, L:], 0.0)

        tmat = jnp.where(eye, 1.0, 0.0) - nmat
        pw = nmat
        steps = L.bit_length() - 1
        for s in range(1, steps):
            pw = _ein('hij,hjk->hik', pw, pw, P_DATA)
            tmat = tmat + _ein('hij,hjk->hik', tmat, pw, P_DATA)

        akv = _ein('hlm,hmv->hlv', aks, v_h, P_DATA)
        ua = _ein('hlm,hmd->hld', tmat, al, P_DATA)
        d0 = -_ein('hlm,hmv->hlv', tmat, akv, P_DATA)
        qt = rho - _ein('hlm,hmd->hld', rbi, ua, P_DATA)
        o0 = _ein('hlm,hmv->hlv', rbi, d0, P_DATA) + _ein('hlm,hmv->hlv', rki, v_h, P_DATA)
        be_lt = jnp.swapaxes(be_l, 1, 2)
        ka_lt = jnp.swapaxes(ka_l, 1, 2)
        g = jnp.where(eye, w_l, 0.0) - _ein('hkl,hld->hkd', be_lt, ua, P_DATA)
        hh = _ein('hkl,hlv->hkv', be_lt, d0, P_DATA) + _ein('hkl,hlv->hkv', ka_lt, v_h, P_DATA)

        qt_ref[0, d] = qt
        o0_ref[0, d] = o0
        g_ref[0, d, 0] = g
        h_ref[0, d, 0] = hh

    bonus_ref[0] = jnp.sum(r_h * kdir_sum * rk_h, axis=-1, keepdims=True) * v_h


def _rwkv_p2_kernel(qt_ref, o0_ref, g_ref, h_ref, s0_ref, o_ref, sfin_ref, s_ref, *, nc):
    c = pl.program_id(2)

    @pl.when(c == 0)
    def _():
        s_ref[...] = s0_ref[0, 0]

    s = s_ref[...]
    o_ref[0, 0] = _ein('hlk,hkv->hlv', qt_ref[0, 0], s, P_STATE) + o0_ref[0, 0]
    s_new = _ein('hkj,hjv->hkv', g_ref[0, 0, 0], s, P_STATE) + h_ref[0, 0, 0]
    s_ref[...] = s_new

    @pl.when(c == nc - 1)
    def _():
        sfin_ref[0, 0] = s_new


def _rwkv(p, s0, mu, w0, w_up, a0, a_up, k_k, k_a, r_k):
    bsz, t, _ = p.shape
    L = RW_CHUNK
    nc = t // L
    hb = L // SUBLANE
    nrow8 = t // SUBLANE

    def cur(name):
        blk = _cblk(name)
        w = COL[name][1]
        return pl.BlockSpec((1, L, w), lambda b, c: (b, c, blk))

    def prev(name):
        blk = _cblk(name)
        w = COL[name][1]
        return pl.BlockSpec((1, SUBLANE, w), lambda b, c: (b, jnp.maximum(c * hb - 1, 0), blk))

    def nxt(name):
        blk = _cblk(name)
        w = COL[name][1]
        return pl.BlockSpec((1, SUBLANE, w), lambda b, c: (b, jnp.minimum((c + 1) * hb, nrow8 - 1), blk))

    def full(a):
        nd = a.ndim
        return pl.BlockSpec(a.shape, lambda b, c: (0,) * nd)

    names = ("r", "k", "v", "wdad")
    mu_p = [mu[o:o + s].reshape(1, s) for o, s in
            ((0, RW_C), (RW_C, RW_C), (2 * RW_C, RW_C), (3 * RW_C, 2 * RW_LORA))]
    small = mu_p + [w0, w_up, a0, a_up, k_k.reshape(1, RW_C), k_a.reshape(1, RW_C), r_k.reshape(1, RW_C)]
    hm = (RW_HEADS, L, RW_HEAD)
    qt, o0, g, h, bonus = pl.pallas_call(
        functools.partial(_rwkv_p1_kernel, nc=nc),
        grid=(bsz, nc),
        in_specs=[cur(n) for n in names] + [prev(n) for n in names] + [nxt(n) for n in names]
                 + [full(a) for a in small],
        out_specs=[pl.BlockSpec((1, 2) + hm, lambda b, c: (b, 0, 0, c, 0)),
                   pl.BlockSpec((1, 2) + hm, lambda b, c: (b, 0, 0, c, 0)),
                   pl.BlockSpec((1, 2, 1, RW_HEADS, RW_HEAD, RW_HEAD), lambda b, c: (b, 0, c, 0, 0, 0)),
                   pl.BlockSpec((1, 2, 1, RW_HEADS, RW_HEAD, RW_HEAD), lambda b, c: (b, 0, c, 0, 0, 0)),
                   pl.BlockSpec((1,) + hm, lambda b, c: (b, 0, c, 0))],
        out_shape=[jax.ShapeDtypeStruct((bsz, 2, RW_HEADS, t, RW_HEAD), F32),
                   jax.ShapeDtypeStruct((bsz, 2, RW_HEADS, t, RW_HEAD), F32),
                   jax.ShapeDtypeStruct((bsz, 2, nc, RW_HEADS, RW_HEAD, RW_HEAD), F32),
                   jax.ShapeDtypeStruct((bsz, 2, nc, RW_HEADS, RW_HEAD, RW_HEAD), F32),
                   jax.ShapeDtypeStruct((bsz, RW_HEADS, t, RW_HEAD), F32)],
        compiler_params=_params("parallel", "parallel"),
        name="rwkv_chunk_prep",
    )(*([p] * 12), *small)

    def cc(c, d):
        return c + d * (nc - 1 - 2 * c)

    o, s_fin = pl.pallas_call(
        functools.partial(_rwkv_p2_kernel, nc=nc),
        grid=(bsz, 2, nc),
        in_specs=[pl.BlockSpec((1, 1) + hm, lambda b, d, c: (b, d, 0, cc(c, d), 0)),
                  pl.BlockSpec((1, 1) + hm, lambda b, d, c: (b, d, 0, cc(c, d), 0)),
                  pl.BlockSpec((1, 1, 1, RW_HEADS, RW_HEAD, RW_HEAD), lambda b, d, c: (b, d, cc(c, d), 0, 0, 0)),
                  pl.BlockSpec((1, 1, 1, RW_HEADS, RW_HEAD, RW_HEAD), lambda b, d, c: (b, d, cc(c, d), 0, 0, 0)),
                  pl.BlockSpec((1, 1, RW_HEADS, RW_HEAD, RW_HEAD), lambda b, d, c: (b, d, 0, 0, 0))],
        out_specs=[pl.BlockSpec((1, 1) + hm, lambda b, d, c: (b, d, 0, cc(c, d), 0)),
                   pl.BlockSpec((1, 1, RW_HEADS, RW_HEAD, RW_HEAD), lambda b, d, c: (b, d, 0, 0, 0))],
        out_shape=[jax.ShapeDtypeStruct((bsz, 2, RW_HEADS, t, RW_HEAD), F32),
                   jax.ShapeDtypeStruct((bsz, 2, RW_HEADS, RW_HEAD, RW_HEAD), F32)],
        scratch_shapes=[pltpu.VMEM((RW_HEADS, RW_HEAD, RW_HEAD), F32)],
        compiler_params=_params("parallel", "parallel", "arbitrary"),
        name="rwkv_state_scan",
    )(qt, o0, g, h, s0)
    return o, bonus, s_fin


CONV_HALO = 16


def _conv_kernel(cur_ref, prev_ref, next_ref, w_ref, b_ref, lnw_ref, lnb_ref, o_ref, u_ref, *, nt, tt):
    i = pl.program_id(1)

    def glu(x):
        return x[:, :CONV_CH] * jax.nn.sigmoid(x[:, CONV_CH:])

    u_ref[0:CONV_HALO, :] = jnp.where(i > 0, glu(prev_ref[0]), 0.0)
    u_ref[CONV_HALO:CONV_HALO + tt, :] = glu(cur_ref[0])
    u_ref[CONV_HALO + tt:, :] = jnp.where(i < nt - 1, glu(next_ref[0]), 0.0)
    acc = jnp.zeros((tt, CONV_CH), F32) + b_ref[...]
    base = CONV_HALO - CONV_K // 2
    for j in range(CONV_K):
        acc = acc + u_ref[base + j:base + j + tt, :] * w_ref[j:j + 1, :]
    mu = jnp.mean(acc, axis=-1, keepdims=True)
    xc = acc - mu
    var = jnp.mean(xc * xc, axis=-1, keepdims=True)
    un = (xc * lax.rsqrt(var + CONV_LN_EPS)) * lnw_ref[...] + lnb_ref[...]
    o_ref[0] = un * jax.nn.sigmoid(un)


def _conv(p, conv_w, conv_b, ln_w, ln_b):
    bsz, t, _ = p.shape
    tt = min(512, t)
    nt = t // tt
    hb = tt // CONV_HALO
    nh = t // CONV_HALO
    blk = _cblk("cv")
    w = COL["cv"][1]
    vec = lambda a: a.reshape(1, CONV_CH)
    return pl.pallas_call(
        functools.partial(_conv_kernel, nt=nt, tt=tt),
        grid=(bsz, nt),
        in_specs=[pl.BlockSpec((1, tt, w), lambda b, i: (b, i, blk)),
                  pl.BlockSpec((1, CONV_HALO, w), lambda b, i: (b, jnp.maximum(i * hb - 1, 0), blk)),
                  pl.BlockSpec((1, CONV_HALO, w), lambda b, i: (b, jnp.minimum((i + 1) * hb, nh - 1), blk)),
                  pl.BlockSpec((CONV_K, CONV_CH), lambda b, i: (0, 0)),
                  pl.BlockSpec((1, CONV_CH), lambda b, i: (0, 0)),
                  pl.BlockSpec((1, CONV_CH), lambda b, i: (0, 0)),
                  pl.BlockSpec((1, CONV_CH), lambda b, i: (0, 0))],
        out_specs=pl.BlockSpec((1, tt, CONV_CH), lambda b, i: (b, i, 0)),
        out_shape=jax.ShapeDtypeStruct((bsz, t, CONV_CH), F32),
        scratch_shapes=[pltpu.VMEM((tt + 2 * CONV_HALO, CONV_CH), F32)],
        compiler_params=_params("parallel", "parallel"),
        name="conv_module",
    )(p, p, p, conv_w, vec(conv_b), vec(ln_w), vec(ln_b))


def _swap_halves(x, hs):
    slabs = []
    for j in range(x.shape[-1] // LANE):
        s = x[:, LANE * j:LANE * (j + 1)]
        lane = lax.broadcasted_iota(jnp.int32, s.shape, 1)
        slabs.append(jnp.where((lane % (2 * hs)) < hs, pltpu.roll(s, LANE - hs, 1), pltpu.roll(s, hs, 1)))
    return slabs[0] if len(slabs) == 1 else jnp.concatenate(slabs, axis=-1)


def _rope(x, cos, sin, hs):
    reps = x.shape[-1] // LANE
    if reps > 1:
        cos = jnp.concatenate([cos] * reps, axis=-1)
        sin = jnp.concatenate([sin] * reps, axis=-1)
    return x * cos + _swap_halves(x, hs) * sin


def _rms(x, w):
    ms = jnp.mean(x * x, axis=-1, keepdims=True)
    return (x * lax.rsqrt(ms + NORM_EPS)) * w


def _prep_kernel(*refs, rope):
    if rope:
        (dq_ref, dkv_ref, kr_ref, wq_ref, wk_ref, c64_ref, s64_ref, c32_ref, s32_ref,
         qnw_ref, qup_ref, kvnw_ref, kvup_ref, qm_ref, kmt_ref, vm_ref, wqo_ref, wko_ref) = refs
    else:
        (dq_ref, dkv_ref, kr_ref, wq_ref, wk_ref,
         qnw_ref, qup_ref, kvnw_ref, kvup_ref, qm_ref, kmt_ref, vm_ref, wqo_ref, wko_ref) = refs
    hw = MLA_HEADS * MLA_NOPE
    q = jnp.dot(_rms(dq_ref[0], qnw_ref[...]).astype(BF16), qup_ref[...], preferred_element_type=F32)
    kv = jnp.dot(_rms(dkv_ref[0], kvnw_ref[...]).astype(BF16), kvup_ref[...], preferred_element_type=F32)
    qn, qr = q[:, :hw], q[:, hw:]
    kn, vv = kv[:, :hw], kv[:, hw:]
    kr = kr_ref[0]
    wq = wq_ref[0]
    wk = wk_ref[0]
    if rope:
        qr = _rope(qr, c32_ref[...], s32_ref[...], MLA_ROPE // 4)
        kr = _rope(kr, c32_ref[...], s32_ref[...], MLA_ROPE // 4)
        wq = _rope(wq, c64_ref[...], s64_ref[...], WA_HEAD // 4)
        wk = _rope(wk, c64_ref[...], s64_ref[...], WA_HEAD // 4)
    wqo_ref[0] = wq.astype(BF16)
    wko_ref[0] = wk.astype(BF16)
    for h in range(MLA_HEADS):
        sl = slice(MLA_NOPE * h, MLA_NOPE * (h + 1))
        qm_ref[0, h] = jnp.concatenate([qn[:, sl], qr[:, sl]], axis=-1).astype(BF16)
        kh = jnp.concatenate([kn[:, sl], kr[:, :MLA_NOPE]], axis=-1)
        kmt_ref[0, h, 0] = kh.T.astype(BF16)
        vm_ref[0, h] = vv[:, sl].astype(BF16)


def _attn_prep(p, tables, q_norm, q_up_p, kv_norm, kv_up_p, tk):
    bsz, t, _ = p.shape
    tm = tk
    rope = tables is not None

    def slab(name):
        blk = _cblk(name)
        return pl.BlockSpec((1, tm, COL[name][1]), lambda b, i: (b, i, blk))

    def full(a):
        nd = a.ndim
        return pl.BlockSpec(a.shape, lambda b, i: (0,) * nd)

    ins = [p] * 5
    in_specs = [slab(n) for n in ("dq", "dkv", "kr", "wq", "wk")]
    if rope:
        ins += list(tables)
        in_specs += [pl.BlockSpec((tm, LANE), lambda b, i: (i, 0))] * 4
    small = [q_norm.reshape(1, -1), q_up_p, kv_norm.reshape(1, -1), kv_up_p]
    ins += small
    in_specs += [full(a) for a in small]
    return pl.pallas_call(
        functools.partial(_prep_kernel, rope=rope),
        grid=(bsz, t // tm),
        in_specs=in_specs,
        out_specs=[pl.BlockSpec((1, MLA_HEADS, tm, LANE), lambda b, i: (b, 0, i, 0)),
                   pl.BlockSpec((1, MLA_HEADS, 1, LANE, tm), lambda b, i: (b, 0, i, 0, 0)),
                   pl.BlockSpec((1, MLA_HEADS, tm, MLA_V), lambda b, i: (b, 0, i, 0)),
                   pl.BlockSpec((1, tm, WA_HEADS * WA_HEAD), lambda b, i: (b, i, 0)),
                   pl.BlockSpec((1, tm, WA_KV_HEADS * WA_HEAD), lambda b, i: (b, i, 0))],
        out_shape=[jax.ShapeDtypeStruct((bsz, MLA_HEADS, t, LANE), BF16),
                   jax.ShapeDtypeStruct((bsz, MLA_HEADS, t // tm, LANE, tm), BF16),
                   jax.ShapeDtypeStruct((bsz, MLA_HEADS, t, MLA_V), BF16),
                   jax.ShapeDtypeStruct((bsz, t, WA_HEADS * WA_HEAD), BF16),
                   jax.ShapeDtypeStruct((bsz, t, WA_KV_HEADS * WA_HEAD), BF16)],
        compiler_params=_params("parallel", "parallel"),
        name="attn_prep",
    )(*ins)


def _rope_tables(t):
    pos = jnp.arange(t)
    row = (pos // GRID_W).astype(F32)[:, None]
    col = (pos % GRID_W).astype(F32)[:, None]
    out = []
    for d in (WA_HEAD, MLA_ROPE):
        q4 = d // 4
        freqs = ROPE_BASE ** (-jnp.arange(q4, dtype=F32) / q4)
        ar, ac = row * freqs[None, :], col * freqs[None, :]
        cos = jnp.concatenate([jnp.cos(ar), jnp.cos(ar), jnp.cos(ac), jnp.cos(ac)], axis=-1)
        sin = jnp.concatenate([-jnp.sin(ar), jnp.sin(ar), -jnp.sin(ac), jnp.sin(ac)], axis=-1)
        out += [jnp.tile(cos, (1, LANE // d)), jnp.tile(sin, (1, LANE // d))]
    return tuple(out)


def _wattn_kernel(*refs, band, nb):
    if band:
        q_ref, kp_ref, kc_ref, kn_ref, vp_ref, vc_ref, vn_ref, kx_ref, vx_ref, sink_ref, o_ref = refs
    else:
        q_ref, kx_ref, vx_ref, sink_ref, o_ref = refs
    n = pl.program_id(1)
    q = q_ref[0]
    tq = q.shape[0]
    outs = []
    for g in range(WA_KV_HEADS):
        ks = slice(WA_HEAD * g, WA_HEAD * (g + 1))
        qg = jnp.concatenate([q[:, WA_HEAD * (WA_GROUP * g + j):WA_HEAD * (WA_GROUP * g + j + 1)]
                              for j in range(WA_GROUP)], axis=0)
        kx = kx_ref[0][:, ks].astype(BF16)
        vx = vx_ref[0][:, ks].astype(BF16)
        if band:
            kb = jnp.concatenate([kp_ref[0][:, ks], kc_ref[0][:, ks], kn_ref[0][:, ks], kx], axis=0)
            vb = jnp.concatenate([vp_ref[0][:, ks].astype(BF16), vc_ref[0][:, ks].astype(BF16),
                                  vn_ref[0][:, ks].astype(BF16), vx], axis=0)
        else:
            kb, vb = kx, vx
        s = jnp.einsum('qd,kd->qk', qg, kb, preferred_element_type=F32) * WA_SCALE
        if band:
            nk = s.shape[1]
            qi = lax.broadcasted_iota(jnp.int32, (tq, nk), 0)
            kj = lax.broadcasted_iota(jnp.int32, (tq, nk), 1)
            rel = kj - BLOCK - qi
            ok = (jnp.abs(rel) <= WINDOW) & ((kj >= BLOCK) | (n > 0)) & ((kj < 2 * BLOCK) | (n < nb - 1))
            ok = ok | (kj >= 3 * BLOCK)
            ok = jnp.concatenate([ok] * WA_GROUP, axis=0)
            s = jnp.where(ok, s, -jnp.inf)
        sk = jnp.concatenate([jnp.full((tq, 1), sink_ref[0, WA_GROUP * g + j], F32) for j in range(WA_GROUP)], axis=0)
        m = jnp.maximum(jnp.max(s, axis=-1, keepdims=True), sk)
        e = jnp.exp(s - m)
        inv = 1.0 / (jnp.sum(e, axis=-1, keepdims=True) + jnp.exp(sk - m))
        og = jnp.dot((e * inv).astype(BF16), vb, preferred_element_type=F32)
        outs += [og[tq * j:tq * (j + 1)] for j in range(WA_GROUP)]
    o_ref[0] = jnp.concatenate(outs, axis=-1)


def _window_attn(wq, wk, p, wkx, px, sink, band):
    bsz, t, _ = wq.shape
    cx = wkx.shape[1]
    tq = BLOCK if band else t
    nb = t // tq
    vblk = _cblk("wv")
    kvw = WA_KV_HEADS * WA_HEAD
    ins = [wq]
    in_specs = [pl.BlockSpec((1, tq, WA_HEADS * WA_HEAD), lambda b, n: (b, n, 0))]
    if band:
        idx = (lambda n: jnp.maximum(n - 1, 0), lambda n: n, lambda n: jnp.minimum(n + 1, nb - 1))
        ins += [wk] * 3 + [p] * 3
        in_specs += [pl.BlockSpec((1, tq, kvw), functools.partial(lambda f, b, n: (b, f(n), 0), f)) for f in idx]
        in_specs += [pl.BlockSpec((1, tq, kvw), functools.partial(lambda f, b, n: (b, f(n), vblk), f)) for f in idx]
    ins += [wkx, px, sink.reshape(1, WA_HEADS)]
    in_specs += [pl.BlockSpec((1, cx, kvw), lambda b, n: (b, 0, 0)),
                 pl.BlockSpec((1, cx, kvw), lambda b, n: (b, 0, vblk)),
                 pl.BlockSpec(memory_space=pltpu.SMEM)]
    return pl.pallas_call(
        functools.partial(_wattn_kernel, band=band, nb=nb),
        grid=(bsz, nb),
        in_specs=in_specs,
        out_specs=pl.BlockSpec((1, tq, WA_HEADS * WA_HEAD), lambda b, n: (b, n, 0)),
        out_shape=jax.ShapeDtypeStruct((bsz, t, WA_HEADS * WA_HEAD), F32),
        compiler_params=_params("parallel", "parallel"),
        name="window_attn" if band else "context_gqa",
    )(*ins)


MLA_HPS = 2


def _mla_kernel(*refs, nkt, extra):
    if extra:
        q_ref, kt_ref, v_ref, ktx_ref, vx_ref, o_ref = refs
    else:
        q_ref, kt_ref, v_ref, o_ref = refs
    tq = q_ref.shape[2]
    tk = kt_ref.shape[4]
    outs = []
    for hh in range(MLA_HPS):
        q = q_ref[0, hh]

        def update(carry, kt, v):
            m, l, acc = carry
            s = jnp.dot(q, kt, preferred_element_type=F32) * MLA_SCALE
            m_new = jnp.maximum(m, jnp.max(s, axis=-1, keepdims=True))
            corr = jnp.exp(m - m_new)
            pexp = jnp.exp(s - m_new)
            l = l * corr + jnp.sum(pexp, axis=-1, keepdims=True)
            acc = acc * corr + jnp.dot(pexp.astype(BF16), v, preferred_element_type=F32)
            return m_new, l, acc

        def body(j, carry):
            start = pl.multiple_of(j * tk, tk)
            return update(carry, kt_ref[0, hh, j], v_ref[0, hh, pl.ds(start, tk), :])

        carry = (jnp.full((tq, 1), -jnp.inf, F32), jnp.zeros((tq, 1), F32), jnp.zeros((tq, MLA_V), F32))
        carry = lax.fori_loop(0, nkt, body, carry)
        if extra:
            carry = update(carry, ktx_ref[0, hh, 0], vx_ref[0, hh])
        m, l, acc = carry
        outs.append(acc / l)
    o_ref[0] = jnp.concatenate(outs, axis=-1)


def _mla_attn(qm, kmt, vm, kmtx=None, vmx=None):
    bsz, nh, t, _ = qm.shape
    nkt, tk = kmt.shape[2], kmt.shape[4]
    tkv = vm.shape[2]
    tq = min(512, t)
    extra = kmtx is not None
    ins = [qm, kmt, vm]
    in_specs = [pl.BlockSpec((1, MLA_HPS, tq, LANE), lambda b, h, i: (b, h, i, 0)),
                pl.BlockSpec((1, MLA_HPS, nkt, LANE, tk), lambda b, h, i: (b, h, 0, 0, 0)),
                pl.BlockSpec((1, MLA_HPS, tkv, MLA_V), lambda b, h, i: (b, h, 0, 0))]
    if extra:
        cx = vmx.shape[2]
        ins += [kmtx, vmx]
        in_specs += [pl.BlockSpec((1, MLA_HPS, 1, LANE, cx), lambda b, h, i: (b, h, 0, 0, 0)),
                     pl.BlockSpec((1, MLA_HPS, cx, MLA_V), lambda b, h, i: (b, h, 0, 0))]
    return pl.pallas_call(
        functools.partial(_mla_kernel, nkt=nkt, extra=extra),
        grid=(bsz, nh // MLA_HPS, t // tq),
        in_specs=in_specs,
        out_specs=pl.BlockSpec((1, tq, MLA_HPS * MLA_V), lambda b, h, i: (b, i, h)),
        out_shape=jax.ShapeDtypeStruct((bsz, t, nh * MLA_V), F32),
        compiler_params=_params("parallel", "parallel", "parallel"),
        name="mla_attn",
    )(*ins)


def _merge_kernel(x_ref, g_ref, o_ref, bonus_ref, yb_ref, yc_ref, yd_ref, z_ref, gate_ref,
                  gnw_ref, gnb_ref, wb_ref, wo_ref, fnw_ref, out_ref, *, final):
    o = o_ref[0, 0] + o_ref[0, 1]
    mu = jnp.mean(o, axis=-1, keepdims=True)
    oc = o - mu
    var = jnp.mean(oc * oc, axis=-1, keepdims=True)
    on = oc * lax.rsqrt(var + RW_GN_EPS)
    ya = jnp.concatenate([on[h] for h in range(RW_HEADS)], axis=-1) * gnw_ref[...] + gnb_ref[...]
    ya = ya + jnp.concatenate([bonus_ref[0, h] for h in range(RW_HEADS)], axis=-1)
    ys = (ya, yb_ref[0], yc_ref[0], yd_ref[0])
    m = None
    for i, y in enumerate(ys):
        zi = z_ref[0, :, BRANCH_W * i:BRANCH_W * (i + 1)]
        gi = gate_ref[0, :, D_MODEL * i:D_MODEL * (i + 1)]
        yz = (y * (zi * jax.nn.sigmoid(zi))).astype(BF16)
        term = jax.nn.sigmoid(gi) * jnp.dot(yz, wb_ref[i], preferred_element_type=F32)
        m = term if m is None else m + term
    xn = x_ref[0] + g_ref[0] * jnp.dot(m.astype(BF16), wo_ref[...], preferred_element_type=F32)
    if final:
        xn = _rms(xn, fnw_ref[...])
    out_ref[0] = xn


def _merge(x, g, o, bonus, yb, yc, yd, p, gn_w, gn_b, w_branch, w_out, final_norm_w, final):
    bsz, t, d = x.shape
    tm = min(256, t)
    zblk, gblk = _cblk("z"), _cblk("gate")
    row = lambda w: pl.BlockSpec((1, tm, w), lambda b, i: (b, i, 0))
    return pl.pallas_call(
        functools.partial(_merge_kernel, final=final),
        grid=(bsz, t // tm),
        in_specs=[row(d),
                  pl.BlockSpec((1, 1, d), lambda b, i: (b, 0, 0)),
                  pl.BlockSpec((1, 2, RW_HEADS, tm, RW_HEAD), lambda b, i: (b, 0, 0, i, 0)),
                  pl.BlockSpec((1, RW_HEADS, tm, RW_HEAD), lambda b, i: (b, 0, i, 0)),
                  row(BRANCH_W), row(BRANCH_W), row(BRANCH_W),
                  pl.BlockSpec((1, tm, COL["z"][1]), lambda b, i: (b, i, zblk)),
                  pl.BlockSpec((1, tm, COL["gate"][1]), lambda b, i: (b, i, gblk)),
                  pl.BlockSpec((1, RW_C), lambda b, i: (0, 0)),
                  pl.BlockSpec((1, RW_C), lambda b, i: (0, 0)),
                  pl.BlockSpec((N_BRANCH, BRANCH_W, d), lambda b, i: (0, 0, 0)),
                  pl.BlockSpec((d, d), lambda b, i: (0, 0)),
                  pl.BlockSpec((1, d), lambda b, i: (0, 0))],
        out_specs=row(d),
        out_shape=jax.ShapeDtypeStruct((bsz, t, d), F32),
        compiler_params=_params("parallel", "parallel"),
        name="merge",
    )(x, g, o, bonus, yb, yc, yd, p, p, gn_w.reshape(1, RW_C), gn_b.reshape(1, RW_C),
      w_branch, w_out, final_norm_w.reshape(1, d))


MLA_TK = 512


def _permute_q_up(q_up):
    r = q_up.shape[0]
    w = q_up.reshape(r, MLA_HEADS, MLA_NOPE + MLA_ROPE)
    nope = w[:, :, :MLA_NOPE].reshape(r, MLA_HEADS * MLA_NOPE)
    rope = jnp.pad(w[:, :, MLA_NOPE:], ((0, 0), (0, 0), (0, MLA_NOPE - MLA_ROPE))).reshape(r, MLA_HEADS * MLA_NOPE)
    return jnp.concatenate([nope, rope], axis=-1)


def _permute_kv_up(kv_up):
    r = kv_up.shape[0]
    w = kv_up.reshape(r, MLA_HEADS, MLA_NOPE + MLA_V)
    return jnp.concatenate([w[:, :, :MLA_NOPE].reshape(r, -1), w[:, :, MLA_NOPE:].reshape(r, -1)], axis=-1)


def kernel(x, c, ctx, c_ctx, norm_w, ada_w, ada_b, w_in, rwkv_mu, rwkv_w0, rwkv_w_up, rwkv_a0, rwkv_a_up, rwkv_k_k, rwkv_k_a, rwkv_r_k, rwkv_gn_w, rwkv_gn_b, conv_w, conv_b, conv_ln_w, conv_ln_b, attn_sink, mla_q_norm, mla_q_up, mla_kv_norm, mla_kv_up, w_branch, w_out, final_norm_w):
    bsz, t, d = x.shape
    n_ctx = ctx.shape[1]
    depth = norm_w.shape[0]
    assert t % MLA_TK == 0 and t % BLOCK == 0 and n_ctx % RW_CHUNK == 0 and n_ctx <= MLA_TK

    cvec = jnp.concatenate([c, c_ctx[None, :], jnp.zeros((SUBLANE - bsz - 1, d), F32)], axis=0)
    mod = _modulation(cvec, ada_w, ada_b)
    tables = _rope_tables(t)
    s_zero = jnp.zeros((bsz, 2, RW_HEADS, RW_HEAD, RW_HEAD), F32)

    for l in range(depth):
        last = l == depth - 1
        mx = mod[l, :bsz][:, None, :]
        mc = jnp.broadcast_to(mod[l, bsz][None, None, :], (bsz, 1, 3 * d))
        sh_x, sc_x, g_x = mx[..., :d], mx[..., d:2 * d], mx[..., 2 * d:]
        sh_c, sc_c, g_c = mc[..., :d], mc[..., d:2 * d], mc[..., 2 * d:]
        w_p = _pack_cols(w_in[l]).astype(BF16)
        px = _in_proj(x, norm_w[l], sc_x, sh_x, w_p)
        pc = _in_proj(ctx, norm_w[l], sc_c, sh_c, w_p)

        mu_p = jnp.concatenate([rwkv_mu[l]], axis=0)
        rw_args = (mu_p, rwkv_w0[l], rwkv_w_up[l], rwkv_a0[l], rwkv_a_up[l], rwkv_k_k[l], rwkv_k_a[l],
                   rwkv_r_k[l].reshape(RW_C))
        o_c, bonus_c, s_ctx = _rwkv(pc, s_zero, *rw_args)
        o_x, bonus_x, _ = _rwkv(px, s_ctx, *rw_args)

        conv_args = (conv_w[l], conv_b[l], conv_ln_w[l], conv_ln_b[l])
        yb_x = _conv(px, *conv_args)

        q_up_p = _permute_q_up(mla_q_up[l]).astype(BF16)
        kv_up_p = _permute_kv_up(mla_kv_up[l]).astype(BF16)
        prep_w = (mla_q_norm[l], q_up_p, mla_kv_norm[l], kv_up_p)
        qm_x, kmt_x, vm_x, wq_x, wk_x = _attn_prep(px, tables, *prep_w, tk=MLA_TK)
        qm_c, kmt_c, vm_c, wq_c, wk_c = _attn_prep(pc, None, *prep_w, tk=n_ctx)

        yc_x = _window_attn(wq_x, wk_x, px, wk_c, pc, attn_sink[l], band=True)
        yd_x = _mla_attn(qm_x, kmt_x, vm_x, kmt_c, vm_c)

        merge_w = (rwkv_gn_w[l], rwkv_gn_b[l], w_branch[l].astype(BF16), w_out[l].astype(BF16), final_norm_w)
        x = _merge(x, g_x, o_x, bonus_x, yb_x, yc_x, yd_x, px, *merge_w, final=last)

        if not last:
            yb_c = _conv(pc, *conv_args)
            yc_c = _window_attn(wq_c, wk_c, pc, wk_c, pc, attn_sink[l], band=False)
            yd_c = _mla_attn(qm_c, kmt_c, vm_c)
            ctx = _merge(ctx, g_c, o_c, bonus_c, yb_c, yc_c, yd_c, pc, *merge_w, final=False)

    return x
```

```python
import functools

import jax
import jax.numpy as jnp
from jax import lax
from jax.experimental import pallas as pl
from jax.experimental.pallas import tpu as pltpu

F32 = jnp.float32
BF16 = jnp.bfloat16

D_MODEL = 1024
GRID_W = 64
N_BRANCH = 4
BRANCH_W = 512
RW_HEADS = 8
RW_HEAD = 64
RW_C = RW_HEADS * RW_HEAD
RW_LORA = 64
RW_GN_EPS = 64e-5
CONV_CH = 512
CONV_K = 31
CONV_LN_EPS = 1e-5
WA_HEADS = 8
WA_KV_HEADS = 2
WA_HEAD = 64
WA_GROUP = WA_HEADS // WA_KV_HEADS
WINDOW = 128
BLOCK = 128
WA_SCALE = WA_HEAD ** -0.5
MLA_HEADS = 8
MLA_Q_RANK = 256
MLA_KV_RANK = 128
MLA_NOPE = 64
MLA_ROPE = 32
MLA_V = 64
MLA_SCALE = (MLA_NOPE + MLA_ROPE) ** -0.5
MLA_QSCALE = MLA_SCALE * 1.4426950408889634
ROPE_BASE = 10000.0
NORM_EPS = 1e-6

LANE = 128
SUBLANE = 8
VMEM_LIMIT = 48 * 1024 * 1024

_SRC = {}
_off = 0
for _name, _size in (("r", RW_C), ("k", RW_C), ("v", RW_C), ("wd", RW_LORA), ("ad", RW_LORA),
                     ("cv", 2 * CONV_CH), ("wq", WA_HEADS * WA_HEAD), ("wk", WA_KV_HEADS * WA_HEAD),
                     ("wv", WA_KV_HEADS * WA_HEAD), ("dq", MLA_Q_RANK), ("dkv", MLA_KV_RANK),
                     ("kr", MLA_ROPE), ("z", N_BRANCH * BRANCH_W), ("gate", N_BRANCH * D_MODEL)):
    _SRC[_name] = (_off, _size)
    _off += _size
N_IN = _off

_PACK32 = (("r", 512), ("k", 512), ("v", 512))
_PACK16 = (("gate", 4096), ("z", 2048), ("cv", 1024), ("wq", 512), ("dq", 256), ("wdad", 128),
           ("wk", 128), ("wv", 128), ("dkv", 128), ("kr", 128), ("pad", 128))
_PACK = _PACK32 + _PACK16
COL = {}
for _pack in (_PACK32, _PACK16):
    _off = 0
    for _name, _size in _pack:
        assert _off % _size == 0
        COL[_name] = (_off, _size)
        _off += _size
N32 = sum(s for _, s in _PACK32)
N16 = sum(s for _, s in _PACK16)
N_PACK = N32 + N16


def _cblk(name):
    off, size = COL[name]
    return off // size


def _pack_cols(w):
    def src(name):
        o, s = _SRC[name]
        return w[..., o:o + s]
    pieces = []
    for name, size in _PACK:
        if name == "wdad":
            pieces += [src("wd"), src("ad")]
        elif name == "kr":
            pieces += [src("kr"), jnp.zeros(w.shape[:-1] + (size - MLA_ROPE,), w.dtype)]
        elif name == "pad":
            pieces.append(jnp.zeros(w.shape[:-1] + (size,), w.dtype))
        else:
            pieces.append(src(name))
    return jnp.concatenate(pieces, axis=-1)


def _params(*sem):
    return pltpu.CompilerParams(dimension_semantics=sem, vmem_limit_bytes=VMEM_LIMIT)


def _sigmoid(x):
    return 0.5 * jnp.tanh(0.5 * x) + 0.5


def _split2(x):
    hi = x.astype(BF16)
    lo = (x - hi.astype(F32)).astype(BF16)
    return hi, lo


def _split3(x):
    hi = x.astype(BF16)
    r1 = x - hi.astype(F32)
    mid = r1.astype(BF16)
    lo = (r1 - mid.astype(F32)).astype(BF16)
    return hi, mid, lo


def _ein(spec, a, b, passes=1):
    def e(x, y):
        return jnp.einsum(spec, x, y, preferred_element_type=F32)
    if passes == 1:
        return e(a.astype(BF16), b.astype(BF16))
    ah, al = _split2(a)
    bh, bl = _split2(b)
    return e(ah, bh) + (e(ah, bl) + e(al, bh))


def _mod_kernel(c_ref, w_ref, b_ref, o_ref):
    cv = c_ref[...]
    sc = cv * jax.nn.sigmoid(cv)
    o_ref[0] = jnp.dot(sc, w_ref[0], preferred_element_type=F32,
                       precision=lax.Precision.HIGHEST) + b_ref[0]


def _modulation(cvec, ada_w, ada_b):
    depth, d, d3 = ada_w.shape
    rows = cvec.shape[0]
    tn = 512
    return pl.pallas_call(
        _mod_kernel,
        grid=(depth, d3 // tn),
        in_specs=[pl.BlockSpec((rows, d), lambda l, j: (0, 0)),
                  pl.BlockSpec((1, d, tn), lambda l, j: (l, 0, j)),
                  pl.BlockSpec((1, 1, tn), lambda l, j: (l, 0, j))],
        out_specs=pl.BlockSpec((1, rows, tn), lambda l, j: (l, 0, j)),
        out_shape=jax.ShapeDtypeStruct((depth, rows, d3), F32),
        compiler_params=_params("parallel", "parallel"),
        name="modulation",
    )(cvec, ada_w, ada_b.reshape(depth, 1, d3))


INPROJ_TN = 512
N32_TILES = N32 // INPROJ_TN


def _inproj_kernel(x_ref, nw_ref, sc_ref, sh_ref, w_ref, o32_ref, o16_ref, h_ref):
    j = pl.program_id(2)

    @pl.when(j == 0)
    def _():
        xf = x_ref[0]
        ms = jnp.mean(xf * xf, axis=-1, keepdims=True)
        hn = (xf * lax.rsqrt(ms + NORM_EPS)) * nw_ref[...]
        h_ref[...] = (hn * (1.0 + sc_ref[0]) + sh_ref[0]).astype(BF16)

    @pl.when(j < N32_TILES)
    def _():
        o32_ref[0] = jnp.dot(h_ref[...], w_ref[...], preferred_element_type=F32)

    @pl.when(j >= N32_TILES)
    def _():
        o16_ref[0] = jnp.dot(h_ref[...], w_ref[...], preferred_element_type=F32).astype(BF16)


def _in_proj(x, norm_w, scale, shift, w_packed):
    bsz, t, d = x.shape
    tm = min(1024, t)
    tn = INPROJ_TN
    return pl.pallas_call(
        _inproj_kernel,
        grid=(bsz, t // tm, N_PACK // tn),
        in_specs=[pl.BlockSpec((1, tm, d), lambda b, i, j: (b, i, 0)),
                  pl.BlockSpec((1, d), lambda b, i, j: (0, 0)),
                  pl.BlockSpec((1, 1, d), lambda b, i, j: (b, 0, 0)),
                  pl.BlockSpec((1, 1, d), lambda b, i, j: (b, 0, 0)),
                  pl.BlockSpec((d, tn), lambda b, i, j: (0, j))],
        out_specs=[pl.BlockSpec((1, tm, tn), lambda b, i, j: (b, i, jnp.minimum(j, N32_TILES - 1))),
                   pl.BlockSpec((1, tm, tn), lambda b, i, j: (b, i, jnp.maximum(j - N32_TILES, 0)))],
        out_shape=[jax.ShapeDtypeStruct((bsz, t, N32), F32),
                   jax.ShapeDtypeStruct((bsz, t, N16), BF16)],
        scratch_shapes=[pltpu.VMEM((tm, d), BF16)],
        compiler_params=_params("parallel", "parallel", "arbitrary"),
        name="in_proj",
    )(x, norm_w.reshape(1, d), scale, shift, w_packed)


RW_CHUNK = 64
P_DATA = 1
P_STATE = 3


def _heads_major(x):
    return jnp.stack([x[:, RW_HEAD * h:RW_HEAD * (h + 1)] for h in range(RW_HEADS)], axis=0)


def _rwkv_p1_kernel(r_ref, k_ref, v_ref, wa_ref, rp_ref, kp_ref, vp_ref, wap_ref,
                    rn_ref, kn_ref, vn_ref, wan_ref,
                    mur_ref, muk_ref, muv_ref, muwa_ref, w0_ref, wup_ref, a0_ref, aup_ref,
                    kkw_ref, kaw_ref, rkw_ref,
                    qt_ref, o0_ref, g_ref, h_ref, bonus_ref, *, nc):
    L = RW_CHUNK
    c = pl.program_id(1)

    def shift_mix(cur_ref, p_ref, n_ref, mu_ref):
        cur = cur_ref[0].astype(F32)
        hr = p_ref.shape[1]
        prow = jnp.where(c > 0, p_ref[0, hr - 1:hr, :].astype(F32), 0.0)
        nrow = jnp.where(c < nc - 1, n_ref[0, 0:1, :].astype(F32), 0.0)
        ridx = lax.broadcasted_iota(jnp.int32, cur.shape, 0)
        fprev = jnp.where(ridx == 0, prow, pltpu.roll(cur, 1, 0))
        fnext = jnp.where(ridx == L - 1, nrow, pltpu.roll(cur, L - 1, 0))
        return cur + mu_ref[...] * (0.5 * (fprev + fnext) - cur)

    r = shift_mix(r_ref, rp_ref, rn_ref, mur_ref)
    k = shift_mix(k_ref, kp_ref, kn_ref, muk_ref)
    v = shift_mix(v_ref, vp_ref, vn_ref, muv_ref)
    wa = shift_mix(wa_ref, wap_ref, wan_ref, muwa_ref)
    wd_t = jnp.tanh(wa[:, :RW_LORA]).astype(BF16)
    ad = wa[:, RW_LORA:].astype(BF16)

    r_h = _heads_major(r)
    k_h = _heads_major(k)
    v_h = _heads_major(v)
    kk_h = k_h * _heads_major(kkw_ref[...])
    kk_h = kk_h / jnp.maximum(jnp.sqrt(jnp.sum(kk_h * kk_h, axis=-1, keepdims=True)), 1e-12)
    ka_h = _heads_major(kaw_ref[...])
    rk_h = _heads_major(rkw_ref[...])

    row = lax.broadcasted_iota(jnp.int32, (L, L), 0)
    col = lax.broadcasted_iota(jnp.int32, (L, L), 1)
    eye = (row == col)[None]
    eye_n = (lax.broadcasted_iota(jnp.int32, (RW_HEAD, RW_HEAD), 0)
             == lax.broadcasted_iota(jnp.int32, (RW_HEAD, RW_HEAD), 1))[None]
    kdir_sum = None
    for d in range(2):
        incl2 = (col <= row) if d == 0 else (col >= row)
        strict = ((col < row) if d == 0 else (col > row))[None]
        incl = incl2[None]
        last = L - 1 if d == 0 else 0

        w_pre = w0_ref[d:d + 1, :] + jnp.dot(wd_t, wup_ref[d].astype(BF16), preferred_element_type=F32)
        w_log = -(jnp.maximum(-w_pre, 0.0) + jnp.log1p(jnp.exp(-jnp.abs(w_pre)))) - 0.5
        logw = -jnp.exp(w_log)
        lh, lm, ll = _split3(logw)
        tri = incl2.astype(BF16)
        cum = (jnp.dot(tri, lh, preferred_element_type=F32)
               + (jnp.dot(tri, lm, preferred_element_type=F32) + jnp.dot(tri, ll, preferred_element_type=F32)))
        a = jax.nn.sigmoid(a0_ref[d:d + 1, :] + jnp.dot(ad, aup_ref[d].astype(BF16), preferred_element_type=F32))

        logw_h = _heads_major(logw)
        cum_h = _heads_major(cum)
        a_h = _heads_major(a)
        cum_l = cum_h[:, last:last + 1, :]
        e_pos = jnp.exp(cum_h)
        e_in = jnp.exp(cum_h - logw_h)
        e_neg = jnp.exp(-cum_h)
        e_l = jnp.exp(cum_l - cum_h)
        w_l = jnp.exp(cum_l)

        kdir = k_h * (1.0 + (a_h - 1.0) * ka_h)
        kdir_sum = kdir if kdir_sum is None else kdir_sum + kdir
        b_h = kk_h * a_h
        al = kk_h * e_in
        rho = r_h * e_pos
        be = b_h * e_neg
        ka = kdir * e_neg
        be_l = b_h * e_l
        ka_l = kdir * e_l

        ar = jnp.concatenate([al, rho], axis=1)
        mb = _ein('hld,hmd->hlm', ar, be, P_DATA)
        mk = _ein('hld,hmd->hlm', ar, ka, P_DATA)
        nmat = jnp.where(strict, mb[:, :L], 0.0)
        aks = jnp.where(strict, mk[:, :L], 0.0)
        rbi = jnp.where(incl, mb[:, L:], 0.0)
        rki = jnp.where(incl, mk[:, L:], 0.0)

        tmat = jnp.where(eye, 1.0, 0.0) - nmat
        pw = nmat
        steps = L.bit_length() - 1
        for s in range(1, steps):
            pw = _ein('hij,hjk->hik', pw, pw, P_DATA)
            tmat = tmat + _ein('hij,hjk->hik', tmat, pw, P_DATA)

        akv = _ein('hlm,hmv->hlv', aks, v_h, P_DATA)
        ua = _ein('hlm,hmd->hld', tmat, al, P_DATA)
        d0 = -_ein('hlm,hmv->hlv', tmat, akv, P_DATA)
        qt = rho - _ein('hlm,hmd->hld', rbi, ua, P_DATA)
        o0 = _ein('hlm,hmv->hlv', rbi, d0, P_DATA) + _ein('hlm,hmv->hlv', rki, v_h, P_DATA)
        be_lt = jnp.swapaxes(be_l, 1, 2)
        ka_lt = jnp.swapaxes(ka_l, 1, 2)
        g = jnp.where(eye_n, w_l, 0.0) -_ein('hkl,hld->hkd', be_lt, ua, P_DATA)
        hh = _ein('hkl,hlv->hkv', be_lt, d0, P_DATA) + _ein('hkl,hlv->hkv', ka_lt, v_h, P_DATA)

        qt_ref[0, d] = qt
        o0_ref[0, d] = o0
        g_ref[0, d, 0] = g
        h_ref[0, d, 0] = hh

    bonus_ref[0] = jnp.sum(r_h * kdir_sum * rk_h, axis=-1, keepdims=True) * v_h


def _rwkv_p2_kernel(qt_ref, o0_ref, g_ref, h_ref, s0_ref, o_ref, sfin_ref, s_ref, *, nsteps, cps):
    L = RW_CHUNK
    d = pl.program_id(1)
    c = pl.program_id(2)

    @pl.when(c == 0)
    def _():
        s_ref[...] = s0_ref[0, 0]

    def body(i, s):
        ci = i + d * (cps - 1 - 2 * i)
        rows = pl.ds(pl.multiple_of(ci * L, L), L)
        o_ref[0, 0, :, rows, :] = (_ein('hlk,hkv->hlv', qt_ref[0, 0, :, rows, :], s, P_STATE)
                                   + o0_ref[0, 0, :, rows, :])
        return _ein('hkj,hjv->hkv', g_ref[0, 0, ci], s, P_STATE) + h_ref[0, 0, ci]

    s_new = lax.fori_loop(0, cps, body, s_ref[...])
    s_ref[...] = s_new

    @pl.when(c == nsteps - 1)
    def _():
        sfin_ref[0, 0] = s_new


RW_SCAN_CHUNKS = 8
HALO16 = 2 * SUBLANE


def _rwkv(p32, p16, s0, mu, w0, w_up, a0, a_up, k_k, k_a, r_k):
    bsz, t, _ = p32.shape
    L = RW_CHUNK
    nc = t // L

    def halo_rows(name):
        return HALO16 if name == "wdad" else SUBLANE

    def cur(name):
        blk = _cblk(name)
        w = COL[name][1]
        return pl.BlockSpec((1, L, w), lambda b, c: (b, c, blk))

    def prev(name):
        blk = _cblk(name)
        w = COL[name][1]
        hr = halo_rows(name)
        return pl.BlockSpec((1, hr, w), lambda b, c: (b, jnp.maximum(c * (L // hr) - 1, 0), blk))

    def nxt(name):
        blk = _cblk(name)
        w = COL[name][1]
        hr = halo_rows(name)
        return pl.BlockSpec((1, hr, w), lambda b, c: (b, jnp.minimum((c + 1) * (L // hr), t // hr - 1), blk))

    def full(a):
        nd = a.ndim
        return pl.BlockSpec(a.shape, lambda b, c: (0,) * nd)

    names = ("r", "k", "v", "wdad")
    srcs = [p32, p32, p32, p16]
    mu_p = [mu[o:o + s].reshape(1, s) for o, s in
            ((0, RW_C), (RW_C, RW_C), (2 * RW_C, RW_C), (3 * RW_C, 2 * RW_LORA))]
    small = mu_p + [w0, w_up, a0, a_up, k_k.reshape(1, RW_C), k_a.reshape(1, RW_C), r_k.reshape(1, RW_C)]
    hm = (RW_HEADS, L, RW_HEAD)
    qt, o0, g, h, bonus = pl.pallas_call(
        functools.partial(_rwkv_p1_kernel, nc=nc),
        grid=(bsz, nc),
        in_specs=[cur(n) for n in names] + [prev(n) for n in names] + [nxt(n) for n in names]
                 + [full(a) for a in small],
        out_specs=[pl.BlockSpec((1, 2) + hm, lambda b, c: (b, 0, 0, c, 0)),
                   pl.BlockSpec((1, 2) + hm, lambda b, c: (b, 0, 0, c, 0)),
                   pl.BlockSpec((1, 2, 1, RW_HEADS, RW_HEAD, RW_HEAD), lambda b, c: (b, 0, c, 0, 0, 0)),
                   pl.BlockSpec((1, 2, 1, RW_HEADS, RW_HEAD, RW_HEAD), lambda b, c: (b, 0, c, 0, 0, 0)),
                   pl.BlockSpec((1,) + hm, lambda b, c: (b, 0, c, 0))],
        out_shape=[jax.ShapeDtypeStruct((bsz, 2, RW_HEADS, t, RW_HEAD), F32),
                   jax.ShapeDtypeStruct((bsz, 2, RW_HEADS, t, RW_HEAD), F32),
                   jax.ShapeDtypeStruct((bsz, 2, nc, RW_HEADS, RW_HEAD, RW_HEAD), F32),
                   jax.ShapeDtypeStruct((bsz, 2, nc, RW_HEADS, RW_HEAD, RW_HEAD), F32),
                   jax.ShapeDtypeStruct((bsz, RW_HEADS, t, RW_HEAD), F32)],
        compiler_params=_params("parallel", "parallel"),
        name="rwkv_chunk_prep",
    )(*(srcs * 3), *small)

    cps = min(RW_SCAN_CHUNKS, nc)
    nsteps = nc // cps

    def cc(c, d):
        return c + d * (nsteps - 1 - 2 * c)

    hmb = (RW_HEADS, cps * L, RW_HEAD)
    gblk = (1, 1, cps, RW_HEADS, RW_HEAD, RW_HEAD)
    o, s_fin = pl.pallas_call(
        functools.partial(_rwkv_p2_kernel, nsteps=nsteps, cps=cps),
        grid=(bsz, 2, nsteps),
        in_specs=[pl.BlockSpec((1, 1) + hmb, lambda b, d, c: (b, d, 0, cc(c, d), 0)),
                  pl.BlockSpec((1, 1) + hmb, lambda b, d, c: (b, d, 0, cc(c, d), 0)),
                  pl.BlockSpec(gblk, lambda b, d, c: (b, d, cc(c, d), 0, 0, 0)),
                  pl.BlockSpec(gblk, lambda b, d, c: (b, d, cc(c, d), 0, 0, 0)),
                  pl.BlockSpec((1, 1, RW_HEADS, RW_HEAD, RW_HEAD), lambda b, d, c: (b, d, 0, 0, 0))],
        out_specs=[pl.BlockSpec((1, 1) + hmb, lambda b, d, c: (b, d, 0, cc(c, d), 0)),
                   pl.BlockSpec((1, 1, RW_HEADS, RW_HEAD, RW_HEAD), lambda b, d, c: (b, d, 0, 0, 0))],
        out_shape=[jax.ShapeDtypeStruct((bsz, 2, RW_HEADS, t, RW_HEAD), F32),
                   jax.ShapeDtypeStruct((bsz, 2, RW_HEADS, RW_HEAD, RW_HEAD), F32)],
        scratch_shapes=[pltpu.VMEM((RW_HEADS, RW_HEAD, RW_HEAD), F32)],
        compiler_params=_params("parallel", "parallel", "arbitrary"),
        name="rwkv_state_scan",
    )(qt, o0, g, h, s0)
    return o, bonus, s_fin


CONV_HALO = 16


def _conv_kernel(cur_ref, prev_ref, next_ref, w_ref, b_ref, lnw_ref, lnb_ref, o_ref, u_ref, *, nt, tt):
    i = pl.program_id(1)

    def glu(x):
        x = x.astype(F32)
        return x[:, :CONV_CH] * _sigmoid(x[:, CONV_CH:])

    u_ref[0:CONV_HALO, :] = jnp.where(i > 0, glu(prev_ref[0]), 0.0)
    u_ref[CONV_HALO:CONV_HALO + tt, :] = glu(cur_ref[0])
    u_ref[CONV_HALO + tt:, :] = jnp.where(i < nt - 1, glu(next_ref[0]), 0.0)
    acc = jnp.zeros((tt, CONV_CH), F32) + b_ref[...]
    base = CONV_HALO - CONV_K // 2
    for j in range(CONV_K):
        acc = acc + u_ref[base + j:base + j + tt, :] * w_ref[j:j + 1, :]
    mu = jnp.mean(acc, axis=-1, keepdims=True)
    xc = acc - mu
    var = jnp.mean(xc * xc, axis=-1, keepdims=True)
    un = (xc * lax.rsqrt(var + CONV_LN_EPS)) * lnw_ref[...] + lnb_ref[...]
    o_ref[0] = un * _sigmoid(un)


def _conv(p, conv_w, conv_b, ln_w, ln_b):
    bsz, t, _ = p.shape
    tt = min(512, t)
    nt = t // tt
    hb = tt // CONV_HALO
    nh = t // CONV_HALO
    blk = _cblk("cv")
    w = COL["cv"][1]
    vec = lambda a: a.reshape(1, CONV_CH)
    return pl.pallas_call(
        functools.partial(_conv_kernel, nt=nt, tt=tt),
        grid=(bsz, nt),
        in_specs=[pl.BlockSpec((1, tt, w), lambda b, i: (b, i, blk)),
                  pl.BlockSpec((1, CONV_HALO, w), lambda b, i: (b, jnp.maximum(i * hb - 1, 0), blk)),
                  pl.BlockSpec((1, CONV_HALO, w), lambda b, i: (b, jnp.minimum((i + 1) * hb, nh - 1), blk)),
                  pl.BlockSpec((CONV_K, CONV_CH), lambda b, i: (0, 0)),
                  pl.BlockSpec((1, CONV_CH), lambda b, i: (0, 0)),
                  pl.BlockSpec((1, CONV_CH), lambda b, i: (0, 0)),
                  pl.BlockSpec((1, CONV_CH), lambda b, i: (0, 0))],
        out_specs=pl.BlockSpec((1, tt, CONV_CH), lambda b, i: (b, i, 0)),
        out_shape=jax.ShapeDtypeStruct((bsz, t, CONV_CH), F32),
        scratch_shapes=[pltpu.VMEM((tt + 2 * CONV_HALO, CONV_CH), F32)],
        compiler_params=_params("parallel", "parallel"),
        name="conv_module",
    )(p, p, p, conv_w, vec(conv_b), vec(ln_w), vec(ln_b))


def _swap_halves(x, hs):
    slabs = []
    for j in range(x.shape[-1] // LANE):
        s = x[:, LANE * j:LANE * (j + 1)]
        lane = lax.broadcasted_iota(jnp.int32, s.shape, 1)
        slabs.append(jnp.where((lane % (2 * hs)) < hs, pltpu.roll(s, LANE - hs, 1), pltpu.roll(s, hs, 1)))
    return slabs[0] if len(slabs) == 1 else jnp.concatenate(slabs, axis=-1)


def _rope(x, cos, sin, hs):
    reps = x.shape[-1] // LANE
    if reps > 1:
        cos = jnp.concatenate([cos] * reps, axis=-1)
        sin = jnp.concatenate([sin] * reps, axis=-1)
    return x * cos + _swap_halves(x, hs) * sin


def _rms(x, w):
    ms = jnp.mean(x * x, axis=-1, keepdims=True)
    return (x * lax.rsqrt(ms + NORM_EPS)) * w


def _prep_kernel(*refs, rope):
    if rope:
        (dq_ref, dkv_ref, kr_ref, wq_ref, wk_ref, c64_ref, s64_ref, c32_ref, s32_ref,
         qnw_ref, qup_ref, kvnw_ref, kvup_ref, qm_ref, kmt_ref, vm_ref, wqo_ref, wko_ref) = refs
    else:
        (dq_ref, dkv_ref, kr_ref, wq_ref, wk_ref,
         qnw_ref, qup_ref, kvnw_ref, kvup_ref, qm_ref, kmt_ref, vm_ref, wqo_ref, wko_ref) = refs
    hw = MLA_HEADS * MLA_NOPE
    q = jnp.dot(_rms(dq_ref[0].astype(F32), qnw_ref[...]).astype(BF16), qup_ref[...],
                preferred_element_type=F32)
    kv = jnp.dot(_rms(dkv_ref[0].astype(F32), kvnw_ref[...]).astype(BF16), kvup_ref[...],
                 preferred_element_type=F32)
    qn, qr = q[:, :hw], q[:, hw:]
    kn, vv = kv[:, :hw], kv[:, hw:]
    kr = kr_ref[0].astype(F32)
    wq = wq_ref[0].astype(F32)
    wk = wk_ref[0].astype(F32)
    if rope:
        qr = _rope(qr, c32_ref[...], s32_ref[...], MLA_ROPE // 4)
        kr = _rope(kr, c32_ref[...], s32_ref[...], MLA_ROPE // 4)
        wq = _rope(wq, c64_ref[...], s64_ref[...], WA_HEAD // 4)
        wk = _rope(wk, c64_ref[...], s64_ref[...], WA_HEAD // 4)
    wqo_ref[0] = wq.astype(BF16)
    wko_ref[0] = wk.astype(BF16)
    qn = qn * MLA_QSCALE
    qr = qr * MLA_QSCALE
    lane = lax.broadcasted_iota(jnp.int32, (qn.shape[0], MLA_V), 1)
    ones_col = jnp.where(lane == 0, 1.0, 0.0)
    for h in range(MLA_HEADS):
        sl = slice(MLA_NOPE * h, MLA_NOPE * (h + 1))
        qm_ref[0, h] = jnp.concatenate([qn[:, sl], qr[:, sl]], axis=-1).astype(BF16)
        kh = jnp.concatenate([kn[:, sl], kr[:, :MLA_NOPE]], axis=-1)
        kmt_ref[0, h, 0] = kh.T.astype(BF16)
        vm_ref[0, h] = jnp.concatenate([vv[:, sl], ones_col], axis=-1).astype(BF16)


def _attn_prep(p, tables, q_norm, q_up_p, kv_norm, kv_up_p, tk):
    bsz, t, _ = p.shape
    tm = tk
    rope = tables is not None

    def slab(name):
        blk = _cblk(name)
        return pl.BlockSpec((1, tm, COL[name][1]), lambda b, i: (b, i, blk))

    def full(a):
        nd = a.ndim
        return pl.BlockSpec(a.shape, lambda b, i: (0,) * nd)

    ins = [p] * 5
    in_specs = [slab(n) for n in ("dq", "dkv", "kr", "wq", "wk")]
    if rope:
        ins += list(tables)
        in_specs += [pl.BlockSpec((tm, LANE), lambda b, i: (i, 0))] * 4
    small = [q_norm.reshape(1, -1), q_up_p, kv_norm.reshape(1, -1), kv_up_p]
    ins += small
    in_specs += [full(a) for a in small]
    return pl.pallas_call(
        functools.partial(_prep_kernel, rope=rope),
        grid=(bsz, t // tm),
        in_specs=in_specs,
        out_specs=[pl.BlockSpec((1, MLA_HEADS, tm, LANE), lambda b, i: (b, 0, i, 0)),
                   pl.BlockSpec((1, MLA_HEADS, 1, LANE, tm), lambda b, i: (b, 0, i, 0, 0)),
                   pl.BlockSpec((1, MLA_HEADS, tm, LANE), lambda b, i: (b, 0, i, 0)),
                   pl.BlockSpec((1, tm, WA_HEADS * WA_HEAD), lambda b, i: (b, i, 0)),
                   pl.BlockSpec((1, tm, WA_KV_HEADS * WA_HEAD), lambda b, i: (b, i, 0))],
        out_shape=[jax.ShapeDtypeStruct((bsz, MLA_HEADS, t, LANE), BF16),
                   jax.ShapeDtypeStruct((bsz, MLA_HEADS, t // tm, LANE, tm), BF16),
                   jax.ShapeDtypeStruct((bsz, MLA_HEADS, t, LANE), BF16),
                   jax.ShapeDtypeStruct((bsz, t, WA_HEADS * WA_HEAD), BF16),
                   jax.ShapeDtypeStruct((bsz, t, WA_KV_HEADS * WA_HEAD), BF16)],
        compiler_params=_params("parallel", "parallel"),
        name="attn_prep",
    )(*ins)


def _rope_tables(t):
    pos = jnp.arange(t)
    row = (pos // GRID_W).astype(F32)[:, None]
    col = (pos % GRID_W).astype(F32)[:, None]
    out = []
    for d in (WA_HEAD, MLA_ROPE):
        q4 = d // 4
        freqs = ROPE_BASE ** (-jnp.arange(q4, dtype=F32) / q4)
        ar, ac = row * freqs[None, :], col * freqs[None, :]
        cos = jnp.concatenate([jnp.cos(ar), jnp.cos(ar), jnp.cos(ac), jnp.cos(ac)], axis=-1)
        sin = jnp.concatenate([-jnp.sin(ar), jnp.sin(ar), -jnp.sin(ac), jnp.sin(ac)], axis=-1)
        out += [jnp.tile(cos, (1, LANE // d)), jnp.tile(sin, (1, LANE // d))]
    return tuple(out)


def _wattn_kernel(*refs, band, nb):
    if band:
        q_ref, kp_ref, kc_ref, kn_ref, vp_ref, vc_ref, vn_ref, kx_ref, vx_ref, sink_ref, o_ref = refs
    else:
        q_ref, kx_ref, vx_ref, sink_ref, o_ref = refs
    n = pl.program_id(1)
    q = q_ref[0]
    tq = q.shape[0]
    outs = []
    for g in range(WA_KV_HEADS):
        ks = slice(WA_HEAD * g, WA_HEAD * (g + 1))
        qg = jnp.concatenate([q[:, WA_HEAD * (WA_GROUP * g + j):WA_HEAD * (WA_GROUP * g + j + 1)]
                              for j in range(WA_GROUP)], axis=0)
        kx = kx_ref[0][:, ks].astype(BF16)
        vx = vx_ref[0][:, ks].astype(BF16)
        if band:
            kb = jnp.concatenate([kp_ref[0][:, ks], kc_ref[0][:, ks], kn_ref[0][:, ks], kx], axis=0)
            vb = jnp.concatenate([vp_ref[0][:, ks].astype(BF16), vc_ref[0][:, ks].astype(BF16),
                                  vn_ref[0][:, ks].astype(BF16), vx], axis=0)
        else:
            kb, vb = kx, vx
        s = jnp.einsum('qd,kd->qk', qg, kb, preferred_element_type=F32) * WA_SCALE
        if band:
            nk = s.shape[1]
            qi = lax.broadcasted_iota(jnp.int32, (tq, nk), 0)
            kj = lax.broadcasted_iota(jnp.int32, (tq, nk), 1)
            rel = kj - BLOCK - qi
            ok = (jnp.abs(rel) <= WINDOW) & ((kj >= BLOCK) | (n > 0)) & ((kj < 2 * BLOCK) | (n < nb - 1))
            ok = ok | (kj >= 3 * BLOCK)
            ok = jnp.concatenate([ok] * WA_GROUP, axis=0)
            s = jnp.where(ok, s, -jnp.inf)
        sk = jnp.concatenate([jnp.full((tq, 1), sink_ref[0, WA_GROUP * g + j], F32) for j in range(WA_GROUP)], axis=0)
        m = jnp.maximum(jnp.max(s, axis=-1, keepdims=True), sk)
        e = jnp.exp(s - m)
        inv = 1.0 / (jnp.sum(e, axis=-1, keepdims=True) + jnp.exp(sk - m))
        og = jnp.dot((e * inv).astype(BF16), vb, preferred_element_type=F32)
        outs += [og[tq * j:tq * (j + 1)] for j in range(WA_GROUP)]
    o_ref[0] = jnp.concatenate(outs, axis=-1)


def _window_attn(wq, wk, p, wkx, px, sink, band):
    bsz, t, _ = wq.shape
    cx = wkx.shape[1]
    tq = BLOCK if band else t
    nb = t // tq
    vblk = _cblk("wv")
    kvw = WA_KV_HEADS * WA_HEAD
    ins = [wq]
    in_specs = [pl.BlockSpec((1, tq, WA_HEADS * WA_HEAD), lambda b, n: (b, n, 0))]
    if band:
        idx = (lambda n: jnp.maximum(n - 1, 0), lambda n: n, lambda n: jnp.minimum(n + 1, nb - 1))
        ins += [wk] * 3 + [p] * 3
        in_specs += [pl.BlockSpec((1, tq, kvw), functools.partial(lambda f, b, n: (b, f(n), 0), f)) for f in idx]
        in_specs += [pl.BlockSpec((1, tq, kvw), functools.partial(lambda f, b, n: (b, f(n), vblk), f)) for f in idx]
    ins += [wkx, px, sink.reshape(1, WA_HEADS)]
    in_specs += [pl.BlockSpec((1, cx, kvw), lambda b, n: (b, 0, 0)),
                 pl.BlockSpec((1, cx, kvw), lambda b, n: (b, 0, vblk)),
                 pl.BlockSpec(memory_space=pltpu.SMEM)]
    return pl.pallas_call(
        functools.partial(_wattn_kernel, band=band, nb=nb),
        grid=(bsz, nb),
        in_specs=in_specs,
        out_specs=pl.BlockSpec((1, tq, WA_HEADS * WA_HEAD), lambda b, n: (b, n, 0)),
        out_shape=jax.ShapeDtypeStruct((bsz, t, WA_HEADS * WA_HEAD), F32),
        compiler_params=_params("parallel", "parallel"),
        name="window_attn" if band else "context_gqa",
    )(*ins)


MLA_HPS = 2


def _mla_kernel(*refs, nkt, extra):
    if extra:
        q_ref, kt_ref, v_ref, ktx_ref, vx_ref, o_ref = refs
    else:
        q_ref, kt_ref, v_ref, o_ref = refs
    tq = q_ref.shape[2]
    tk = kt_ref.shape[4]

    def update(hh, carry, kt, v):
        m, acc = carry
        s = jnp.dot(q_ref[0, hh], kt, preferred_element_type=F32)
        m_new = jnp.maximum(m, jnp.max(s, axis=-1, keepdims=True))
        p = jnp.exp2(s - m_new).astype(BF16)
        acc = acc * jnp.exp2(m - m_new) + jnp.dot(p, v, preferred_element_type=F32)
        return m_new, acc

    def body(j, carry):
        start = pl.multiple_of(j * tk, tk)
        return tuple(update(hh, carry[hh], kt_ref[0, hh, j], v_ref[0, hh, pl.ds(start, tk), :])
                     for hh in range(MLA_HPS))

    init = (jnp.full((tq, 1), -jnp.inf, F32), jnp.zeros((tq, LANE), F32))
    carry = lax.fori_loop(0, nkt, body, (init,) * MLA_HPS)
    outs = []
    for hh in range(MLA_HPS):
        c = carry[hh]
        if extra:
            c = update(hh, c, ktx_ref[0, hh, 0], vx_ref[0, hh])
        acc = c[1]
        outs.append(acc[:, :MLA_V] / acc[:, MLA_V:MLA_V + 1])
    o_ref[0] = jnp.concatenate(outs, axis=-1)


def _mla_attn(qm, kmt, vm, kmtx=None, vmx=None):
    bsz, nh, t, _ = qm.shape
    nkt, tk = kmt.shape[2], kmt.shape[4]
    tkv = vm.shape[2]
    tq = min(512, t)
    extra = kmtx is not None
    ins = [qm, kmt, vm]
    in_specs = [pl.BlockSpec((1, MLA_HPS, tq, LANE), lambda b, h, i: (b, h, i, 0)),
                pl.BlockSpec((1, MLA_HPS, nkt, LANE, tk), lambda b, h, i: (b, h, 0, 0, 0)),
                pl.BlockSpec((1, MLA_HPS, tkv, LANE), lambda b, h, i: (b, h, 0, 0))]
    if extra:
        cx = vmx.shape[2]
        ins += [kmtx, vmx]
        in_specs += [pl.BlockSpec((1, MLA_HPS, 1, LANE, cx), lambda b, h, i: (b, h, 0, 0, 0)),
                     pl.BlockSpec((1, MLA_HPS, cx, LANE), lambda b, h, i: (b, h, 0, 0))]
    return pl.pallas_call(
        functools.partial(_mla_kernel, nkt=nkt, extra=extra),
        grid=(bsz, nh // MLA_HPS, t // tq),
        in_specs=in_specs,
        out_specs=pl.BlockSpec((1, tq, MLA_HPS * MLA_V), lambda b, h, i: (b, i, h)),
        out_shape=jax.ShapeDtypeStruct((bsz, t, nh * MLA_V), F32),
        compiler_params=_params("parallel", "parallel", "parallel"),
        name="mla_attn",
    )(*ins)


def _merge_kernel(x_ref, g_ref, o_ref, bonus_ref, yb_ref, yc_ref, yd_ref, z_ref, gate_ref,
                  gnw_ref, gnb_ref, wb_ref, wo_ref, fnw_ref, out_ref, *, final):
    o = o_ref[0, 0] + o_ref[0, 1]
    mu = jnp.mean(o, axis=-1, keepdims=True)
    oc = o - mu
    var = jnp.mean(oc * oc, axis=-1, keepdims=True)
    on = oc * lax.rsqrt(var + RW_GN_EPS)
    ya = jnp.concatenate([on[h] for h in range(RW_HEADS)], axis=-1) * gnw_ref[...] + gnb_ref[...]
    ya = ya + jnp.concatenate([bonus_ref[0, h] for h in range(RW_HEADS)], axis=-1)
    ys = (ya, yb_ref[0], yc_ref[0], yd_ref[0])
    m = None
    for i, y in enumerate(ys):
        zi = z_ref[0, :, BRANCH_W * i:BRANCH_W * (i + 1)].astype(F32)
        gi = gate_ref[0, :, D_MODEL * i:D_MODEL * (i + 1)].astype(F32)
        yz = (y * (zi * _sigmoid(zi))).astype(BF16)
        term = _sigmoid(gi) * jnp.dot(yz, wb_ref[i], preferred_element_type=F32)
        m = term if m is None else m + term
    xn = x_ref[0] + g_ref[0] * jnp.dot(m.astype(BF16), wo_ref[...], preferred_element_type=F32)
    if final:
        xn = _rms(xn, fnw_ref[...])
    out_ref[0] = xn


def _merge(x, g, o, bonus, yb, yc, yd, p, gn_w, gn_b, w_branch, w_out, final_norm_w, final):
    bsz, t, d = x.shape
    tm = min(256, t)
    zblk, gblk = _cblk("z"), _cblk("gate")
    row = lambda w: pl.BlockSpec((1, tm, w), lambda b, i: (b, i, 0))
    return pl.pallas_call(
        functools.partial(_merge_kernel, final=final),
        grid=(bsz, t // tm),
        in_specs=[row(d),
                  pl.BlockSpec((1, 1, d), lambda b, i: (b, 0, 0)),
                  pl.BlockSpec((1, 2, RW_HEADS, tm, RW_HEAD), lambda b, i: (b, 0, 0, i, 0)),
                  pl.BlockSpec((1, RW_HEADS, tm, RW_HEAD), lambda b, i: (b, 0, i, 0)),
                  row(BRANCH_W), row(BRANCH_W), row(BRANCH_W),
                  pl.BlockSpec((1, tm, COL["z"][1]), lambda b, i: (b, i, zblk)),
                  pl.BlockSpec((1, tm, COL["gate"][1]), lambda b, i: (b, i, gblk)),
                  pl.BlockSpec((1, RW_C), lambda b, i: (0, 0)),
                  pl.BlockSpec((1, RW_C), lambda b, i: (0, 0)),
                  pl.BlockSpec((N_BRANCH, BRANCH_W, d), lambda b, i: (0, 0, 0)),
                  pl.BlockSpec((d, d), lambda b, i: (0, 0)),
                  pl.BlockSpec((1, d), lambda b, i: (0, 0))],
        out_specs=row(d),
        out_shape=jax.ShapeDtypeStruct((bsz, t, d), F32),
        compiler_params=_params("parallel", "parallel"),
        name="merge",
    )(x, g, o, bonus, yb, yc, yd, p, p, gn_w.reshape(1, RW_C), gn_b.reshape(1, RW_C),
      w_branch, w_out, final_norm_w.reshape(1, d))


MLA_TK = 1024


def _permute_q_up(q_up):
    r = q_up.shape[0]
    w = q_up.reshape(r, MLA_HEADS, MLA_NOPE + MLA_ROPE)
    nope = w[:, :, :MLA_NOPE].reshape(r, MLA_HEADS * MLA_NOPE)
    rope = jnp.pad(w[:, :, MLA_NOPE:], ((0, 0), (0, 0), (0, MLA_NOPE - MLA_ROPE))).reshape(r, MLA_HEADS * MLA_NOPE)
    return jnp.concatenate([nope, rope], axis=-1)


def _permute_kv_up(kv_up):
    r = kv_up.shape[0]
    w = kv_up.reshape(r, MLA_HEADS, MLA_NOPE + MLA_V)
    return jnp.concatenate([w[:, :, :MLA_NOPE].reshape(r, -1), w[:, :, MLA_NOPE:].reshape(r, -1)], axis=-1)


def kernel(x, c, ctx, c_ctx, norm_w, ada_w, ada_b, w_in, rwkv_mu, rwkv_w0, rwkv_w_up, rwkv_a0, rwkv_a_up, rwkv_k_k, rwkv_k_a, rwkv_r_k, rwkv_gn_w, rwkv_gn_b, conv_w, conv_b, conv_ln_w, conv_ln_b, attn_sink, mla_q_norm, mla_q_up, mla_kv_norm, mla_kv_up, w_branch, w_out, final_norm_w):
    bsz, t, d = x.shape
    n_ctx = ctx.shape[1]
    depth = norm_w.shape[0]
    assert t % MLA_TK == 0 and t % BLOCK == 0 and n_ctx % RW_CHUNK == 0 and n_ctx <= MLA_TK

    cvec = jnp.concatenate([c, c_ctx[None, :], jnp.zeros((SUBLANE - bsz - 1, d), F32)], axis=0)
    mod = _modulation(cvec, ada_w, ada_b)
    tables = _rope_tables(t)
    s_zero = jnp.zeros((bsz, 2, RW_HEADS, RW_HEAD, RW_HEAD), F32)

    for l in range(depth):
        last = l == depth - 1
        mx = mod[l, :bsz][:, None, :]
        mc = jnp.broadcast_to(mod[l, bsz][None, None, :], (bsz, 1, 3 * d))
        sh_x, sc_x, g_x = mx[..., :d], mx[..., d:2 * d], mx[..., 2 * d:]
        sh_c, sc_c, g_c = mc[..., :d], mc[..., d:2 * d], mc[..., 2 * d:]
        w_p = _pack_cols(w_in[l]).astype(BF16)
        px32, px = _in_proj(x, norm_w[l], sc_x, sh_x, w_p)
        pc32, pc = _in_proj(ctx, norm_w[l], sc_c, sh_c, w_p)

        rw_args = (rwkv_mu[l], rwkv_w0[l], rwkv_w_up[l], rwkv_a0[l], rwkv_a_up[l], rwkv_k_k[l], rwkv_k_a[l],
                   rwkv_r_k[l].reshape(RW_C))
        o_c, bonus_c, s_ctx = _rwkv(pc32, pc, s_zero, *rw_args)
        o_x, bonus_x, _ = _rwkv(px32, px, s_ctx, *rw_args)

        conv_args = (conv_w[l], conv_b[l], conv_ln_w[l], conv_ln_b[l])
        yb_x = _conv(px, *conv_args)

        q_up_p = _permute_q_up(mla_q_up[l]).astype(BF16)
        kv_up_p = _permute_kv_up(mla_kv_up[l]).astype(BF16)
        prep_w = (mla_q_norm[l], q_up_p, mla_kv_norm[l], kv_up_p)
        qm_x, kmt_x, vm_x, wq_x, wk_x = _attn_prep(px, tables, *prep_w, tk=MLA_TK)
        qm_c, kmt_c, vm_c, wq_c, wk_c = _attn_prep(pc, None, *prep_w, tk=n_ctx)

        yc_x = _window_attn(wq_x, wk_x, px, wk_c, pc, attn_sink[l], band=True)
        yd_x = _mla_attn(qm_x, kmt_x, vm_x, kmt_c, vm_c)

        merge_w = (rwkv_gn_w[l], rwkv_gn_b[l], w_branch[l].astype(BF16), w_out[l].astype(BF16), final_norm_w)
        x = _merge(x, g_x, o_x, bonus_x, yb_x, yc_x, yd_x, px, *merge_w, final=last)

        if not last:
            yb_c = _conv(pc, *conv_args)
            yc_c = _window_attn(wq_c, wk_c, pc, wk_c, pc, attn_sink[l], band=False)
            yd_c = _mla_attn(qm_c, kmt_c, vm_c)
            ctx = _merge(ctx, g_c, o_c, bonus_c, yb_c, yc_c, yd_c, pc, *merge_w, final=False)

    return x
```

```python
import functools

import jax
import jax.numpy as jnp
from jax import lax
from jax.experimental import pallas as pl
from jax.experimental.pallas import tpu as pltpu

F32 = jnp.float32
BF16 = jnp.bfloat16

D_MODEL = 1024
GRID_W = 64
N_BRANCH = 4
BRANCH_W = 512
RW_HEADS = 8
RW_HEAD = 64
RW_C = RW_HEADS * RW_HEAD
RW_LORA = 64
RW_GN_EPS = 64e-5
CONV_CH = 512
CONV_K = 31
CONV_LN_EPS = 1e-5
WA_HEADS = 8
WA_KV_HEADS = 2
WA_HEAD = 64
WA_GROUP = WA_HEADS // WA_KV_HEADS
WINDOW = 128
BLOCK = 128
WA_SCALE = WA_HEAD ** -0.5
MLA_HEADS = 8
MLA_Q_RANK = 256
MLA_KV_RANK = 128
MLA_NOPE = 64
MLA_ROPE = 32
MLA_V = 64
MLA_SCALE = (MLA_NOPE + MLA_ROPE) ** -0.5
MLA_QSCALE = MLA_SCALE * 1.4426950408889634
ROPE_BASE = 10000.0
NORM_EPS = 1e-6

LANE = 128
SUBLANE = 8
VMEM_LIMIT = 48 * 1024 * 1024

_SRC = {}
_off = 0
for _name, _size in (("r", RW_C), ("k", RW_C), ("v", RW_C), ("wd", RW_LORA), ("ad", RW_LORA),
                     ("cv", 2 * CONV_CH), ("wq", WA_HEADS * WA_HEAD), ("wk", WA_KV_HEADS * WA_HEAD),
                     ("wv", WA_KV_HEADS * WA_HEAD), ("dq", MLA_Q_RANK), ("dkv", MLA_KV_RANK),
                     ("kr", MLA_ROPE), ("z", N_BRANCH * BRANCH_W), ("gate", N_BRANCH * D_MODEL)):
    _SRC[_name] = (_off, _size)
    _off += _size
N_IN = _off

_PACK32 = (("r", 512), ("k", 512), ("v", 512))
_PACK16 = (("gate", 4096), ("z", 2048), ("cv", 1024), ("wq", 512), ("dq", 256), ("wdad", 128),
           ("wk", 128), ("wv", 128), ("dkv", 128), ("kr", 128), ("pad", 128))
_PACK = _PACK32 + _PACK16
COL = {}
for _pack in (_PACK32, _PACK16):
    _off = 0
    for _name, _size in _pack:
        assert _off % _size == 0
        COL[_name] = (_off, _size)
        _off += _size
N32 = sum(s for _, s in _PACK32)
N16 = sum(s for _, s in _PACK16)
N_PACK = N32 + N16


def _cblk(name):
    off, size = COL[name]
    return off // size


def _pack_cols(w):
    def src(name):
        o, s = _SRC[name]
        return w[..., o:o + s]
    pieces = []
    for name, size in _PACK:
        if name == "wdad":
            pieces += [src("wd"), src("ad")]
        elif name == "kr":
            pieces += [src("kr"), jnp.zeros(w.shape[:-1] + (size - MLA_ROPE,), w.dtype)]
        elif name == "pad":
            pieces.append(jnp.zeros(w.shape[:-1] + (size,), w.dtype))
        else:
            pieces.append(src(name))
    return jnp.concatenate(pieces, axis=-1)


def _params(*sem):
    return pltpu.CompilerParams(dimension_semantics=sem, vmem_limit_bytes=VMEM_LIMIT)


def _sigmoid(x):
    return 0.5 * jnp.tanh(0.5 * x) + 0.5


def _split2(x):
    hi = x.astype(BF16)
    lo = (x - hi.astype(F32)).astype(BF16)
    return hi, lo


def _split3(x):
    hi = x.astype(BF16)
    r1 = x - hi.astype(F32)
    mid = r1.astype(BF16)
    lo = (r1 - mid.astype(F32)).astype(BF16)
    return hi, mid, lo


def _ein(spec, a, b, passes=1):
    def e(x, y):
        return jnp.einsum(spec, x, y, preferred_element_type=F32)
    if passes == 1:
        return e(a.astype(BF16), b.astype(BF16))
    ah, al = _split2(a)
    bh, bl = _split2(b)
    return e(ah, bh) + (e(ah, bl) + e(al, bh))


def _mod_kernel(c_ref, w_ref, b_ref, o_ref):
    cv = c_ref[...]
    sc = cv * jax.nn.sigmoid(cv)
    o_ref[0] = jnp.dot(sc, w_ref[0], preferred_element_type=F32,
                       precision=lax.Precision.HIGHEST) + b_ref[0]


def _modulation(cvec, ada_w, ada_b):
    depth, d, d3 = ada_w.shape
    rows = cvec.shape[0]
    tn = 512
    return pl.pallas_call(
        _mod_kernel,
        grid=(depth, d3 // tn),
        in_specs=[pl.BlockSpec((rows, d), lambda l, j: (0, 0)),
                  pl.BlockSpec((1, d, tn), lambda l, j: (l, 0, j)),
                  pl.BlockSpec((1, 1, tn), lambda l, j: (l, 0, j))],
        out_specs=pl.BlockSpec((1, rows, tn), lambda l, j: (l, 0, j)),
        out_shape=jax.ShapeDtypeStruct((depth, rows, d3), F32),
        compiler_params=_params("parallel", "parallel"),
        name="modulation",
    )(cvec, ada_w, ada_b.reshape(depth, 1, d3))


INPROJ_TN = 512
N32_TILES = N32 // INPROJ_TN


def _inproj_kernel(x_ref, nw_ref, sc_ref, sh_ref, w_ref, o32_ref, o16_ref, h_ref):
    j = pl.program_id(2)

    @pl.when(j == 0)
    def _():
        xf = x_ref[0]
        ms = jnp.mean(xf * xf, axis=-1, keepdims=True)
        hn = (xf * lax.rsqrt(ms + NORM_EPS)) * nw_ref[...]
        h_ref[...] = (hn * (1.0 + sc_ref[0]) + sh_ref[0]).astype(BF16)

    @pl.when(j < N32_TILES)
    def _():
        o32_ref[0] = jnp.dot(h_ref[...], w_ref[...], preferred_element_type=F32)

    @pl.when(j >= N32_TILES)
    def _():
        o16_ref[0] = jnp.dot(h_ref[...], w_ref[...], preferred_element_type=F32).astype(BF16)


def _in_proj(x, norm_w, scale, shift, w_packed):
    bsz, t, d = x.shape
    tm = min(1024, t)
    tn = INPROJ_TN
    return pl.pallas_call(
        _inproj_kernel,
        grid=(bsz, t // tm, N_PACK // tn),
        in_specs=[pl.BlockSpec((1, tm, d), lambda b, i, j: (b, i, 0)),
                  pl.BlockSpec((1, d), lambda b, i, j: (0, 0)),
                  pl.BlockSpec((1, 1, d), lambda b, i, j: (b, 0, 0)),
                  pl.BlockSpec((1, 1, d), lambda b, i, j: (b, 0, 0)),
                  pl.BlockSpec((d, tn), lambda b, i, j: (0, j))],
        out_specs=[pl.BlockSpec((1, tm, tn), lambda b, i, j: (b, i, jnp.minimum(j, N32_TILES - 1))),
                   pl.BlockSpec((1, tm, tn), lambda b, i, j: (b, i, jnp.maximum(j - N32_TILES, 0)))],
        out_shape=[jax.ShapeDtypeStruct((bsz, t, N32), F32),
                   jax.ShapeDtypeStruct((bsz, t, N16), BF16)],
        scratch_shapes=[pltpu.VMEM((tm, d), BF16)],
        compiler_params=_params("parallel", "parallel", "arbitrary"),
        name="in_proj",
    )(x, norm_w.reshape(1, d), scale, shift, w_packed)


RW_CHUNK = 64
P_DATA = 1
P_STATE = 3


PAIR = 2 * RW_HEAD
N_PAIR = RW_C // PAIR


def _dot1(a, b, passes=1):
    def e(x, y):
        return jnp.dot(x, y, preferred_element_type=F32)
    if passes == 1:
        return e(a.astype(BF16), b.astype(BF16))
    ah, al = _split2(a)
    bh, bl = _split2(b)
    return e(ah, bh) + (e(ah, bl) + e(al, bh))


def _lo_lanes(shape):
    return lax.broadcasted_iota(jnp.int32, shape, len(shape) - 1) % PAIR < RW_HEAD


def _bd2(y):
    lo = _lo_lanes(y.shape)
    zero = jnp.zeros_like(y)
    return jnp.concatenate([jnp.where(lo, y, zero), jnp.where(lo, zero, y)], axis=0)


def _diag_blocks(x):
    return jnp.where(_lo_lanes((RW_HEAD, x.shape[1])), x[:RW_HEAD], x[RW_HEAD:])


def _head_sum(x):
    outs = []
    for p in range(x.shape[-1] // PAIR):
        s = x[:, PAIR * p:PAIR * (p + 1)]
        lo = _lo_lanes(s.shape)
        s0 = jnp.sum(jnp.where(lo, s, 0.0), axis=-1, keepdims=True)
        s1 = jnp.sum(jnp.where(lo, 0.0, s), axis=-1, keepdims=True)
        outs.append(jnp.where(lo, s0, s1))
    return outs[0] if len(outs) == 1 else jnp.concatenate(outs, axis=-1)


def _rwkv_p1_kernel(r_ref, k_ref, v_ref, wa_ref, rp_ref, kp_ref, vp_ref, wap_ref,
                    rn_ref, kn_ref, vn_ref, wan_ref,
                    mur_ref, muk_ref, muv_ref, muwa_ref, w0_ref, wup_ref, a0_ref, aup_ref,
                    kkw_ref, kaw_ref, rkw_ref,
                    qt_ref, o0_ref, g_ref, h_ref, bonus_ref, *, nc):
    L = RW_CHUNK
    R = r_ref.shape[1]
    c = pl.program_id(1)

    def shift_mix(cur_ref, p_ref, n_ref, mu_ref):
        cur = cur_ref[0].astype(F32)
        hr = p_ref.shape[1]
        prow = jnp.where(c > 0, p_ref[0, hr - 1:hr, :].astype(F32), 0.0)
        nrow = jnp.where(c < nc - 1, n_ref[0, 0:1, :].astype(F32), 0.0)
        ridx = lax.broadcasted_iota(jnp.int32, cur.shape, 0)
        fprev = jnp.where(ridx == 0, prow, pltpu.roll(cur, 1, 0))
        fnext = jnp.where(ridx == R - 1, nrow, pltpu.roll(cur, R - 1, 0))
        return cur + mu_ref[...] * (0.5 * (fprev + fnext) - cur)

    r = shift_mix(r_ref, rp_ref, rn_ref, mur_ref)
    k = shift_mix(k_ref, kp_ref, kn_ref, muk_ref)
    v = shift_mix(v_ref, vp_ref, vn_ref, muv_ref)
    wa = shift_mix(wa_ref, wap_ref, wan_ref, muwa_ref)
    wd_t = jnp.tanh(wa[:, :RW_LORA]).astype(BF16)
    ad = wa[:, RW_LORA:].astype(BF16)

    kk = k * kkw_ref[...]
    kk = kk / jnp.maximum(jnp.sqrt(_head_sum(kk * kk)), 1e-12)

    row = lax.broadcasted_iota(jnp.int32, (L, PAIR), 0)
    col = lax.broadcasted_iota(jnp.int32, (L, PAIR), 1) % RW_HEAD
    eye_l = jnp.where(row == col, 1.0, 0.0)
    rown = lax.broadcasted_iota(jnp.int32, (RW_HEAD, PAIR), 0)
    coln = lax.broadcasted_iota(jnp.int32, (RW_HEAD, PAIR), 1) % RW_HEAD
    eye_n = rown == coln
    trow = lax.broadcasted_iota(jnp.int32, (R, R), 0)
    tcol = lax.broadcasted_iota(jnp.int32, (R, R), 1)
    same_chunk = (trow // L) == (tcol // L)
    steps = L.bit_length() - 1
    kdir_sum = None
    dense = []
    for d in range(2):
        tri = (same_chunk & ((tcol <= trow) if d == 0 else (tcol >= trow))).astype(BF16)
        last = L - 1 if d == 0 else 0

        w_pre = w0_ref[d:d + 1, :] + jnp.dot(wd_t, wup_ref[d].astype(BF16), preferred_element_type=F32)
        w_log = -(jnp.maximum(-w_pre, 0.0) + jnp.log1p(jnp.exp(-jnp.abs(w_pre)))) - 0.5
        logw = -jnp.exp(w_log)
        lh, lm, ll = _split3(logw)
        cum = (jnp.dot(tri, lh, preferred_element_type=F32)
               + (jnp.dot(tri, lm, preferred_element_type=F32) + jnp.dot(tri, ll, preferred_element_type=F32)))
        a = jax.nn.sigmoid(a0_ref[d:d + 1, :] + jnp.dot(ad, aup_ref[d].astype(BF16), preferred_element_type=F32))

        cum_ls = [cum[j * L + last:j * L + last + 1, :] for j in range(R // L)]
        cum_l = jnp.concatenate([jnp.broadcast_to(x, (L, RW_C)) for x in cum_ls], axis=0)
        e_neg = jnp.exp(-cum)
        e_l = jnp.exp(cum_l - cum)
        w_l = [jnp.exp(x) for x in cum_ls]
        kdir = k * (1.0 + (a - 1.0) * kaw_ref[...])
        kdir_sum = kdir if kdir_sum is None else kdir_sum + kdir
        b = kk * a
        al_c = kk * jnp.exp(cum - logw)
        rho_c = r * jnp.exp(cum)
        be_c = b * e_neg
        ka_c = kdir * e_neg
        bel_c = b * e_l
        kal_c = kdir * e_l
        dense.append((al_c, rho_c, be_c, ka_c, bel_c, kal_c, w_l))

    bonus_ref[0] = _head_sum(r * kdir_sum * rkw_ref[...]) * v

    chains = [(j, d, p) for j in range(R // L) for d in range(2) for p in range(N_PAIR)]
    tn = (((0,), (0,)), ((), ()))

    def sub(x, ch):
        return x[ch[0] * L:(ch[0] + 1) * L, PAIR * ch[2]:PAIR * (ch[2] + 1)]

    def each(fn, *lists):
        return [fn(ch, *(lst[i] for lst in lists)) for i, ch in enumerate(chains)]

    def products(ch):
        al, rho, be, ka = (sub(x, ch) for x in dense[ch[1]][:4])
        lo = _lo_lanes(be.shape)
        zero = jnp.zeros_like(be)
        rhs_nt = jnp.concatenate([jnp.where(lo, be, zero), jnp.where(lo, zero, be),
                                  jnp.where(lo, ka, zero), jnp.where(lo, zero, ka)], axis=0)
        lhs = jnp.concatenate([al, rho], axis=0)
        return jnp.einsum('ld,md->lm', lhs.astype(BF16), rhs_nt.astype(BF16),
                          preferred_element_type=F32)

    ms = each(products)

    def masks(ch, m):
        d = ch[1]
        incl = (col <= row) if d == 0 else (col >= row)
        strict = (col < row) if d == 0 else (col > row)
        return (-jnp.where(strict, m[:L, :PAIR], 0.0), jnp.where(strict, m[:L, PAIR:], 0.0),
                jnp.where(incl, m[L:, :PAIR], 0.0), jnp.where(incl, m[L:, PAIR:], 0.0))

    mk = each(masks, ms)
    negs, akss, rbis, rkis = ([x[i] for x in mk] for i in range(4))

    accs = [eye_l + neg for neg in negs]
    pws = each(lambda ch, neg: _dot1(neg, _bd2(neg), P_DATA), negs)
    for _ in range(1, steps - 1):
        sts = each(lambda ch, acc, pw: _dot1(jnp.concatenate([acc, pw], axis=0), _bd2(pw), P_DATA), accs, pws)
        accs = [acc + st[:L] for acc, st in zip(accs, sts)]
        pws = [st[L:] for st in sts]
    tmats = each(lambda ch, acc, pw: acc + _dot1(acc, _bd2(pw), P_DATA), accs, pws)

    vss = each(lambda ch, aks, rki: _dot1(jnp.concatenate([aks, rki], axis=0), _bd2(sub(v, ch)), P_DATA),
               akss, rkis)
    tus = each(lambda ch, tmat, vs: _dot1(tmat, jnp.concatenate(
        [_bd2(sub(dense[ch[1]][0], ch)), _bd2(vs[:L])], axis=1), P_DATA), tmats, vss)
    uas = [tu[:, :PAIR] for tu in tus]
    d0s = [-tu[:, PAIR:] for tu in tus]
    rus = each(lambda ch, rbi, ua, d0: _dot1(rbi, jnp.concatenate([_bd2(ua), _bd2(d0)], axis=1), P_DATA),
               rbis, uas, d0s)
    gbs = each(lambda ch, ua, d0: lax.dot_general(
        sub(dense[ch[1]][4], ch).astype(BF16), jnp.concatenate([ua, d0], axis=1).astype(BF16), tn,
        preferred_element_type=F32), uas, d0s)
    gks = each(lambda ch: lax.dot_general(
        sub(dense[ch[1]][5], ch).astype(BF16), sub(v, ch).astype(BF16), tn, preferred_element_type=F32))

    for i, ch in enumerate(chains):
        j, d, p = ch
        rows = slice(j * L, (j + 1) * L)
        sl = slice(PAIR * p, PAIR * (p + 1))
        qt_ref[0, d, rows, sl] = sub(dense[d][1], ch) - rus[i][:, :PAIR]
        o0_ref[0, d, rows, sl] = rus[i][:, PAIR:] + vss[i][L:]
        g_ref[0, d, j, :, sl] = jnp.where(eye_n, dense[d][6][j][:, sl], 0.0) - _diag_blocks(gbs[i][:, :PAIR])
        h_ref[0, d, j, :, sl] = _diag_blocks(gbs[i][:, PAIR:]) + _diag_blocks(gks[i])


def _rwkv_p2_kernel(qt_ref, o0_ref, g_ref, h_ref, s0_ref, o_ref, sfin_ref, s_ref, *, nsteps, cps):
    L = RW_CHUNK
    d = pl.program_id(1)
    c = pl.program_id(2)

    @pl.when(c == 0)
    def _():
        s_ref[...] = s0_ref[0, 0]

    def body(i, s):
        ci = i + d * (cps - 1 - 2 * i)
        rows = pl.ds(pl.multiple_of(ci * L, L), L)
        qt = qt_ref[0, 0, rows, :]
        g = g_ref[0, 0, ci]
        outs = []
        for p in range(N_PAIR):
            sl = slice(PAIR * p, PAIR * (p + 1))
            outs.append(_dot1(jnp.concatenate([qt[:, sl], g[:, sl]], axis=0), _bd2(s[:, sl]), P_STATE))
        st = jnp.concatenate(outs, axis=1)
        o_ref[0, 0, rows, :] = st[:L] + o0_ref[0, 0, rows, :]
        return st[L:] + h_ref[0, 0, ci]

    s_new = lax.fori_loop(0, cps, body, s_ref[...])
    s_ref[...] = s_new

    @pl.when(c == nsteps - 1)
    def _():
        sfin_ref[0, 0] = s_new


RW_PREP_CHUNKS = 4
RW_SCAN_CHUNKS = 8
HALO16 = 2 * SUBLANE


def _rwkv(p32, p16, s0, mu, w0, w_up, a0, a_up, k_k, k_a, r_k):
    bsz, t, _ = p32.shape
    L = RW_CHUNK
    nc = t // L

    def halo_rows(name):
        return HALO16 if name == "wdad" else SUBLANE

    cpb = min(RW_PREP_CHUNKS, nc)
    R = cpb * L
    nblk = t // R

    def cur(name):
        blk = _cblk(name)
        w = COL[name][1]
        return pl.BlockSpec((1, R, w), lambda b, c: (b, c, blk))

    def prev(name):
        blk = _cblk(name)
        w = COL[name][1]
        hr = halo_rows(name)
        return pl.BlockSpec((1, hr, w), lambda b, c: (b, jnp.maximum(c * (R // hr) - 1, 0), blk))

    def nxt(name):
        blk = _cblk(name)
        w = COL[name][1]
        hr = halo_rows(name)
        return pl.BlockSpec((1, hr, w), lambda b, c: (b, jnp.minimum((c + 1) * (R // hr), t // hr - 1), blk))

    def full(a):
        nd = a.ndim
        return pl.BlockSpec(a.shape, lambda b, c: (0,) * nd)

    names = ("r", "k", "v", "wdad")
    srcs = [p32, p32, p32, p16]
    mu_p = [mu[o:o + s].reshape(1, s) for o, s in
            ((0, RW_C), (RW_C, RW_C), (2 * RW_C, RW_C), (3 * RW_C, 2 * RW_LORA))]
    small = mu_p + [w0, w_up, a0, a_up, k_k.reshape(1, RW_C), k_a.reshape(1, RW_C), r_k.reshape(1, RW_C)]
    qt, o0, g, h, bonus = pl.pallas_call(
        functools.partial(_rwkv_p1_kernel, nc=nblk),
        grid=(bsz, nblk),
        in_specs=[cur(n) for n in names] + [prev(n) for n in names] + [nxt(n) for n in names]
                 + [full(a) for a in small],
        out_specs=[pl.BlockSpec((1, 2, R, RW_C), lambda b, c: (b, 0, c, 0)),
                   pl.BlockSpec((1, 2, R, RW_C), lambda b, c: (b, 0, c, 0)),
                   pl.BlockSpec((1, 2, cpb, RW_HEAD, RW_C), lambda b, c: (b, 0, c, 0, 0)),
                   pl.BlockSpec((1, 2, cpb, RW_HEAD, RW_C), lambda b, c: (b, 0, c, 0, 0)),
                   pl.BlockSpec((1, R, RW_C), lambda b, c: (b, c, 0))],
        out_shape=[jax.ShapeDtypeStruct((bsz, 2, t, RW_C), F32),
                   jax.ShapeDtypeStruct((bsz, 2, t, RW_C), F32),
                   jax.ShapeDtypeStruct((bsz, 2, nc, RW_HEAD, RW_C), F32),
                   jax.ShapeDtypeStruct((bsz, 2, nc, RW_HEAD, RW_C), F32),
                   jax.ShapeDtypeStruct((bsz, t, RW_C), F32)],
        compiler_params=_params("parallel", "parallel"),
        name="rwkv_chunk_prep",
    )(*(srcs * 3), *small)

    cps = min(RW_SCAN_CHUNKS, nc)
    nsteps = nc // cps

    def cc(c, d):
        return c + d * (nsteps - 1 - 2 * c)

    tblk = (1, 1, cps * L, RW_C)
    gblk = (1, 1, cps, RW_HEAD, RW_C)
    sblk = (1, 1, RW_HEAD, RW_C)
    o, s_fin = pl.pallas_call(
        functools.partial(_rwkv_p2_kernel, nsteps=nsteps, cps=cps),
        grid=(bsz, 2, nsteps),
        in_specs=[pl.BlockSpec(tblk, lambda b, d, c: (b, d, cc(c, d), 0)),
                  pl.BlockSpec(tblk, lambda b, d, c: (b, d, cc(c, d), 0)),
                  pl.BlockSpec(gblk, lambda b, d, c: (b, d, cc(c, d), 0, 0)),
                  pl.BlockSpec(gblk, lambda b, d, c: (b, d, cc(c, d), 0, 0)),
                  pl.BlockSpec(sblk, lambda b, d, c: (b, d, 0, 0))],
        out_specs=[pl.BlockSpec(tblk, lambda b, d, c: (b, d, cc(c, d), 0)),
                   pl.BlockSpec(sblk, lambda b, d, c: (b, d, 0, 0))],
        out_shape=[jax.ShapeDtypeStruct((bsz, 2, t, RW_C), F32),
                   jax.ShapeDtypeStruct((bsz, 2, RW_HEAD, RW_C), F32)],
        scratch_shapes=[pltpu.VMEM((RW_HEAD, RW_C), F32)],
        compiler_params=_params("parallel", "parallel", "arbitrary"),
        name="rwkv_state_scan",
    )(qt, o0, g, h, s0)
    return o, bonus, s_fin


CONV_HALO = 16


def _conv_kernel(cur_ref, prev_ref, next_ref, w_ref, b_ref, lnw_ref, lnb_ref, o_ref, u_ref, *, nt, tt):
    i = pl.program_id(1)

    def glu(x):
        x = x.astype(F32)
        return x[:, :CONV_CH] * _sigmoid(x[:, CONV_CH:])

    u_ref[0:CONV_HALO, :] = jnp.where(i > 0, glu(prev_ref[0]), 0.0)
    u_ref[CONV_HALO:CONV_HALO + tt, :] = glu(cur_ref[0])
    u_ref[CONV_HALO + tt:, :] = jnp.where(i < nt - 1, glu(next_ref[0]), 0.0)
    acc = jnp.zeros((tt, CONV_CH), F32) + b_ref[...]
    base = CONV_HALO - CONV_K // 2
    for j in range(CONV_K):
        acc = acc + u_ref[base + j:base + j + tt, :] * w_ref[j:j + 1, :]
    mu = jnp.mean(acc, axis=-1, keepdims=True)
    xc = acc - mu
    var = jnp.mean(xc * xc, axis=-1, keepdims=True)
    un = (xc * lax.rsqrt(var + CONV_LN_EPS)) * lnw_ref[...] + lnb_ref[...]
    o_ref[0] = un * _sigmoid(un)


def _conv(p, conv_w, conv_b, ln_w, ln_b):
    bsz, t, _ = p.shape
    tt = min(512, t)
    nt = t // tt
    hb = tt // CONV_HALO
    nh = t // CONV_HALO
    blk = _cblk("cv")
    w = COL["cv"][1]
    vec = lambda a: a.reshape(1, CONV_CH)
    return pl.pallas_call(
        functools.partial(_conv_kernel, nt=nt, tt=tt),
        grid=(bsz, nt),
        in_specs=[pl.BlockSpec((1, tt, w), lambda b, i: (b, i, blk)),
                  pl.BlockSpec((1, CONV_HALO, w), lambda b, i: (b, jnp.maximum(i * hb - 1, 0), blk)),
                  pl.BlockSpec((1, CONV_HALO, w), lambda b, i: (b, jnp.minimum((i + 1) * hb, nh - 1), blk)),
                  pl.BlockSpec((CONV_K, CONV_CH), lambda b, i: (0, 0)),
                  pl.BlockSpec((1, CONV_CH), lambda b, i: (0, 0)),
                  pl.BlockSpec((1, CONV_CH), lambda b, i: (0, 0)),
                  pl.BlockSpec((1, CONV_CH), lambda b, i: (0, 0))],
        out_specs=pl.BlockSpec((1, tt, CONV_CH), lambda b, i: (b, i, 0)),
        out_shape=jax.ShapeDtypeStruct((bsz, t, CONV_CH), F32),
        scratch_shapes=[pltpu.VMEM((tt + 2 * CONV_HALO, CONV_CH), F32)],
        compiler_params=_params("parallel", "parallel"),
        name="conv_module",
    )(p, p, p, conv_w, vec(conv_b), vec(ln_w), vec(ln_b))


def _swap_halves(x, hs):
    slabs = []
    for j in range(x.shape[-1] // LANE):
        s = x[:, LANE * j:LANE * (j + 1)]
        lane = lax.broadcasted_iota(jnp.int32, s.shape, 1)
        slabs.append(jnp.where((lane % (2 * hs)) < hs, pltpu.roll(s, LANE - hs, 1), pltpu.roll(s, hs, 1)))
    return slabs[0] if len(slabs) == 1 else jnp.concatenate(slabs, axis=-1)


def _rope(x, cos, sin, hs):
    reps = x.shape[-1] // LANE
    if reps > 1:
        cos = jnp.concatenate([cos] * reps, axis=-1)
        sin = jnp.concatenate([sin] * reps, axis=-1)
    return x * cos + _swap_halves(x, hs) * sin


def _rms(x, w):
    ms = jnp.mean(x * x, axis=-1, keepdims=True)
    return (x * lax.rsqrt(ms + NORM_EPS)) * w


def _prep_kernel(*refs, rope):
    if rope:
        (dq_ref, dkv_ref, kr_ref, wq_ref, wk_ref, c64_ref, s64_ref, c32_ref, s32_ref,
         qnw_ref, qup_ref, kvnw_ref, kvup_ref, qm_ref, kmt_ref, vm_ref, wqo_ref, wko_ref) = refs
    else:
        (dq_ref, dkv_ref, kr_ref, wq_ref, wk_ref,
         qnw_ref, qup_ref, kvnw_ref, kvup_ref, qm_ref, kmt_ref, vm_ref, wqo_ref, wko_ref) = refs
    hw = MLA_HEADS * MLA_NOPE
    q = jnp.dot(_rms(dq_ref[0].astype(F32), qnw_ref[...]).astype(BF16), qup_ref[...],
                preferred_element_type=F32)
    kv = jnp.dot(_rms(dkv_ref[0].astype(F32), kvnw_ref[...]).astype(BF16), kvup_ref[...],
                 preferred_element_type=F32)
    qn, qr = q[:, :hw], q[:, hw:]
    kn, vv = kv[:, :hw], kv[:, hw:]
    kr = kr_ref[0].astype(F32)
    wq = wq_ref[0].astype(F32)
    wk = wk_ref[0].astype(F32)
    if rope:
        qr = _rope(qr, c32_ref[...], s32_ref[...], MLA_ROPE // 4)
        kr = _rope(kr, c32_ref[...], s32_ref[...], MLA_ROPE // 4)
        wq = _rope(wq, c64_ref[...], s64_ref[...], WA_HEAD // 4)
        wk = _rope(wk, c64_ref[...], s64_ref[...], WA_HEAD // 4)
    wqo_ref[0] = wq.astype(BF16)
    wko_ref[0] = wk.astype(BF16)
    qn = qn * MLA_QSCALE
    qr = qr * MLA_QSCALE
    lane = lax.broadcasted_iota(jnp.int32, (qn.shape[0], MLA_V), 1)
    ones_col = jnp.where(lane == 0, 1.0, 0.0)
    for h in range(MLA_HEADS):
        sl = slice(MLA_NOPE * h, MLA_NOPE * (h + 1))
        qm_ref[0, h] = jnp.concatenate([qn[:, sl], qr[:, sl]], axis=-1).astype(BF16)
        kh = jnp.concatenate([kn[:, sl], kr[:, :MLA_NOPE]], axis=-1)
        kmt_ref[0, h, 0] = kh.T.astype(BF16)
        vm_ref[0, h] = jnp.concatenate([vv[:, sl], ones_col], axis=-1).astype(BF16)


def _attn_prep(p, tables, q_norm, q_up_p, kv_norm, kv_up_p, tk):
    bsz, t, _ = p.shape
    tm = tk
    rope = tables is not None

    def slab(name):
        blk = _cblk(name)
        return pl.BlockSpec((1, tm, COL[name][1]), lambda b, i: (b, i, blk))

    def full(a):
        nd = a.ndim
        return pl.BlockSpec(a.shape, lambda b, i: (0,) * nd)

    ins = [p] * 5
    in_specs = [slab(n) for n in ("dq", "dkv", "kr", "wq", "wk")]
    if rope:
        ins += list(tables)
        in_specs += [pl.BlockSpec((tm, LANE), lambda b, i: (i, 0))] * 4
    small = [q_norm.reshape(1, -1), q_up_p, kv_norm.reshape(1, -1), kv_up_p]
    ins += small
    in_specs += [full(a) for a in small]
    return pl.pallas_call(
        functools.partial(_prep_kernel, rope=rope),
        grid=(bsz, t // tm),
        in_specs=in_specs,
        out_specs=[pl.BlockSpec((1, MLA_HEADS, tm, LANE), lambda b, i: (b, 0, i, 0)),
                   pl.BlockSpec((1, MLA_HEADS, 1, LANE, tm), lambda b, i: (b, 0, i, 0, 0)),
                   pl.BlockSpec((1, MLA_HEADS, tm, LANE), lambda b, i: (b, 0, i, 0)),
                   pl.BlockSpec((1, tm, WA_HEADS * WA_HEAD), lambda b, i: (b, i, 0)),
                   pl.BlockSpec((1, tm, WA_KV_HEADS * WA_HEAD), lambda b, i: (b, i, 0))],
        out_shape=[jax.ShapeDtypeStruct((bsz, MLA_HEADS, t, LANE), BF16),
                   jax.ShapeDtypeStruct((bsz, MLA_HEADS, t // tm, LANE, tm), BF16),
                   jax.ShapeDtypeStruct((bsz, MLA_HEADS, t, LANE), BF16),
                   jax.ShapeDtypeStruct((bsz, t, WA_HEADS * WA_HEAD), BF16),
                   jax.ShapeDtypeStruct((bsz, t, WA_KV_HEADS * WA_HEAD), BF16)],
        compiler_params=_params("parallel", "parallel"),
        name="attn_prep",
    )(*ins)


def _rope_tables(t):
    pos = jnp.arange(t)
    row = (pos // GRID_W).astype(F32)[:, None]
    col = (pos % GRID_W).astype(F32)[:, None]
    out = []
    for d in (WA_HEAD, MLA_ROPE):
        q4 = d // 4
        freqs = ROPE_BASE ** (-jnp.arange(q4, dtype=F32) / q4)
        ar, ac = row * freqs[None, :], col * freqs[None, :]
        cos = jnp.concatenate([jnp.cos(ar), jnp.cos(ar), jnp.cos(ac), jnp.cos(ac)], axis=-1)
        sin = jnp.concatenate([-jnp.sin(ar), jnp.sin(ar), -jnp.sin(ac), jnp.sin(ac)], axis=-1)
        out += [jnp.tile(cos, (1, LANE // d)), jnp.tile(sin, (1, LANE // d))]
    return tuple(out)


def _wattn_kernel(*refs, band, nb):
    if band:
        q_ref, kp_ref, kc_ref, kn_ref, vp_ref, vc_ref, vn_ref, kx_ref, vx_ref, sink_ref, o_ref = refs
    else:
        q_ref, kx_ref, vx_ref, sink_ref, o_ref = refs
    n = pl.program_id(1)
    q = q_ref[0]
    tq = q.shape[0]
    outs = []
    for g in range(WA_KV_HEADS):
        ks = slice(WA_HEAD * g, WA_HEAD * (g + 1))
        qg = jnp.concatenate([q[:, WA_HEAD * (WA_GROUP * g + j):WA_HEAD * (WA_GROUP * g + j + 1)]
                              for j in range(WA_GROUP)], axis=0)
        kx = kx_ref[0][:, ks].astype(BF16)
        vx = vx_ref[0][:, ks].astype(BF16)
        if band:
            kb = jnp.concatenate([kp_ref[0][:, ks], kc_ref[0][:, ks], kn_ref[0][:, ks], kx], axis=0)
            vb = jnp.concatenate([vp_ref[0][:, ks].astype(BF16), vc_ref[0][:, ks].astype(BF16),
                                  vn_ref[0][:, ks].astype(BF16), vx], axis=0)
        else:
            kb, vb = kx, vx
        s = jnp.einsum('qd,kd->qk', qg, kb, preferred_element_type=F32) * WA_SCALE
        if band:
            nk = s.shape[1]
            qi = lax.broadcasted_iota(jnp.int32, (tq, nk), 0)
            kj = lax.broadcasted_iota(jnp.int32, (tq, nk), 1)
            rel = kj - BLOCK - qi
            ok = (jnp.abs(rel) <= WINDOW) & ((kj >= BLOCK) | (n > 0)) & ((kj < 2 * BLOCK) | (n < nb - 1))
            ok = ok | (kj >= 3 * BLOCK)
            ok = jnp.concatenate([ok] * WA_GROUP, axis=0)
            s = jnp.where(ok, s, -jnp.inf)
        sk = jnp.concatenate([jnp.full((tq, 1), sink_ref[0, WA_GROUP * g + j], F32) for j in range(WA_GROUP)], axis=0)
        m = jnp.maximum(jnp.max(s, axis=-1, keepdims=True), sk)
        e = jnp.exp(s - m)
        inv = 1.0 / (jnp.sum(e, axis=-1, keepdims=True) + jnp.exp(sk - m))
        og = jnp.dot((e * inv).astype(BF16), vb, preferred_element_type=F32)
        outs += [og[tq * j:tq * (j + 1)] for j in range(WA_GROUP)]
    o_ref[0] = jnp.concatenate(outs, axis=-1)


def _window_attn(wq, wk, p, wkx, px, sink, band):
    bsz, t, _ = wq.shape
    cx = wkx.shape[1]
    tq = BLOCK if band else t
    nb = t // tq
    vblk = _cblk("wv")
    kvw = WA_KV_HEADS * WA_HEAD
    ins = [wq]
    in_specs = [pl.BlockSpec((1, tq, WA_HEADS * WA_HEAD), lambda b, n: (b, n, 0))]
    if band:
        idx = (lambda n: jnp.maximum(n - 1, 0), lambda n: n, lambda n: jnp.minimum(n + 1, nb - 1))
        ins += [wk] * 3 + [p] * 3
        in_specs += [pl.BlockSpec((1, tq, kvw), functools.partial(lambda f, b, n: (b, f(n), 0), f)) for f in idx]
        in_specs += [pl.BlockSpec((1, tq, kvw), functools.partial(lambda f, b, n: (b, f(n), vblk), f)) for f in idx]
    ins += [wkx, px, sink.reshape(1, WA_HEADS)]
    in_specs += [pl.BlockSpec((1, cx, kvw), lambda b, n: (b, 0, 0)),
                 pl.BlockSpec((1, cx, kvw), lambda b, n: (b, 0, vblk)),
                 pl.BlockSpec(memory_space=pltpu.SMEM)]
    return pl.pallas_call(
        functools.partial(_wattn_kernel, band=band, nb=nb),
        grid=(bsz, nb),
        in_specs=in_specs,
        out_specs=pl.BlockSpec((1, tq, WA_HEADS * WA_HEAD), lambda b, n: (b, n, 0)),
        out_shape=jax.ShapeDtypeStruct((bsz, t, WA_HEADS * WA_HEAD), F32),
        compiler_params=_params("parallel", "parallel"),
        name="window_attn" if band else "context_gqa",
    )(*ins)


MLA_HPS = 2
MLA_QSPLIT = 1


def _mla_kernel(*refs, nkt, extra):
    if extra:
        q_ref, kt_ref, v_ref, ktx_ref, vx_ref, o_ref = refs
    else:
        q_ref, kt_ref, v_ref, o_ref = refs
    tq = q_ref.shape[2]
    tk = kt_ref.shape[4]

    tqs = tq // MLA_QSPLIT
    chains = [(hh, qs) for hh in range(MLA_HPS) for qs in range(MLA_QSPLIT)]

    def update(chain, carry, kt, v):
        hh, qs = chain
        m, acc = carry
        s = jnp.dot(q_ref[0, hh, qs * tqs:(qs + 1) * tqs, :], kt, preferred_element_type=F32)
        m_new = jnp.maximum(m, jnp.max(s, axis=-1, keepdims=True))
        p = jnp.exp2(s - m_new).astype(BF16)
        acc = acc * jnp.exp2(m - m_new) + jnp.dot(p, v, preferred_element_type=F32)
        return m_new, acc

    def body(j, carry):
        start = pl.multiple_of(j * tk, tk)
        return tuple(update(ch, carry[n], kt_ref[0, ch[0], j], v_ref[0, ch[0], pl.ds(start, tk), :])
                     for n, ch in enumerate(chains))

    init = (jnp.full((tqs, 1), -jnp.inf, F32), jnp.zeros((tqs, LANE), F32))
    carry = lax.fori_loop(0, nkt, body, (init,) * len(chains))
    outs = []
    for hh in range(MLA_HPS):
        rows = []
        for qs in range(MLA_QSPLIT):
            n = chains.index((hh, qs))
            c = carry[n]
            if extra:
                c = update(chains[n], c, ktx_ref[0, hh, 0], vx_ref[0, hh])
            acc = c[1]
            rows.append(acc[:, :MLA_V] / acc[:, MLA_V:MLA_V + 1])
        outs.append(jnp.concatenate(rows, axis=0))
    o_ref[0] = jnp.concatenate(outs, axis=-1)


def _mla_attn(qm, kmt, vm, kmtx=None, vmx=None):
    bsz, nh, t, _ = qm.shape
    nkt, tk = kmt.shape[2], kmt.shape[4]
    tkv = vm.shape[2]
    tq = min(1024, t)
    extra = kmtx is not None
    ins = [qm, kmt, vm]
    in_specs = [pl.BlockSpec((1, MLA_HPS, tq, LANE), lambda b, h, i: (b, h, i, 0)),
                pl.BlockSpec((1, MLA_HPS, nkt, LANE, tk), lambda b, h, i: (b, h, 0, 0, 0)),
                pl.BlockSpec((1, MLA_HPS, tkv, LANE), lambda b, h, i: (b, h, 0, 0))]
    if extra:
        cx = vmx.shape[2]
        ins += [kmtx, vmx]
        in_specs += [pl.BlockSpec((1, MLA_HPS, 1, LANE, cx), lambda b, h, i: (b, h, 0, 0, 0)),
                     pl.BlockSpec((1, MLA_HPS, cx, LANE), lambda b, h, i: (b, h, 0, 0))]
    return pl.pallas_call(
        functools.partial(_mla_kernel, nkt=nkt, extra=extra),
        grid=(bsz, nh // MLA_HPS, t // tq),
        in_specs=in_specs,
        out_specs=pl.BlockSpec((1, tq, MLA_HPS * MLA_V), lambda b, h, i: (b, i, h)),
        out_shape=jax.ShapeDtypeStruct((bsz, t, nh * MLA_V), F32),
        compiler_params=_params("parallel", "parallel", "parallel"),
        name="mla_attn",
    )(*ins)


def _merge_kernel(x_ref, g_ref, o_ref, bonus_ref, yb_ref, yc_ref, yd_ref, z_ref, gate_ref,
                  gnw_ref, gnb_ref, wb_ref, wo_ref, fnw_ref, out_ref, *, final):
    o = o_ref[0, 0] + o_ref[0, 1]
    oc = o - _head_sum(o) * (1.0 / RW_HEAD)
    var = _head_sum(oc * oc) * (1.0 / RW_HEAD)
    ya = (oc * lax.rsqrt(var + RW_GN_EPS)) * gnw_ref[...] + gnb_ref[...] + bonus_ref[0]
    ys = (ya, yb_ref[0], yc_ref[0], yd_ref[0])
    m = None
    for i, y in enumerate(ys):
        zi = z_ref[0, :, BRANCH_W * i:BRANCH_W * (i + 1)].astype(F32)
        gi = gate_ref[0, :, D_MODEL * i:D_MODEL * (i + 1)].astype(F32)
        yz = (y * (zi * _sigmoid(zi))).astype(BF16)
        term = _sigmoid(gi) * jnp.dot(yz, wb_ref[i], preferred_element_type=F32)
        m = term if m is None else m + term
    xn = x_ref[0] + g_ref[0] * jnp.dot(m.astype(BF16), wo_ref[...], preferred_element_type=F32)
    if final:
        xn = _rms(xn, fnw_ref[...])
    out_ref[0] = xn


def _merge(x, g, o, bonus, yb, yc, yd, p, gn_w, gn_b, w_branch, w_out, final_norm_w, final):
    bsz, t, d = x.shape
    tm = min(256, t)
    zblk, gblk = _cblk("z"), _cblk("gate")
    row = lambda w: pl.BlockSpec((1, tm, w), lambda b, i: (b, i, 0))
    return pl.pallas_call(
        functools.partial(_merge_kernel, final=final),
        grid=(bsz, t // tm),
        in_specs=[row(d),
                  pl.BlockSpec((1, 1, d), lambda b, i: (b, 0, 0)),
                  pl.BlockSpec((1, 2, tm, RW_C), lambda b, i: (b, 0, i, 0)),
                  row(RW_C),
                  row(BRANCH_W), row(BRANCH_W), row(BRANCH_W),
                  pl.BlockSpec((1, tm, COL["z"][1]), lambda b, i: (b, i, zblk)),
                  pl.BlockSpec((1, tm, COL["gate"][1]), lambda b, i: (b, i, gblk)),
                  pl.BlockSpec((1, RW_C), lambda b, i: (0, 0)),
                  pl.BlockSpec((1, RW_C), lambda b, i: (0, 0)),
                  pl.BlockSpec((N_BRANCH, BRANCH_W, d), lambda b, i: (0, 0, 0)),
                  pl.BlockSpec((d, d), lambda b, i: (0, 0)),
                  pl.BlockSpec((1, d), lambda b, i: (0, 0))],
        out_specs=row(d),
        out_shape=jax.ShapeDtypeStruct((bsz, t, d), F32),
        compiler_params=_params("parallel", "parallel"),
        name="merge",
    )(x, g, o, bonus, yb, yc, yd, p, p, gn_w.reshape(1, RW_C), gn_b.reshape(1, RW_C),
      w_branch, w_out, final_norm_w.reshape(1, d))


MLA_TK = 1024


def _permute_q_up(q_up):
    r = q_up.shape[0]
    w = q_up.reshape(r, MLA_HEADS, MLA_NOPE + MLA_ROPE)
    nope = w[:, :, :MLA_NOPE].reshape(r, MLA_HEADS * MLA_NOPE)
    rope = jnp.pad(w[:, :, MLA_NOPE:], ((0, 0), (0, 0), (0, MLA_NOPE - MLA_ROPE))).reshape(r, MLA_HEADS * MLA_NOPE)
    return jnp.concatenate([nope, rope], axis=-1)


def _permute_kv_up(kv_up):
    r = kv_up.shape[0]
    w = kv_up.reshape(r, MLA_HEADS, MLA_NOPE + MLA_V)
    return jnp.concatenate([w[:, :, :MLA_NOPE].reshape(r, -1), w[:, :, MLA_NOPE:].reshape(r, -1)], axis=-1)


def kernel(x, c, ctx, c_ctx, norm_w, ada_w, ada_b, w_in, rwkv_mu, rwkv_w0, rwkv_w_up, rwkv_a0, rwkv_a_up, rwkv_k_k, rwkv_k_a, rwkv_r_k, rwkv_gn_w, rwkv_gn_b, conv_w, conv_b, conv_ln_w, conv_ln_b, attn_sink, mla_q_norm, mla_q_up, mla_kv_norm, mla_kv_up, w_branch, w_out, final_norm_w):
    bsz, t, d = x.shape
    n_ctx = ctx.shape[1]
    depth = norm_w.shape[0]
    assert t % MLA_TK == 0 and t % BLOCK == 0 and n_ctx % RW_CHUNK == 0 and n_ctx <= MLA_TK

    cvec = jnp.concatenate([c, c_ctx[None, :], jnp.zeros((SUBLANE - bsz - 1, d), F32)], axis=0)
    mod = _modulation(cvec, ada_w, ada_b)
    tables = _rope_tables(t)
    s_zero = jnp.zeros((bsz, 2, RW_HEAD, RW_C), F32)

    for l in range(depth):
        last = l == depth - 1
        mx = mod[l, :bsz][:, None, :]
        mc = jnp.broadcast_to(mod[l, bsz][None, None, :], (bsz, 1, 3 * d))
        sh_x, sc_x, g_x = mx[..., :d], mx[..., d:2 * d], mx[..., 2 * d:]
        sh_c, sc_c, g_c = mc[..., :d], mc[..., d:2 * d], mc[..., 2 * d:]
        w_p = _pack_cols(w_in[l]).astype(BF16)
        px32, px = _in_proj(x, norm_w[l], sc_x, sh_x, w_p)
        pc32, pc = _in_proj(ctx, norm_w[l], sc_c, sh_c, w_p)

        rw_args = (rwkv_mu[l], rwkv_w0[l], rwkv_w_up[l], rwkv_a0[l], rwkv_a_up[l], rwkv_k_k[l], rwkv_k_a[l],
                   rwkv_r_k[l].reshape(RW_C))
        o_c, bonus_c, s_ctx = _rwkv(pc32, pc, s_zero, *rw_args)
        o_x, bonus_x, _ = _rwkv(px32, px, s_ctx, *rw_args)

        conv_args = (conv_w[l], conv_b[l], conv_ln_w[l], conv_ln_b[l])
        yb_x = _conv(px, *conv_args)

        q_up_p = _permute_q_up(mla_q_up[l]).astype(BF16)
        kv_up_p = _permute_kv_up(mla_kv_up[l]).astype(BF16)
        prep_w = (mla_q_norm[l], q_up_p, mla_kv_norm[l], kv_up_p)
        qm_x, kmt_x, vm_x, wq_x, wk_x = _attn_prep(px, tables, *prep_w, tk=MLA_TK)
        qm_c, kmt_c, vm_c, wq_c, wk_c = _attn_prep(pc, None, *prep_w, tk=n_ctx)

        yc_x = _window_attn(wq_x, wk_x, px, wk_c, pc, attn_sink[l], band=True)
        yd_x = _mla_attn(qm_x, kmt_x, vm_x, kmt_c, vm_c)

        merge_w = (rwkv_gn_w[l], rwkv_gn_b[l], w_branch[l].astype(BF16), w_out[l].astype(BF16), final_norm_w)
        x = _merge(x, g_x, o_x, bonus_x, yb_x, yc_x, yd_x, px, *merge_w, final=last)

        if not last:
            yb_c = _conv(pc, *conv_args)
            yc_c = _window_attn(wq_c, wk_c, pc, wk_c, pc, attn_sink[l], band=False)
            yd_c = _mla_attn(qm_c, kmt_c, vm_c)
            ctx = _merge(ctx, g_c, o_c, bonus_c, yb_c, yc_c, yd_c, pc, *merge_w, final=False)

    return x
```

```python
import functools

import jax
import jax.numpy as jnp
from jax import lax
from jax.experimental import pallas as pl
from jax.experimental.pallas import tpu as pltpu

F32 = jnp.float32
BF16 = jnp.bfloat16

D_MODEL = 1024
GRID_W = 64
N_BRANCH = 4
BRANCH_W = 512
RW_HEADS = 8
RW_HEAD = 64
RW_C = RW_HEADS * RW_HEAD
RW_LORA = 64
RW_GN_EPS = 64e-5
CONV_CH = 512
CONV_K = 31
CONV_LN_EPS = 1e-5
WA_HEADS = 8
WA_KV_HEADS = 2
WA_HEAD = 64
WA_GROUP = WA_HEADS // WA_KV_HEADS
WINDOW = 128
BLOCK = 128
WA_SCALE = WA_HEAD ** -0.5
MLA_HEADS = 8
MLA_Q_RANK = 256
MLA_KV_RANK = 128
MLA_NOPE = 64
MLA_ROPE = 32
MLA_V = 64
MLA_SCALE = (MLA_NOPE + MLA_ROPE) ** -0.5
MLA_QSCALE = MLA_SCALE * 1.4426950408889634
ROPE_BASE = 10000.0
NORM_EPS = 1e-6

LANE = 128
SUBLANE = 8
VMEM_LIMIT = 48 * 1024 * 1024

_SRC = {}
_off = 0
for _name, _size in (("r", RW_C), ("k", RW_C), ("v", RW_C), ("wd", RW_LORA), ("ad", RW_LORA),
                     ("cv", 2 * CONV_CH), ("wq", WA_HEADS * WA_HEAD), ("wk", WA_KV_HEADS * WA_HEAD),
                     ("wv", WA_KV_HEADS * WA_HEAD), ("dq", MLA_Q_RANK), ("dkv", MLA_KV_RANK),
                     ("kr", MLA_ROPE), ("z", N_BRANCH * BRANCH_W), ("gate", N_BRANCH * D_MODEL)):
    _SRC[_name] = (_off, _size)
    _off += _size
N_IN = _off

_PACK32 = (("r", 512), ("k", 512), ("v", 512), ("wq", 512))
_PACK16 = (("gate", 4096), ("z", 2048), ("cv", 1024), ("dq", 256), ("wdad", 128),
           ("wk", 128), ("wv", 128), ("dkv", 128), ("kr", 128), ("pad", 128))
_PACK = _PACK32 + _PACK16
COL = {}
for _pack in (_PACK32, _PACK16):
    _off = 0
    for _name, _size in _pack:
        assert _off % _size == 0
        COL[_name] = (_off, _size)
        _off += _size
N32 = sum(s for _, s in _PACK32)
N16 = sum(s for _, s in _PACK16)
N_PACK = N32 + N16


def _cblk(name):
    off, size = COL[name]
    return off // size


def _pack_cols(w):
    def src(name):
        o, s = _SRC[name]
        return w[..., o:o + s]
    pieces = []
    for name, size in _PACK:
        if name == "wdad":
            pieces += [src("wd"), src("ad")]
        elif name == "kr":
            pieces += [src("kr"), jnp.zeros(w.shape[:-1] + (size - MLA_ROPE,), w.dtype)]
        elif name == "pad":
            pieces.append(jnp.zeros(w.shape[:-1] + (size,), w.dtype))
        else:
            pieces.append(src(name))
    return jnp.concatenate(pieces, axis=-1)


def _params(*sem):
    return pltpu.CompilerParams(dimension_semantics=sem, vmem_limit_bytes=VMEM_LIMIT)


def _sigmoid(x):
    return 0.5 * jnp.tanh(0.5 * x) + 0.5


def _split2(x):
    hi = x.astype(BF16)
    lo = (x - hi.astype(F32)).astype(BF16)
    return hi, lo


def _split3(x):
    hi = x.astype(BF16)
    r1 = x - hi.astype(F32)
    mid = r1.astype(BF16)
    lo = (r1 - mid.astype(F32)).astype(BF16)
    return hi, mid, lo


def _ein(spec, a, b, passes=1):
    def e(x, y):
        return jnp.einsum(spec, x, y, preferred_element_type=F32)
    if passes == 1:
        return e(a.astype(BF16), b.astype(BF16))
    ah, al = _split2(a)
    bh, bl = _split2(b)
    return e(ah, bh) + (e(ah, bl) + e(al, bh))


def _mod_kernel(c_ref, w_ref, b_ref, o_ref):
    cv = c_ref[...]
    sc = cv * jax.nn.sigmoid(cv)
    o_ref[0] = jnp.dot(sc, w_ref[0], preferred_element_type=F32,
                       precision=lax.Precision.HIGHEST) + b_ref[0]


def _modulation(cvec, ada_w, ada_b):
    depth, d, d3 = ada_w.shape
    rows = cvec.shape[0]
    tn = 512
    return pl.pallas_call(
        _mod_kernel,
        grid=(depth, d3 // tn),
        in_specs=[pl.BlockSpec((rows, d), lambda l, j: (0, 0)),
                  pl.BlockSpec((1, d, tn), lambda l, j: (l, 0, j)),
                  pl.BlockSpec((1, 1, tn), lambda l, j: (l, 0, j))],
        out_specs=pl.BlockSpec((1, rows, tn), lambda l, j: (l, 0, j)),
        out_shape=jax.ShapeDtypeStruct((depth, rows, d3), F32),
        compiler_params=_params("parallel", "parallel"),
        name="modulation",
    )(cvec, ada_w, ada_b.reshape(depth, 1, d3))


INPROJ_TN = 1024
assert N32 % INPROJ_TN == 0 and N16 % INPROJ_TN == 0
N32_TILES = N32 // INPROJ_TN


def _inproj_kernel(x_ref, nw_ref, sc_ref, sh_ref, w_ref, o32_ref, o16_ref, h_ref):
    j = pl.program_id(2)

    @pl.when(j == 0)
    def _():
        xf = x_ref[0]
        ms = jnp.mean(xf * xf, axis=-1, keepdims=True)
        hn = (xf * lax.rsqrt(ms + NORM_EPS)) * nw_ref[...]
        h_ref[...] = (hn * (1.0 + sc_ref[0]) + sh_ref[0]).astype(BF16)

    @pl.when(j < N32_TILES)
    def _():
        o32_ref[0] = jnp.dot(h_ref[...], w_ref[...], preferred_element_type=F32)

    @pl.when(j >= N32_TILES)
    def _():
        o16_ref[0] = jnp.dot(h_ref[...], w_ref[...], preferred_element_type=F32).astype(BF16)


def _in_proj(x, norm_w, scale, shift, w_packed):
    bsz, t, d = x.shape
    tm = min(1024, t)
    tn = INPROJ_TN
    return pl.pallas_call(
        _inproj_kernel,
        grid=(bsz, t // tm, N_PACK // tn),
        in_specs=[pl.BlockSpec((1, tm, d), lambda b, i, j: (b, i, 0)),
                  pl.BlockSpec((1, d), lambda b, i, j: (0, 0)),
                  pl.BlockSpec((1, 1, d), lambda b, i, j: (b, 0, 0)),
                  pl.BlockSpec((1, 1, d), lambda b, i, j: (b, 0, 0)),
                  pl.BlockSpec((d, tn), lambda b, i, j: (0, j))],
        out_specs=[pl.BlockSpec((1, tm, tn), lambda b, i, j: (b, i, jnp.minimum(j, N32_TILES - 1))),
                   pl.BlockSpec((1, tm, tn), lambda b, i, j: (b, i, jnp.maximum(j - N32_TILES, 0)))],
        out_shape=[jax.ShapeDtypeStruct((bsz, t, N32), F32),
                   jax.ShapeDtypeStruct((bsz, t, N16), BF16)],
        scratch_shapes=[pltpu.VMEM((tm, d), BF16)],
        compiler_params=_params("parallel", "parallel", "arbitrary"),
        name="in_proj",
    )(x, norm_w.reshape(1, d), scale, shift, w_packed)


RW_CHUNK = 64
P_DATA = 1
P_STATE = 3


PAIR = 2 * RW_HEAD
N_PAIR = RW_C // PAIR


def _dot1(a, b, passes=1):
    def e(x, y):
        return jnp.dot(x, y, preferred_element_type=F32)
    if passes == 1:
        return e(a.astype(BF16), b.astype(BF16))
    ah, al = _split2(a)
    bh, bl = _split2(b)
    return e(ah, bh) + (e(ah, bl) + e(al, bh))


def _lo_lanes(shape):
    return lax.broadcasted_iota(jnp.int32, shape, len(shape) - 1) % PAIR < RW_HEAD


def _bd2(y):
    lo = _lo_lanes(y.shape)
    zero = jnp.zeros_like(y)
    return jnp.concatenate([jnp.where(lo, y, zero), jnp.where(lo, zero, y)], axis=0)


def _diag_blocks(x):
    return jnp.where(_lo_lanes((RW_HEAD, x.shape[1])), x[:RW_HEAD], x[RW_HEAD:])


def _head_sum(x):
    outs = []
    for p in range(x.shape[-1] // PAIR):
        s = x[:, PAIR * p:PAIR * (p + 1)]
        lo = _lo_lanes(s.shape)
        s0 = jnp.sum(jnp.where(lo, s, 0.0), axis=-1, keepdims=True)
        s1 = jnp.sum(jnp.where(lo, 0.0, s), axis=-1, keepdims=True)
        outs.append(jnp.where(lo, s0, s1))
    return outs[0] if len(outs) == 1 else jnp.concatenate(outs, axis=-1)


def _rwkv_p1_kernel(r_ref, k_ref, v_ref, wa_ref, rp_ref, kp_ref, vp_ref, wap_ref,
                    rn_ref, kn_ref, vn_ref, wan_ref,
                    mur_ref, muk_ref, muv_ref, muwa_ref, w0_ref, wup_ref, a0_ref, aup_ref,
                    kkw_ref, kaw_ref, rkw_ref,
                    qt_ref, o0_ref, g_ref, h_ref, bonus_ref, *, nc):
    L = RW_CHUNK
    R = r_ref.shape[1]
    c = pl.program_id(1)

    def shift_mix(cur_ref, p_ref, n_ref, mu_ref):
        cur = cur_ref[0].astype(F32)
        hr = p_ref.shape[1]
        prow = jnp.where(c > 0, p_ref[0, hr - 1:hr, :].astype(F32), 0.0)
        nrow = jnp.where(c < nc - 1, n_ref[0, 0:1, :].astype(F32), 0.0)
        ridx = lax.broadcasted_iota(jnp.int32, cur.shape, 0)
        fprev = jnp.where(ridx == 0, prow, pltpu.roll(cur, 1, 0))
        fnext = jnp.where(ridx == R - 1, nrow, pltpu.roll(cur, R - 1, 0))
        return cur + mu_ref[...] * (0.5 * (fprev + fnext) - cur)

    r = shift_mix(r_ref, rp_ref, rn_ref, mur_ref)
    k = shift_mix(k_ref, kp_ref, kn_ref, muk_ref)
    v = shift_mix(v_ref, vp_ref, vn_ref, muv_ref)
    wa = shift_mix(wa_ref, wap_ref, wan_ref, muwa_ref)
    wd_t = jnp.tanh(wa[:, :RW_LORA]).astype(BF16)
    ad = wa[:, RW_LORA:].astype(BF16)

    kk = k * kkw_ref[...]
    kk = kk / jnp.maximum(jnp.sqrt(_head_sum(kk * kk)), 1e-12)

    row = lax.broadcasted_iota(jnp.int32, (L, PAIR), 0)
    col = lax.broadcasted_iota(jnp.int32, (L, PAIR), 1) % RW_HEAD
    eye_l = jnp.where(row == col, 1.0, 0.0)
    rown = lax.broadcasted_iota(jnp.int32, (RW_HEAD, PAIR), 0)
    coln = lax.broadcasted_iota(jnp.int32, (RW_HEAD, PAIR), 1) % RW_HEAD
    eye_n = rown == coln
    trow = lax.broadcasted_iota(jnp.int32, (R, R), 0)
    tcol = lax.broadcasted_iota(jnp.int32, (R, R), 1)
    same_chunk = (trow // L) == (tcol // L)
    steps = L.bit_length() - 1
    kdir_sum = None
    dense = []
    for d in range(2):
        tri = (same_chunk & ((tcol <= trow) if d == 0 else (tcol >= trow))).astype(BF16)
        last = L - 1 if d == 0 else 0

        w_pre = w0_ref[d:d + 1, :] + jnp.dot(wd_t, wup_ref[d].astype(BF16), preferred_element_type=F32)
        w_log = -(jnp.maximum(-w_pre, 0.0) + jnp.log1p(jnp.exp(-jnp.abs(w_pre)))) - 0.5
        logw = -jnp.exp(w_log)
        lh, lm, ll = _split3(logw)
        cum = (jnp.dot(tri, lh, preferred_element_type=F32)
               + (jnp.dot(tri, lm, preferred_element_type=F32) + jnp.dot(tri, ll, preferred_element_type=F32)))
        a = jax.nn.sigmoid(a0_ref[d:d + 1, :] + jnp.dot(ad, aup_ref[d].astype(BF16), preferred_element_type=F32))

        cum_ls = [cum[j * L + last:j * L + last + 1, :] for j in range(R // L)]
        cum_l = jnp.concatenate([jnp.broadcast_to(x, (L, RW_C)) for x in cum_ls], axis=0)
        e_neg = jnp.exp(-cum)
        e_l = jnp.exp(cum_l - cum)
        w_l = [jnp.exp(x) for x in cum_ls]
        kdir = k * (1.0 + (a - 1.0) * kaw_ref[...])
        kdir_sum = kdir if kdir_sum is None else kdir_sum + kdir
        b = kk * a
        al_c = kk * jnp.exp(cum - logw)
        rho_c = r * jnp.exp(cum)
        be_c = b * e_neg
        ka_c = kdir * e_neg
        bel_c = b * e_l
        kal_c = kdir * e_l
        dense.append((al_c, rho_c, be_c, ka_c, bel_c, kal_c, w_l))

    bonus_ref[0] = _head_sum(r * kdir_sum * rkw_ref[...]) * v

    chains = [(j, d, p) for j in range(R // L) for d in range(2) for p in range(N_PAIR)]
    tn = (((0,), (0,)), ((), ()))

    def sub(x, ch):
        return x[ch[0] * L:(ch[0] + 1) * L, PAIR * ch[2]:PAIR * (ch[2] + 1)]

    def each(fn, *lists):
        return [fn(ch, *(lst[i] for lst in lists)) for i, ch in enumerate(chains)]

    def products(ch):
        al, rho, be, ka = (sub(x, ch) for x in dense[ch[1]][:4])
        lo = _lo_lanes(be.shape)
        zero = jnp.zeros_like(be)
        rhs_nt = jnp.concatenate([jnp.where(lo, be, zero), jnp.where(lo, zero, be),
                                  jnp.where(lo, ka, zero), jnp.where(lo, zero, ka)], axis=0)
        lhs = jnp.concatenate([al, rho], axis=0)
        return jnp.einsum('ld,md->lm', lhs.astype(BF16), rhs_nt.astype(BF16),
                          preferred_element_type=F32)

    ms = each(products)

    def masks(ch, m):
        d = ch[1]
        incl = (col <= row) if d == 0 else (col >= row)
        strict = (col < row) if d == 0 else (col > row)
        return (-jnp.where(strict, m[:L, :PAIR], 0.0), jnp.where(strict, m[:L, PAIR:], 0.0),
                jnp.where(incl, m[L:, :PAIR], 0.0), jnp.where(incl, m[L:, PAIR:], 0.0))

    mk = each(masks, ms)
    negs, akss, rbis, rkis = ([x[i] for x in mk] for i in range(4))

    accs = [eye_l + neg for neg in negs]
    pws = each(lambda ch, neg: _dot1(neg, _bd2(neg), P_DATA), negs)
    for _ in range(1, steps - 1):
        sts = each(lambda ch, acc, pw: _dot1(jnp.concatenate([acc, pw], axis=0), _bd2(pw), P_DATA), accs, pws)
        accs = [acc + st[:L] for acc, st in zip(accs, sts)]
        pws = [st[L:] for st in sts]
    tmats = each(lambda ch, acc, pw: acc + _dot1(acc, _bd2(pw), P_DATA), accs, pws)

    vss = each(lambda ch, aks, rki: _dot1(jnp.concatenate([aks, rki], axis=0), _bd2(sub(v, ch)), P_DATA),
               akss, rkis)
    tus = each(lambda ch, tmat, vs: _dot1(tmat, jnp.concatenate(
        [_bd2(sub(dense[ch[1]][0], ch)), _bd2(vs[:L])], axis=1), P_DATA), tmats, vss)
    uas = [tu[:, :PAIR] for tu in tus]
    d0s = [-tu[:, PAIR:] for tu in tus]
    rus = each(lambda ch, rbi, ua, d0: _dot1(rbi, jnp.concatenate([_bd2(ua), _bd2(d0)], axis=1), P_DATA),
               rbis, uas, d0s)
    gbs = each(lambda ch, ua, d0: lax.dot_general(
        sub(dense[ch[1]][4], ch).astype(BF16), jnp.concatenate([ua, d0], axis=1).astype(BF16), tn,
        preferred_element_type=F32), uas, d0s)
    gks = each(lambda ch: lax.dot_general(
        sub(dense[ch[1]][5], ch).astype(BF16), sub(v, ch).astype(BF16), tn, preferred_element_type=F32))

    for i, ch in enumerate(chains):
        j, d, p = ch
        rows = slice(j * L, (j + 1) * L)
        sl = slice(PAIR * p, PAIR * (p + 1))
        qt_ref[0, d, rows, sl] = sub(dense[d][1], ch) - rus[i][:, :PAIR]
        o0_ref[0, d, rows, sl] = rus[i][:, PAIR:] + vss[i][L:]
        g_ref[0, d, j, :, sl] = jnp.where(eye_n, dense[d][6][j][:, sl], 0.0) - _diag_blocks(gbs[i][:, :PAIR])
        h_ref[0, d, j, :, sl] = _diag_blocks(gbs[i][:, PAIR:]) + _diag_blocks(gks[i])


def _rwkv_p2_kernel(qt_ref, o0_ref, g_ref, h_ref, s0_ref, o_ref, sfin_ref, s_ref, *, nsteps, cps):
    L = RW_CHUNK
    d = pl.program_id(1)
    c = pl.program_id(2)

    @pl.when(c == 0)
    def _():
        s_ref[...] = s0_ref[0, 0]

    def body(i, s):
        ci = i + d * (cps - 1 - 2 * i)
        rows = pl.ds(pl.multiple_of(ci * L, L), L)
        qt = qt_ref[0, 0, rows, :]
        g = g_ref[0, 0, ci]
        outs = []
        for p in range(N_PAIR):
            sl = slice(PAIR * p, PAIR * (p + 1))
            outs.append(_dot1(jnp.concatenate([qt[:, sl], g[:, sl]], axis=0), _bd2(s[:, sl]), P_STATE))
        st = jnp.concatenate(outs, axis=1)
        o_ref[0, 0, rows, :] = st[:L] + o0_ref[0, 0, rows, :]
        return st[L:] + h_ref[0, 0, ci]

    s_new = lax.fori_loop(0, cps, body, s_ref[...])
    s_ref[...] = s_new

    @pl.when(c == nsteps - 1)
    def _():
        sfin_ref[0, 0] = s_new


RW_PREP_CHUNKS = 4
RW_SCAN_CHUNKS = 8
HALO16 = 2 * SUBLANE


def _rwkv(p32, p16, s0, mu, w0, w_up, a0, a_up, k_k, k_a, r_k):
    bsz, t, _ = p32.shape
    L = RW_CHUNK
    nc = t // L

    def halo_rows(name):
        return HALO16 if name == "wdad" else SUBLANE

    cpb = min(RW_PREP_CHUNKS, nc)
    R = cpb * L
    nblk = t // R

    def cur(name):
        blk = _cblk(name)
        w = COL[name][1]
        return pl.BlockSpec((1, R, w), lambda b, c: (b, c, blk))

    def prev(name):
        blk = _cblk(name)
        w = COL[name][1]
        hr = halo_rows(name)
        return pl.BlockSpec((1, hr, w), lambda b, c: (b, jnp.maximum(c * (R // hr) - 1, 0), blk))

    def nxt(name):
        blk = _cblk(name)
        w = COL[name][1]
        hr = halo_rows(name)
        return pl.BlockSpec((1, hr, w), lambda b, c: (b, jnp.minimum((c + 1) * (R // hr), t // hr - 1), blk))

    def full(a):
        nd = a.ndim
        return pl.BlockSpec(a.shape, lambda b, c: (0,) * nd)

    names = ("r", "k", "v", "wdad")
    srcs = [p32, p32, p32, p16]
    mu_p = [mu[o:o + s].reshape(1, s) for o, s in
            ((0, RW_C), (RW_C, RW_C), (2 * RW_C, RW_C), (3 * RW_C, 2 * RW_LORA))]
    small = mu_p + [w0, w_up, a0, a_up, k_k.reshape(1, RW_C), k_a.reshape(1, RW_C), r_k.reshape(1, RW_C)]
    qt, o0, g, h, bonus = pl.pallas_call(
        functools.partial(_rwkv_p1_kernel, nc=nblk),
        grid=(bsz, nblk),
        in_specs=[cur(n) for n in names] + [prev(n) for n in names] + [nxt(n) for n in names]
                 + [full(a) for a in small],
        out_specs=[pl.BlockSpec((1, 2, R, RW_C), lambda b, c: (b, 0, c, 0)),
                   pl.BlockSpec((1, 2, R, RW_C), lambda b, c: (b, 0, c, 0)),
                   pl.BlockSpec((1, 2, cpb, RW_HEAD, RW_C), lambda b, c: (b, 0, c, 0, 0)),
                   pl.BlockSpec((1, 2, cpb, RW_HEAD, RW_C), lambda b, c: (b, 0, c, 0, 0)),
                   pl.BlockSpec((1, R, RW_C), lambda b, c: (b, c, 0))],
        out_shape=[jax.ShapeDtypeStruct((bsz, 2, t, RW_C), F32),
                   jax.ShapeDtypeStruct((bsz, 2, t, RW_C), F32),
                   jax.ShapeDtypeStruct((bsz, 2, nc, RW_HEAD, RW_C), F32),
                   jax.ShapeDtypeStruct((bsz, 2, nc, RW_HEAD, RW_C), F32),
                   jax.ShapeDtypeStruct((bsz, t, RW_C), F32)],
        compiler_params=_params("parallel", "parallel"),
        name="rwkv_chunk_prep",
    )(*(srcs * 3), *small)

    cps = min(RW_SCAN_CHUNKS, nc)
    nsteps = nc // cps

    def cc(c, d):
        return c + d * (nsteps - 1 - 2 * c)

    tblk = (1, 1, cps * L, RW_C)
    gblk = (1, 1, cps, RW_HEAD, RW_C)
    sblk = (1, 1, RW_HEAD, RW_C)
    o, s_fin = pl.pallas_call(
        functools.partial(_rwkv_p2_kernel, nsteps=nsteps, cps=cps),
        grid=(bsz, 2, nsteps),
        in_specs=[pl.BlockSpec(tblk, lambda b, d, c: (b, d, cc(c, d), 0)),
                  pl.BlockSpec(tblk, lambda b, d, c: (b, d, cc(c, d), 0)),
                  pl.BlockSpec(gblk, lambda b, d, c: (b, d, cc(c, d), 0, 0)),
                  pl.BlockSpec(gblk, lambda b, d, c: (b, d, cc(c, d), 0, 0)),
                  pl.BlockSpec(sblk, lambda b, d, c: (b, d, 0, 0))],
        out_specs=[pl.BlockSpec(tblk, lambda b, d, c: (b, d, cc(c, d), 0)),
                   pl.BlockSpec(sblk, lambda b, d, c: (b, d, 0, 0))],
        out_shape=[jax.ShapeDtypeStruct((bsz, 2, t, RW_C), F32),
                   jax.ShapeDtypeStruct((bsz, 2, RW_HEAD, RW_C), F32)],
        scratch_shapes=[pltpu.VMEM((RW_HEAD, RW_C), F32)],
        compiler_params=_params("parallel", "parallel", "arbitrary"),
        name="rwkv_state_scan",
    )(qt, o0, g, h, s0)
    return o, bonus, s_fin


CONV_HALO = 16


def _conv_kernel(cur_ref, prev_ref, next_ref, w_ref, b_ref, lnw_ref, lnb_ref, o_ref, u_ref, *, nt, tt):
    i = pl.program_id(1)

    def glu(x):
        x = x.astype(F32)
        return x[:, :CONV_CH] * _sigmoid(x[:, CONV_CH:])

    u_ref[0:CONV_HALO, :] = jnp.where(i > 0, glu(prev_ref[0]), 0.0)
    u_ref[CONV_HALO:CONV_HALO + tt, :] = glu(cur_ref[0])
    u_ref[CONV_HALO + tt:, :] = jnp.where(i < nt - 1, glu(next_ref[0]), 0.0)
    acc = jnp.zeros((tt, CONV_CH), F32) + b_ref[...]
    base = CONV_HALO - CONV_K // 2
    for j in range(CONV_K):
        acc = acc + u_ref[base + j:base + j + tt, :] * w_ref[j:j + 1, :]
    mu = jnp.mean(acc, axis=-1, keepdims=True)
    xc = acc - mu
    var = jnp.mean(xc * xc, axis=-1, keepdims=True)
    un = (xc * lax.rsqrt(var + CONV_LN_EPS)) * lnw_ref[...] + lnb_ref[...]
    o_ref[0] = un * _sigmoid(un)


def _conv(p, conv_w, conv_b, ln_w, ln_b):
    bsz, t, _ = p.shape
    tt = min(512, t)
    nt = t // tt
    hb = tt // CONV_HALO
    nh = t // CONV_HALO
    blk = _cblk("cv")
    w = COL["cv"][1]
    vec = lambda a: a.reshape(1, CONV_CH)
    return pl.pallas_call(
        functools.partial(_conv_kernel, nt=nt, tt=tt),
        grid=(bsz, nt),
        in_specs=[pl.BlockSpec((1, tt, w), lambda b, i: (b, i, blk)),
                  pl.BlockSpec((1, CONV_HALO, w), lambda b, i: (b, jnp.maximum(i * hb - 1, 0), blk)),
                  pl.BlockSpec((1, CONV_HALO, w), lambda b, i: (b, jnp.minimum((i + 1) * hb, nh - 1), blk)),
                  pl.BlockSpec((CONV_K, CONV_CH), lambda b, i: (0, 0)),
                  pl.BlockSpec((1, CONV_CH), lambda b, i: (0, 0)),
                  pl.BlockSpec((1, CONV_CH), lambda b, i: (0, 0)),
                  pl.BlockSpec((1, CONV_CH), lambda b, i: (0, 0))],
        out_specs=pl.BlockSpec((1, tt, CONV_CH), lambda b, i: (b, i, 0)),
        out_shape=jax.ShapeDtypeStruct((bsz, t, CONV_CH), F32),
        scratch_shapes=[pltpu.VMEM((tt + 2 * CONV_HALO, CONV_CH), F32)],
        compiler_params=_params("parallel", "parallel"),
        name="conv_module",
    )(p, p, p, conv_w, vec(conv_b), vec(ln_w), vec(ln_b))


def _swap_halves(x, hs):
    slabs = []
    for j in range(x.shape[-1] // LANE):
        s = x[:, LANE * j:LANE * (j + 1)]
        lane = lax.broadcasted_iota(jnp.int32, s.shape, 1)
        slabs.append(jnp.where((lane % (2 * hs)) < hs, pltpu.roll(s, LANE - hs, 1), pltpu.roll(s, hs, 1)))
    return slabs[0] if len(slabs) == 1 else jnp.concatenate(slabs, axis=-1)


def _rope(x, cos, sin, hs):
    reps = x.shape[-1] // LANE
    if reps > 1:
        cos = jnp.concatenate([cos] * reps, axis=-1)
        sin = jnp.concatenate([sin] * reps, axis=-1)
    return x * cos + _swap_halves(x, hs) * sin


def _rms(x, w):
    ms = jnp.mean(x * x, axis=-1, keepdims=True)
    return (x * lax.rsqrt(ms + NORM_EPS)) * w


def _prep_kernel(*refs, rope):
    if rope:
        (dq_ref, dkv_ref, kr_ref, wq_ref, wk_ref, c64_ref, s64_ref, c32_ref, s32_ref,
         qnw_ref, qup_ref, kvnw_ref, kvup_ref, qm_ref, kmt_ref, vm_ref, wqo_ref, wko_ref) = refs
    else:
        (dq_ref, dkv_ref, kr_ref, wq_ref, wk_ref,
         qnw_ref, qup_ref, kvnw_ref, kvup_ref, qm_ref, kmt_ref, vm_ref, wqo_ref, wko_ref) = refs
    hw = MLA_HEADS * MLA_NOPE
    q = jnp.dot(_rms(dq_ref[0].astype(F32), qnw_ref[...]).astype(BF16), qup_ref[...],
                preferred_element_type=F32)
    kv = jnp.dot(_rms(dkv_ref[0].astype(F32), kvnw_ref[...]).astype(BF16), kvup_ref[...],
                 preferred_element_type=F32)
    qn, qr = q[:, :hw], q[:, hw:]
    kn, vv = kv[:, :hw], kv[:, hw:]
    kr = kr_ref[0].astype(F32)
    wq = wq_ref[0].astype(F32)
    wk = wk_ref[0].astype(F32)
    if rope:
        qr = _rope(qr, c32_ref[...], s32_ref[...], MLA_ROPE // 4)
        kr = _rope(kr, c32_ref[...], s32_ref[...], MLA_ROPE // 4)
        wq = _rope(wq, c64_ref[...], s64_ref[...], WA_HEAD // 4)
        wk = _rope(wk, c64_ref[...], s64_ref[...], WA_HEAD // 4)
    wqo_ref[0] = wq.astype(BF16)
    wko_ref[0] = wk.astype(BF16)
    qn = qn * MLA_QSCALE
    qr = qr * MLA_QSCALE
    lane = lax.broadcasted_iota(jnp.int32, (qn.shape[0], MLA_V), 1)
    ones_col = jnp.where(lane == 0, 1.0, 0.0)
    for h in range(MLA_HEADS):
        sl = slice(MLA_NOPE * h, MLA_NOPE * (h + 1))
        qm_ref[0, h] = jnp.concatenate([qn[:, sl], qr[:, sl]], axis=-1).astype(BF16)
        kh = jnp.concatenate([kn[:, sl], kr[:, :MLA_NOPE]], axis=-1)
        kmt_ref[0, h, 0] = kh.T.astype(BF16)
        vm_ref[0, h] = jnp.concatenate([vv[:, sl], ones_col], axis=-1).astype(BF16)


def _attn_prep(p32, p, tables, q_norm, q_up_p, kv_norm, kv_up_p, tk):
    bsz, t, _ = p.shape
    tm = tk
    rope = tables is not None

    def slab(name):
        blk = _cblk(name)
        return pl.BlockSpec((1, tm, COL[name][1]), lambda b, i: (b, i, blk))

    def full(a):
        nd = a.ndim
        return pl.BlockSpec(a.shape, lambda b, i: (0,) * nd)

    ins = [p, p, p, p32, p]
    in_specs = [slab(n) for n in ("dq", "dkv", "kr", "wq", "wk")]
    if rope:
        ins += list(tables)
        in_specs += [pl.BlockSpec((tm, LANE), lambda b, i: (i, 0))] * 4
    small = [q_norm.reshape(1, -1), q_up_p, kv_norm.reshape(1, -1), kv_up_p]
    ins += small
    in_specs += [full(a) for a in small]
    return pl.pallas_call(
        functools.partial(_prep_kernel, rope=rope),
        grid=(bsz, t // tm),
        in_specs=in_specs,
        out_specs=[pl.BlockSpec((1, MLA_HEADS, tm, LANE), lambda b, i: (b, 0, i, 0)),
                   pl.BlockSpec((1, MLA_HEADS, 1, LANE, tm), lambda b, i: (b, 0, i, 0, 0)),
                   pl.BlockSpec((1, MLA_HEADS, tm, LANE), lambda b, i: (b, 0, i, 0)),
                   pl.BlockSpec((1, tm, WA_HEADS * WA_HEAD), lambda b, i: (b, i, 0)),
                   pl.BlockSpec((1, tm, WA_KV_HEADS * WA_HEAD), lambda b, i: (b, i, 0))],
        out_shape=[jax.ShapeDtypeStruct((bsz, MLA_HEADS, t, LANE), BF16),
                   jax.ShapeDtypeStruct((bsz, MLA_HEADS, t // tm, LANE, tm), BF16),
                   jax.ShapeDtypeStruct((bsz, MLA_HEADS, t, LANE), BF16),
                   jax.ShapeDtypeStruct((bsz, t, WA_HEADS * WA_HEAD), BF16),
                   jax.ShapeDtypeStruct((bsz, t, WA_KV_HEADS * WA_HEAD), BF16)],
        compiler_params=_params("parallel", "parallel"),
        name="attn_prep",
    )(*ins)


def _rope_tables(t):
    pos = jnp.arange(t)
    row = (pos // GRID_W).astype(F32)[:, None]
    col = (pos % GRID_W).astype(F32)[:, None]
    out = []
    for d in (WA_HEAD, MLA_ROPE):
        q4 = d // 4
        freqs = ROPE_BASE ** (-jnp.arange(q4, dtype=F32) / q4)
        ar, ac = row * freqs[None, :], col * freqs[None, :]
        cos = jnp.concatenate([jnp.cos(ar), jnp.cos(ar), jnp.cos(ac), jnp.cos(ac)], axis=-1)
        sin = jnp.concatenate([-jnp.sin(ar), jnp.sin(ar), -jnp.sin(ac), jnp.sin(ac)], axis=-1)
        out += [jnp.tile(cos, (1, LANE // d)), jnp.tile(sin, (1, LANE // d))]
    return tuple(out)


WA_QBLOCKS = 4


def _wattn_kernel(*refs, band, nbt):
    if band:
        q_ref, kp_ref, kc_ref, kn_ref, vp_ref, vc_ref, vn_ref, kx_ref, vx_ref, sink_ref, o_ref = refs
    else:
        q_ref, kx_ref, vx_ref, sink_ref, o_ref = refs
    n = pl.program_id(1)
    tq = q_ref.shape[1]
    nq = tq // BLOCK if band else 1
    rows = tq // nq
    q = q_ref[0]
    kx = kx_ref[0]
    vx = vx_ref[0]
    if band:
        kc, vc = kc_ref[0], vc_ref[0]
        kblocks = [kp_ref[0]] + [kc[BLOCK * j:BLOCK * (j + 1)] for j in range(nq)] + [kn_ref[0]]
        vblocks = [vp_ref[0]] + [vc[BLOCK * j:BLOCK * (j + 1)] for j in range(nq)] + [vn_ref[0]]
        nk = 3 * BLOCK + kx.shape[0]
        qi = lax.broadcasted_iota(jnp.int32, (rows, nk), 0)
        kj = lax.broadcasted_iota(jnp.int32, (rows, nk), 1)
        in_band = jnp.abs(kj - BLOCK - qi) <= WINDOW
        is_ctx = kj >= 3 * BLOCK

    chains = [(j, g) for j in range(nq) for g in range(WA_KV_HEADS)]

    def scores(ch):
        j, g = ch
        ks = slice(WA_HEAD * g, WA_HEAD * (g + 1))
        qg = jnp.concatenate([q[rows * j:rows * (j + 1), WA_HEAD * (WA_GROUP * g + i):WA_HEAD * (WA_GROUP * g + i + 1)]
                              for i in range(WA_GROUP)], axis=0)
        if band:
            kb = jnp.concatenate([kblocks[j][:, ks], kblocks[j + 1][:, ks], kblocks[j + 2][:, ks], kx[:, ks]], axis=0)
        else:
            kb = kx[:, ks]
        s = jnp.einsum('qd,kd->qk', qg, kb, preferred_element_type=F32) * WA_SCALE
        if band:
            blk = n * nq + j
            ok = in_band & ((kj >= BLOCK) | (blk > 0)) & ((kj < 2 * BLOCK) | (blk < nbt - 1))
            ok = ok | is_ctx
            s = jnp.where(jnp.concatenate([ok] * WA_GROUP, axis=0), s, -jnp.inf)
        return s

    ss = [scores(ch) for ch in chains]

    def probs(ch, s):
        g = ch[1]
        sk = jnp.concatenate([jnp.full((rows, 1), sink_ref[0, WA_GROUP * g + i], F32) for i in range(WA_GROUP)],
                             axis=0)
        m = jnp.maximum(jnp.max(s, axis=-1, keepdims=True), sk)
        e = jnp.exp(s - m)
        inv = 1.0 / (jnp.sum(e, axis=-1, keepdims=True) + jnp.exp(sk - m))
        return (e * inv).astype(BF16)

    ps = [probs(ch, s) for ch, s in zip(chains, ss)]

    def values(ch, p):
        j, g = ch
        ks = slice(WA_HEAD * g, WA_HEAD * (g + 1))
        if band:
            vb = jnp.concatenate([vblocks[j][:, ks], vblocks[j + 1][:, ks], vblocks[j + 2][:, ks], vx[:, ks]], axis=0)
        else:
            vb = vx[:, ks]
        return jnp.dot(p, vb, preferred_element_type=F32)

    ogs = [values(ch, p) for ch, p in zip(chains, ps)]
    for j in range(nq):
        heads = []
        for g in range(WA_KV_HEADS):
            og = ogs[chains.index((j, g))]
            heads += [og[rows * i:rows * (i + 1)] for i in range(WA_GROUP)]
        o_ref[0, rows * j:rows * (j + 1), :] = jnp.concatenate(heads, axis=-1)


def _window_attn(wq, wk, p, wkx, px, sink, band):
    bsz, t, _ = wq.shape
    cx = wkx.shape[1]
    tq = WA_QBLOCKS * BLOCK if band else t
    nsteps = t // tq
    nbt = t // BLOCK
    vblk = _cblk("wv")
    kvw = WA_KV_HEADS * WA_HEAD
    ins = [wq]
    in_specs = [pl.BlockSpec((1, tq, WA_HEADS * WA_HEAD), lambda b, n: (b, n, 0))]
    if band:
        prev = lambda n: jnp.maximum(n * WA_QBLOCKS - 1, 0)
        nxt = lambda n: jnp.minimum((n + 1) * WA_QBLOCKS, nbt - 1)
        for src, cblk in ((wk, 0), (p, vblk)):
            ins += [src] * 3
            in_specs += [pl.BlockSpec((1, BLOCK, kvw), functools.partial(lambda c, b, n: (b, prev(n), c), cblk)),
                         pl.BlockSpec((1, tq, kvw), functools.partial(lambda c, b, n: (b, n, c), cblk)),
                         pl.BlockSpec((1, BLOCK, kvw), functools.partial(lambda c, b, n: (b, nxt(n), c), cblk))]
    ins += [wkx, px, sink.reshape(1, WA_HEADS)]
    in_specs += [pl.BlockSpec((1, cx, kvw), lambda b, n: (b, 0, 0)),
                 pl.BlockSpec((1, cx, kvw), lambda b, n: (b, 0, vblk)),
                 pl.BlockSpec(memory_space=pltpu.SMEM)]
    return pl.pallas_call(
        functools.partial(_wattn_kernel, band=band, nbt=nbt),
        grid=(bsz, nsteps),
        in_specs=in_specs,
        out_specs=pl.BlockSpec((1, tq, WA_HEADS * WA_HEAD), lambda b, n: (b, n, 0)),
        out_shape=jax.ShapeDtypeStruct((bsz, t, WA_HEADS * WA_HEAD), F32),
        compiler_params=_params("parallel", "parallel"),
        name="window_attn" if band else "context_gqa",
    )(*ins)


MLA_HPS = 2


def _mla_kernel(*refs, nkt, extra):
    if extra:
        q_ref, kt_ref, v_ref, ktx_ref, vx_ref, o_ref = refs
    else:
        q_ref, kt_ref, v_ref, o_ref = refs
    tq = q_ref.shape[2]
    tk = kt_ref.shape[4]

    heads = range(MLA_HPS)

    def update(carry, kts, vs, staged):
        def score(hh):
            return jnp.dot(q_ref[0, hh], kts[hh], preferred_element_type=F32)

        def finish(hh, s):
            m, acc = carry[hh]
            m_new = jnp.maximum(m, jnp.max(s, axis=-1, keepdims=True))
            p = jnp.exp2(s - m_new).astype(BF16)
            return m_new, acc * jnp.exp2(m - m_new) + jnp.dot(p, vs[hh], preferred_element_type=F32)

        if staged:
            ss = [score(hh) for hh in heads]
            return tuple(finish(hh, ss[hh]) for hh in heads)
        return tuple(finish(hh, score(hh)) for hh in heads)

    def body(j, carry):
        start = pl.multiple_of(j * tk, tk)
        return update(carry, [kt_ref[0, hh, j] for hh in heads],
                      [v_ref[0, hh, pl.ds(start, tk), :] for hh in heads], staged=False)

    init = (jnp.full((tq, 1), -jnp.inf, F32), jnp.zeros((tq, LANE), F32))
    carry = lax.fori_loop(0, nkt, body, (init,) * MLA_HPS)
    if extra:
        carry = update(carry, [ktx_ref[0, hh, 0] for hh in heads], [vx_ref[0, hh] for hh in heads], staged=True)
    o_ref[0] = jnp.concatenate([carry[hh][1][:, :MLA_V] / carry[hh][1][:, MLA_V:MLA_V + 1] for hh in heads],
                               axis=-1)


def _mla_attn(qm, kmt, vm, kmtx=None, vmx=None):
    bsz, nh, t, _ = qm.shape
    nkt, tk = kmt.shape[2], kmt.shape[4]
    tkv = vm.shape[2]
    tq = min(1024, t)
    extra = kmtx is not None
    ins = [qm, kmt, vm]
    in_specs = [pl.BlockSpec((1, MLA_HPS, tq, LANE), lambda b, h, i: (b, h, i, 0)),
                pl.BlockSpec((1, MLA_HPS, nkt, LANE, tk), lambda b, h, i: (b, h, 0, 0, 0)),
                pl.BlockSpec((1, MLA_HPS, tkv, LANE), lambda b, h, i: (b, h, 0, 0))]
    if extra:
        cx = vmx.shape[2]
        ins += [kmtx, vmx]
        in_specs += [pl.BlockSpec((1, MLA_HPS, 1, LANE, cx), lambda b, h, i: (b, h, 0, 0, 0)),
                     pl.BlockSpec((1, MLA_HPS, cx, LANE), lambda b, h, i: (b, h, 0, 0))]
    return pl.pallas_call(
        functools.partial(_mla_kernel, nkt=nkt, extra=extra),
        grid=(bsz, nh // MLA_HPS, t // tq),
        in_specs=in_specs,
        out_specs=pl.BlockSpec((1, tq, MLA_HPS * MLA_V), lambda b, h, i: (b, i, h)),
        out_shape=jax.ShapeDtypeStruct((bsz, t, nh * MLA_V), F32),
        compiler_params=_params("parallel", "parallel", "parallel"),
        name="mla_attn",
    )(*ins)


def _merge_kernel(x_ref, g_ref, o_ref, bonus_ref, yb_ref, yc_ref, yd_ref, z_ref, gate_ref,
                  gnw_ref, gnb_ref, wb_ref, wo_ref, fnw_ref, out_ref, *, final):
    o = o_ref[0, 0] + o_ref[0, 1]
    oc = o - _head_sum(o) * (1.0 / RW_HEAD)
    var = _head_sum(oc * oc) * (1.0 / RW_HEAD)
    ya = (oc * lax.rsqrt(var + RW_GN_EPS)) * gnw_ref[...] + gnb_ref[...] + bonus_ref[0]
    ys = (ya, yb_ref[0], yc_ref[0], yd_ref[0])
    m = None
    for i, y in enumerate(ys):
        zi = z_ref[0, :, BRANCH_W * i:BRANCH_W * (i + 1)].astype(F32)
        gi = gate_ref[0, :, D_MODEL * i:D_MODEL * (i + 1)].astype(F32)
        yz = (y * (zi * _sigmoid(zi))).astype(BF16)
        term = _sigmoid(gi) * jnp.dot(yz, wb_ref[i], preferred_element_type=F32)
        m = term if m is None else m + term
    xn = x_ref[0] + g_ref[0] * jnp.dot(m.astype(BF16), wo_ref[...], preferred_element_type=F32)
    if final:
        xn = _rms(xn, fnw_ref[...])
    out_ref[0] = xn


def _merge(x, g, o, bonus, yb, yc, yd, p, gn_w, gn_b, w_branch, w_out, final_norm_w, final):
    bsz, t, d = x.shape
    tm = min(256, t)
    zblk, gblk = _cblk("z"), _cblk("gate")
    row = lambda w: pl.BlockSpec((1, tm, w), lambda b, i: (b, i, 0))
    return pl.pallas_call(
        functools.partial(_merge_kernel, final=final),
        grid=(bsz, t // tm),
        in_specs=[row(d),
                  pl.BlockSpec((1, 1, d), lambda b, i: (b, 0, 0)),
                  pl.BlockSpec((1, 2, tm, RW_C), lambda b, i: (b, 0, i, 0)),
                  row(RW_C),
                  row(BRANCH_W), row(BRANCH_W), row(BRANCH_W),
                  pl.BlockSpec((1, tm, COL["z"][1]), lambda b, i: (b, i, zblk)),
                  pl.BlockSpec((1, tm, COL["gate"][1]), lambda b, i: (b, i, gblk)),
                  pl.BlockSpec((1, RW_C), lambda b, i: (0, 0)),
                  pl.BlockSpec((1, RW_C), lambda b, i: (0, 0)),
                  pl.BlockSpec((N_BRANCH, BRANCH_W, d), lambda b, i: (0, 0, 0)),
                  pl.BlockSpec((d, d), lambda b, i: (0, 0)),
                  pl.BlockSpec((1, d), lambda b, i: (0, 0))],
        out_specs=row(d),
        out_shape=jax.ShapeDtypeStruct((bsz, t, d), F32),
        compiler_params=_params("parallel", "parallel"),
        name="merge",
    )(x, g, o, bonus, yb, yc, yd, p, p, gn_w.reshape(1, RW_C), gn_b.reshape(1, RW_C),
      w_branch, w_out, final_norm_w.reshape(1, d))


MLA_TK = 1024


def _permute_q_up(q_up):
    r = q_up.shape[0]
    w = q_up.reshape(r, MLA_HEADS, MLA_NOPE + MLA_ROPE)
    nope = w[:, :, :MLA_NOPE].reshape(r, MLA_HEADS * MLA_NOPE)
    rope = jnp.pad(w[:, :, MLA_NOPE:], ((0, 0), (0, 0), (0, MLA_NOPE - MLA_ROPE))).reshape(r, MLA_HEADS * MLA_NOPE)
    return jnp.concatenate([nope, rope], axis=-1)


def _permute_kv_up(kv_up):
    r = kv_up.shape[0]
    w = kv_up.reshape(r, MLA_HEADS, MLA_NOPE + MLA_V)
    return jnp.concatenate([w[:, :, :MLA_NOPE].reshape(r, -1), w[:, :, MLA_NOPE:].reshape(r, -1)], axis=-1)


def kernel(x, c, ctx, c_ctx, norm_w, ada_w, ada_b, w_in, rwkv_mu, rwkv_w0, rwkv_w_up, rwkv_a0, rwkv_a_up, rwkv_k_k, rwkv_k_a, rwkv_r_k, rwkv_gn_w, rwkv_gn_b, conv_w, conv_b, conv_ln_w, conv_ln_b, attn_sink, mla_q_norm, mla_q_up, mla_kv_norm, mla_kv_up, w_branch, w_out, final_norm_w):
    bsz, t, d = x.shape
    n_ctx = ctx.shape[1]
    depth = norm_w.shape[0]
    assert t % MLA_TK == 0 and t % BLOCK == 0 and n_ctx % RW_CHUNK == 0 and n_ctx <= MLA_TK

    cvec = jnp.concatenate([c, c_ctx[None, :], jnp.zeros((SUBLANE - bsz - 1, d), F32)], axis=0)
    mod = _modulation(cvec, ada_w, ada_b)
    tables = _rope_tables(t)
    s_zero = jnp.zeros((bsz, 2, RW_HEAD, RW_C), F32)

    for l in range(depth):
        last = l == depth - 1
        mx = mod[l, :bsz][:, None, :]
        mc = jnp.broadcast_to(mod[l, bsz][None, None, :], (bsz, 1, 3 * d))
        sh_x, sc_x, g_x = mx[..., :d], mx[..., d:2 * d], mx[..., 2 * d:]
        sh_c, sc_c, g_c = mc[..., :d], mc[..., d:2 * d], mc[..., 2 * d:]
        w_p = _pack_cols(w_in[l]).astype(BF16)
        px32, px = _in_proj(x, norm_w[l], sc_x, sh_x, w_p)
        pc32, pc = _in_proj(ctx, norm_w[l], sc_c, sh_c, w_p)

        rw_args = (rwkv_mu[l], rwkv_w0[l], rwkv_w_up[l], rwkv_a0[l], rwkv_a_up[l], rwkv_k_k[l], rwkv_k_a[l],
                   rwkv_r_k[l].reshape(RW_C))
        o_c, bonus_c, s_ctx = _rwkv(pc32, pc, s_zero, *rw_args)
        o_x, bonus_x, _ = _rwkv(px32, px, s_ctx, *rw_args)

        conv_args = (conv_w[l], conv_b[l], conv_ln_w[l], conv_ln_b[l])
        yb_x = _conv(px, *conv_args)

        q_up_p = _permute_q_up(mla_q_up[l]).astype(BF16)
        kv_up_p = _permute_kv_up(mla_kv_up[l]).astype(BF16)
        prep_w = (mla_q_norm[l], q_up_p, mla_kv_norm[l], kv_up_p)
        qm_x, kmt_x, vm_x, wq_x, wk_x = _attn_prep(px32, px, tables, *prep_w, tk=MLA_TK)
        qm_c, kmt_c, vm_c, wq_c, wk_c = _attn_prep(pc32, pc, None, *prep_w, tk=n_ctx)

        yc_x = _window_attn(wq_x, wk_x, px, wk_c, pc, attn_sink[l], band=True)
        yd_x = _mla_attn(qm_x, kmt_x, vm_x, kmt_c, vm_c)

        merge_w = (rwkv_gn_w[l], rwkv_gn_b[l], w_branch[l].astype(BF16), w_out[l].astype(BF16), final_norm_w)
        x = _merge(x, g_x, o_x, bonus_x, yb_x, yc_x, yd_x, px, *merge_w, final=last)

        if not last:
            yb_c = _conv(pc, *conv_args)
            yc_c = _window_attn(wq_c, wk_c, pc, wk_c, pc, attn_sink[l], band=False)
            yd_c = _mla_attn(qm_c, kmt_c, vm_c)
            ctx = _merge(ctx, g_c, o_c, bonus_c, yb_c, yc_c, yd_c, pc, *merge_w, final=False)

    return x
```

```python
import functools

import jax
import jax.numpy as jnp
from jax import lax
from jax.experimental import pallas as pl
from jax.experimental.pallas import tpu as pltpu

F32 = jnp.float32
BF16 = jnp.bfloat16

D_MODEL = 1024
GRID_W = 64
N_BRANCH = 4
BRANCH_W = 512
RW_HEADS = 8
RW_HEAD = 64
RW_C = RW_HEADS * RW_HEAD
RW_LORA = 64
RW_GN_EPS = 64e-5
CONV_CH = 512
CONV_K = 31
CONV_LN_EPS = 1e-5
WA_HEADS = 8
WA_KV_HEADS = 2
WA_HEAD = 64
WA_GROUP = WA_HEADS // WA_KV_HEADS
WINDOW = 128
BLOCK = 128
WA_SCALE = WA_HEAD ** -0.5
MLA_HEADS = 8
MLA_Q_RANK = 256
MLA_KV_RANK = 128
MLA_NOPE = 64
MLA_ROPE = 32
MLA_V = 64
MLA_SCALE = (MLA_NOPE + MLA_ROPE) ** -0.5
MLA_QSCALE = MLA_SCALE * 1.4426950408889634
ROPE_BASE = 10000.0
NORM_EPS = 1e-6

LANE = 128
SUBLANE = 8
VMEM_LIMIT = 48 * 1024 * 1024

_SRC = {}
_off = 0
for _name, _size in (("r", RW_C), ("k", RW_C), ("v", RW_C), ("wd", RW_LORA), ("ad", RW_LORA),
                     ("cv", 2 * CONV_CH), ("wq", WA_HEADS * WA_HEAD), ("wk", WA_KV_HEADS * WA_HEAD),
                     ("wv", WA_KV_HEADS * WA_HEAD), ("dq", MLA_Q_RANK), ("dkv", MLA_KV_RANK),
                     ("kr", MLA_ROPE), ("z", N_BRANCH * BRANCH_W), ("gate", N_BRANCH * D_MODEL)):
    _SRC[_name] = (_off, _size)
    _off += _size
N_IN = _off

_PACK32 = (("r", 512), ("k", 512), ("v", 512), ("wq", 512))
_PACK16 = (("gate", 4096), ("z", 2048), ("cv", 1024), ("dq", 256), ("wdad", 128),
           ("wk", 128), ("wv", 128), ("dkv", 128), ("kr", 128), ("pad", 128))
_PACK = _PACK32 + _PACK16
COL = {}
for _pack in (_PACK32, _PACK16):
    _off = 0
    for _name, _size in _pack:
        assert _off % _size == 0
        COL[_name] = (_off, _size)
        _off += _size
N32 = sum(s for _, s in _PACK32)
N16 = sum(s for _, s in _PACK16)
N_PACK = N32 + N16


def _cblk(name):
    off, size = COL[name]
    return off // size


def _pack_cols(w):
    def src(name):
        o, s = _SRC[name]
        return w[..., o:o + s]
    pieces = []
    for name, size in _PACK:
        if name in ("z", "gate"):
            pieces.append(0.5 * src(name))
        elif name == "wdad":
            pieces += [src("wd"), src("ad")]
        elif name == "kr":
            pieces += [src("kr"), jnp.zeros(w.shape[:-1] + (size - MLA_ROPE,), w.dtype)]
        elif name == "pad":
            pieces.append(jnp.zeros(w.shape[:-1] + (size,), w.dtype))
        else:
            pieces.append(src(name))
    return jnp.concatenate(pieces, axis=-1)


def _params(*sem):
    return pltpu.CompilerParams(dimension_semantics=sem, vmem_limit_bytes=VMEM_LIMIT)


def _sigmoid(x):
    return 0.5 * jnp.tanh(0.5 * x) + 0.5


def _split2(x):
    hi = x.astype(BF16)
    lo = (x - hi.astype(F32)).astype(BF16)
    return hi, lo


def _split3(x):
    hi = x.astype(BF16)
    r1 = x - hi.astype(F32)
    mid = r1.astype(BF16)
    lo = (r1 - mid.astype(F32)).astype(BF16)
    return hi, mid, lo


def _mod_kernel(c_ref, w_ref, b_ref, o_ref):
    cv = c_ref[...]
    sc = cv * jax.nn.sigmoid(cv)
    o_ref[0] = jnp.dot(sc, w_ref[0], preferred_element_type=F32,
                       precision=lax.Precision.HIGHEST) + b_ref[0]


def _modulation(cvec, ada_w, ada_b):
    depth, d, d3 = ada_w.shape
    rows = cvec.shape[0]
    tn = 512
    return pl.pallas_call(
        _mod_kernel,
        grid=(depth, d3 // tn),
        in_specs=[pl.BlockSpec((rows, d), lambda l, j: (0, 0)),
                  pl.BlockSpec((1, d, tn), lambda l, j: (l, 0, j)),
                  pl.BlockSpec((1, 1, tn), lambda l, j: (l, 0, j))],
        out_specs=pl.BlockSpec((1, rows, tn), lambda l, j: (l, 0, j)),
        out_shape=jax.ShapeDtypeStruct((depth, rows, d3), F32),
        compiler_params=_params("parallel", "parallel"),
        name="modulation",
    )(cvec, ada_w, ada_b.reshape(depth, 1, d3))


INPROJ_TN = 1024
assert N32 % INPROJ_TN == 0 and N16 % INPROJ_TN == 0
N32_TILES = N32 // INPROJ_TN


def _inproj_kernel(x_ref, nw_ref, sc_ref, sh_ref, w_ref, o32_ref, o16_ref, h_ref):
    j = pl.program_id(2)

    @pl.when(j == 0)
    def _():
        xf = x_ref[0]
        ms = jnp.mean(xf * xf, axis=-1, keepdims=True)
        hn = (xf * lax.rsqrt(ms + NORM_EPS)) * nw_ref[...]
        h_ref[...] = (hn * (1.0 + sc_ref[0]) + sh_ref[0]).astype(BF16)

    @pl.when(j < N32_TILES)
    def _():
        o32_ref[0] = jnp.dot(h_ref[...], w_ref[...], preferred_element_type=F32)

    @pl.when(j >= N32_TILES)
    def _():
        o16_ref[0] = jnp.dot(h_ref[...], w_ref[...], preferred_element_type=F32).astype(BF16)


def _in_proj(x, norm_w, scale, shift, w_packed):
    bsz, t, d = x.shape
    tm = min(1024, t)
    tn = INPROJ_TN
    return pl.pallas_call(
        _inproj_kernel,
        grid=(bsz, t // tm, N_PACK // tn),
        in_specs=[pl.BlockSpec((1, tm, d), lambda b, i, j: (b, i, 0)),
                  pl.BlockSpec((1, d), lambda b, i, j: (0, 0)),
                  pl.BlockSpec((1, 1, d), lambda b, i, j: (b, 0, 0)),
                  pl.BlockSpec((1, 1, d), lambda b, i, j: (b, 0, 0)),
                  pl.BlockSpec((d, tn), lambda b, i, j: (0, j))],
        out_specs=[pl.BlockSpec((1, tm, tn), lambda b, i, j: (b, i, jnp.minimum(j, N32_TILES - 1))),
                   pl.BlockSpec((1, tm, tn), lambda b, i, j: (b, i, jnp.maximum(j - N32_TILES, 0)))],
        out_shape=[jax.ShapeDtypeStruct((bsz, t, N32), F32),
                   jax.ShapeDtypeStruct((bsz, t, N16), BF16)],
        scratch_shapes=[pltpu.VMEM((tm, d), BF16)],
        compiler_params=_params("parallel", "parallel", "arbitrary"),
        name="in_proj",
    )(x, norm_w.reshape(1, d), scale, shift, w_packed)


RW_CHUNK = 64
P_DATA = 1
P_STATE = 3


PAIR = 2 * RW_HEAD
N_PAIR = RW_C // PAIR


def _dot1(a, b, passes=1):
    def e(x, y):
        return jnp.dot(x, y, preferred_element_type=F32)
    if passes == 1:
        return e(a.astype(BF16), b.astype(BF16))
    ah, al = _split2(a)
    bh, bl = _split2(b)
    return e(ah, bh) + (e(ah, bl) + e(al, bh))


def _lo_lanes(shape):
    return lax.broadcasted_iota(jnp.int32, shape, len(shape) - 1) % PAIR < RW_HEAD


def _bd2(y):
    lo = _lo_lanes(y.shape)
    zero = jnp.zeros_like(y)
    return jnp.concatenate([jnp.where(lo, y, zero), jnp.where(lo, zero, y)], axis=0)


def _head_sum(x):
    outs = []
    for p in range(x.shape[-1] // PAIR):
        s = x[:, PAIR * p:PAIR * (p + 1)]
        lo = _lo_lanes(s.shape)
        s0 = jnp.sum(jnp.where(lo, s, 0.0), axis=-1, keepdims=True)
        s1 = jnp.sum(jnp.where(lo, 0.0, s), axis=-1, keepdims=True)
        outs.append(jnp.where(lo, s0, s1))
    return outs[0] if len(outs) == 1 else jnp.concatenate(outs, axis=-1)


def _rwkv_p1_kernel(r_ref, k_ref, v_ref, wa_ref, rp_ref, kp_ref, vp_ref, wap_ref,
                    rn_ref, kn_ref, vn_ref, wan_ref,
                    mur_ref, muk_ref, muv_ref, muwa_ref, w0_ref, wup_ref, a0_ref, aup_ref,
                    kkw_ref, kaw_ref, rkw_ref,
                    qt_ref, o0_ref, g_ref, h_ref, bonus_ref, *, nc):
    L = RW_CHUNK
    R = r_ref.shape[1]
    c = pl.program_id(1)

    def shift_mix(cur_ref, p_ref, n_ref, mu_ref):
        cur = cur_ref[0].astype(F32)
        hr = p_ref.shape[1]
        prow = jnp.where(c > 0, p_ref[0, hr - 1:hr, :].astype(F32), 0.0)
        nrow = jnp.where(c < nc - 1, n_ref[0, 0:1, :].astype(F32), 0.0)
        ridx = lax.broadcasted_iota(jnp.int32, cur.shape, 0)
        fprev = jnp.where(ridx == 0, prow, pltpu.roll(cur, 1, 0))
        fnext = jnp.where(ridx == R - 1, nrow, pltpu.roll(cur, R - 1, 0))
        return cur + mu_ref[...] * (0.5 * (fprev + fnext) - cur)

    r = shift_mix(r_ref, rp_ref, rn_ref, mur_ref)
    k = shift_mix(k_ref, kp_ref, kn_ref, muk_ref)
    v = shift_mix(v_ref, vp_ref, vn_ref, muv_ref)
    wa = shift_mix(wa_ref, wap_ref, wan_ref, muwa_ref)
    wd_t = jnp.tanh(wa[:, :RW_LORA]).astype(BF16)
    ad = wa[:, RW_LORA:].astype(BF16)

    kk = k * kkw_ref[...]
    kk = kk / jnp.maximum(jnp.sqrt(_head_sum(kk * kk)), 1e-12)

    row = lax.broadcasted_iota(jnp.int32, (L, PAIR), 0)
    col = lax.broadcasted_iota(jnp.int32, (L, PAIR), 1) % RW_HEAD
    eye_l = jnp.where(row == col, 1.0, 0.0)
    rown = lax.broadcasted_iota(jnp.int32, (RW_HEAD, PAIR), 0)
    coln = lax.broadcasted_iota(jnp.int32, (RW_HEAD, PAIR), 1) % RW_HEAD
    eye_n = rown == coln
    trow = lax.broadcasted_iota(jnp.int32, (R, R), 0)
    tcol = lax.broadcasted_iota(jnp.int32, (R, R), 1)
    same_chunk = (trow // L) == (tcol // L)
    steps = L.bit_length() - 1
    kdir_sum = None
    dense = []
    for d in range(2):
        tri = (same_chunk & ((tcol <= trow) if d == 0 else (tcol >= trow))).astype(BF16)
        last = L - 1 if d == 0 else 0

        w_pre = w0_ref[d:d + 1, :] + jnp.dot(wd_t, wup_ref[d].astype(BF16), preferred_element_type=F32)
        w_log = -(jnp.maximum(-w_pre, 0.0) + jnp.log1p(jnp.exp(-jnp.abs(w_pre)))) - 0.5
        logw = -jnp.exp(w_log)
        lh, lm, ll = _split3(logw)
        cum = (jnp.dot(tri, lh, preferred_element_type=F32)
               + (jnp.dot(tri, lm, preferred_element_type=F32) + jnp.dot(tri, ll, preferred_element_type=F32)))
        a = jax.nn.sigmoid(a0_ref[d:d + 1, :] + jnp.dot(ad, aup_ref[d].astype(BF16), preferred_element_type=F32))

        cum_ls = [cum[j * L + last:j * L + last + 1, :] for j in range(R // L)]
        cum_l = jnp.concatenate([jnp.broadcast_to(x, (L, RW_C)) for x in cum_ls], axis=0)
        e_neg = jnp.exp(-cum)
        e_l = jnp.exp(cum_l - cum)
        w_l = [jnp.exp(x) for x in cum_ls]
        kdir = k * (1.0 + (a - 1.0) * kaw_ref[...])
        kdir_sum = kdir if kdir_sum is None else kdir_sum + kdir
        b = kk * a
        al_c = kk * jnp.exp(cum - logw)
        rho_c = r * jnp.exp(cum)
        be_c = b * e_neg
        ka_c = kdir * e_neg
        bel_c = b * e_l
        kal_c = kdir * e_l
        dense.append((al_c, rho_c, be_c, ka_c, bel_c, kal_c, w_l))

    bonus_ref[0] = _head_sum(r * kdir_sum * rkw_ref[...]) * v

    chains = [(j, d, p) for j in range(R // L) for d in range(2) for p in range(N_PAIR)]

    def sub(x, ch):
        return x[ch[0] * L:(ch[0] + 1) * L, PAIR * ch[2]:PAIR * (ch[2] + 1)]

    def bd(y):
        return _bd2(y.astype(BF16))

    def each(fn, *lists):
        return [fn(ch, *(lst[i] for lst in lists)) for i, ch in enumerate(chains)]

    def products(ch):
        al, rho, be, ka = (sub(x, ch) for x in dense[ch[1]][:4])
        lo = _lo_lanes(be.shape)
        zero = jnp.zeros_like(be)
        rhs_nt = jnp.concatenate([jnp.where(lo, be, zero), jnp.where(lo, zero, be),
                                  jnp.where(lo, ka, zero), jnp.where(lo, zero, ka)], axis=0)
        lhs = jnp.concatenate([al, rho], axis=0)
        return jnp.einsum('ld,md->lm', lhs.astype(BF16), rhs_nt.astype(BF16),
                          preferred_element_type=F32)

    ms = each(products)

    def block_t(x):
        xt = x.T
        return jnp.concatenate([xt[:RW_HEAD], xt[RW_HEAD:]], axis=1)

    belts = each(lambda ch: block_t(sub(dense[ch[1]][4], ch)))
    kalts = each(lambda ch: block_t(sub(dense[ch[1]][5], ch)))

    def masks(ch, m):
        d = ch[1]
        incl = (col <= row) if d == 0 else (col >= row)
        strict = (col < row) if d == 0 else (col > row)
        return (-jnp.where(strict, m[:L, :PAIR], 0.0), jnp.where(strict, m[:L, PAIR:], 0.0),
                jnp.where(incl, m[L:, :PAIR], 0.0), jnp.where(incl, m[L:, PAIR:], 0.0))

    mk = each(masks, ms)
    negs, akss, rbis, rkis = ([x[i] for x in mk] for i in range(4))

    accs = [eye_l + neg for neg in negs]
    pws = each(lambda ch, neg: _dot1(neg, bd(neg), P_DATA), negs)
    for _ in range(1, steps - 1):
        sts = each(lambda ch, acc, pw: _dot1(jnp.concatenate([acc, pw], axis=0), bd(pw), P_DATA), accs, pws)
        accs = [acc + st[:L] for acc, st in zip(accs, sts)]
        pws = [st[L:] for st in sts]
    tmats = each(lambda ch, acc, pw: acc + _dot1(acc, bd(pw), P_DATA), accs, pws)

    vss = each(lambda ch, aks, rki, kalt: _dot1(jnp.concatenate([aks, rki, kalt], axis=0), bd(sub(v, ch)), P_DATA),
               akss, rkis, kalts)
    tus = each(lambda ch, tmat, vs: _dot1(tmat, jnp.concatenate(
        [bd(sub(dense[ch[1]][0], ch)), bd(vs[:L])], axis=1), P_DATA), tmats, vss)
    uas = [tu[:, :PAIR] for tu in tus]
    d0s = [-tu[:, PAIR:] for tu in tus]
    rus = each(lambda ch, rbi, belt, ua, d0: _dot1(jnp.concatenate([rbi, belt], axis=0),
                                                   jnp.concatenate([bd(ua), bd(d0)], axis=1), P_DATA),
               rbis, belts, uas, d0s)

    for i, ch in enumerate(chains):
        j, d, p = ch
        rows = slice(j * L, (j + 1) * L)
        sl = slice(PAIR * p, PAIR * (p + 1))
        qt_ref[0, d, rows, sl] = sub(dense[d][1], ch) - rus[i][:L, :PAIR]
        o0_ref[0, d, rows, sl] = rus[i][:L, PAIR:] + vss[i][L:2 * L]
        g_ref[0, d, j, :, sl] = jnp.where(eye_n, dense[d][6][j][:, sl], 0.0) - rus[i][L:, :PAIR]
        h_ref[0, d, j, :, sl] = rus[i][L:, PAIR:] + vss[i][2 * L:]


def _rwkv_p2_kernel(qtf_ref, o0f_ref, gf_ref, hf_ref, qtb_ref, o0b_ref, gb_ref, hb_ref, s0_ref,
                    of_ref, ob_ref, sfin_ref, s_ref, *, nsteps, cps):
    L = RW_CHUNK
    c = pl.program_id(1)
    dirs = ((qtf_ref, o0f_ref, gf_ref, hf_ref, of_ref), (qtb_ref, o0b_ref, gb_ref, hb_ref, ob_ref))

    @pl.when(c == 0)
    def _():
        s_ref[...] = s0_ref[0]

    def body(i, states):
        cis = (i, cps - 1 - i)
        rows = [pl.ds(pl.multiple_of(ci * L, L), L) for ci in cis]
        qts = [dirs[d][0][0, 0, rows[d], :] for d in range(2)]
        gs = [dirs[d][2][0, 0, cis[d]] for d in range(2)]
        prods = [[_dot1(jnp.concatenate([qts[d][:, PAIR * p:PAIR * (p + 1)], gs[d][:, PAIR * p:PAIR * (p + 1)]],
                                        axis=0), _bd2(states[d][:, PAIR * p:PAIR * (p + 1)]), P_STATE)
                  for p in range(N_PAIR)] for d in range(2)]
        new = []
        for d in range(2):
            st = jnp.concatenate(prods[d], axis=1)
            dirs[d][4][0, rows[d], :] = st[:L] + dirs[d][1][0, 0, rows[d], :]
            new.append(st[L:] + dirs[d][3][0, 0, cis[d]])
        return tuple(new)

    s_new = lax.fori_loop(0, cps, body, (s_ref[0], s_ref[1]))
    s_ref[0] = s_new[0]
    s_ref[1] = s_new[1]

    @pl.when(c == nsteps - 1)
    def _():
        sfin_ref[0, 0] = s_new[0]
        sfin_ref[0, 1] = s_new[1]


RW_PREP_CHUNKS = 4
RW_SCAN_CHUNKS = 8
HALO16 = 2 * SUBLANE


def _rwkv(p32, p16, s0, mu, w0, w_up, a0, a_up, k_k, k_a, r_k):
    bsz, t, _ = p32.shape
    L = RW_CHUNK
    nc = t // L

    def halo_rows(name):
        return HALO16 if name == "wdad" else SUBLANE

    cpb = min(RW_PREP_CHUNKS, nc)
    R = cpb * L
    nblk = t // R

    def cur(name):
        blk = _cblk(name)
        w = COL[name][1]
        return pl.BlockSpec((1, R, w), lambda b, c: (b, c, blk))

    def prev(name):
        blk = _cblk(name)
        w = COL[name][1]
        hr = halo_rows(name)
        return pl.BlockSpec((1, hr, w), lambda b, c: (b, jnp.maximum(c * (R // hr) - 1, 0), blk))

    def nxt(name):
        blk = _cblk(name)
        w = COL[name][1]
        hr = halo_rows(name)
        return pl.BlockSpec((1, hr, w), lambda b, c: (b, jnp.minimum((c + 1) * (R // hr), t // hr - 1), blk))

    def full(a):
        nd = a.ndim
        return pl.BlockSpec(a.shape, lambda b, c: (0,) * nd)

    names = ("r", "k", "v", "wdad")
    srcs = [p32, p32, p32, p16]
    mu_p = [mu[o:o + s].reshape(1, s) for o, s in
            ((0, RW_C), (RW_C, RW_C), (2 * RW_C, RW_C), (3 * RW_C, 2 * RW_LORA))]
    small = mu_p + [w0, w_up, a0, a_up, k_k.reshape(1, RW_C), k_a.reshape(1, RW_C), r_k.reshape(1, RW_C)]
    qt, o0, g, h, bonus = pl.pallas_call(
        functools.partial(_rwkv_p1_kernel, nc=nblk),
        grid=(bsz, nblk),
        in_specs=[cur(n) for n in names] + [prev(n) for n in names] + [nxt(n) for n in names]
                 + [full(a) for a in small],
        out_specs=[pl.BlockSpec((1, 2, R, RW_C), lambda b, c: (b, 0, c, 0)),
                   pl.BlockSpec((1, 2, R, RW_C), lambda b, c: (b, 0, c, 0)),
                   pl.BlockSpec((1, 2, cpb, RW_HEAD, RW_C), lambda b, c: (b, 0, c, 0, 0)),
                   pl.BlockSpec((1, 2, cpb, RW_HEAD, RW_C), lambda b, c: (b, 0, c, 0, 0)),
                   pl.BlockSpec((1, R, RW_C), lambda b, c: (b, c, 0))],
        out_shape=[jax.ShapeDtypeStruct((bsz, 2, t, RW_C), F32),
                   jax.ShapeDtypeStruct((bsz, 2, t, RW_C), F32),
                   jax.ShapeDtypeStruct((bsz, 2, nc, RW_HEAD, RW_C), F32),
                   jax.ShapeDtypeStruct((bsz, 2, nc, RW_HEAD, RW_C), F32),
                   jax.ShapeDtypeStruct((bsz, t, RW_C), F32)],
        compiler_params=_params("parallel", "parallel"),
        name="rwkv_chunk_prep",
    )(*(srcs * 3), *small)

    cps = min(RW_SCAN_CHUNKS, nc)
    nsteps = nc // cps

    tblk = (1, 1, cps * L, RW_C)
    gblk = (1, 1, cps, RW_HEAD, RW_C)
    sblk = (1, 2, RW_HEAD, RW_C)
    fwd_t = pl.BlockSpec(tblk, lambda b, c: (b, 0, c, 0))
    bwd_t = pl.BlockSpec(tblk, lambda b, c: (b, 1, nsteps - 1 - c, 0))
    fwd_g = pl.BlockSpec(gblk, lambda b, c: (b, 0, c, 0, 0))
    bwd_g = pl.BlockSpec(gblk, lambda b, c: (b, 1, nsteps - 1 - c, 0, 0))
    o_f, o_b, s_fin = pl.pallas_call(
        functools.partial(_rwkv_p2_kernel, nsteps=nsteps, cps=cps),
        grid=(bsz, nsteps),
        in_specs=[fwd_t, fwd_t, fwd_g, fwd_g, bwd_t, bwd_t, bwd_g, bwd_g,
                  pl.BlockSpec(sblk, lambda b, c: (b, 0, 0, 0))],
        out_specs=[pl.BlockSpec((1, cps * L, RW_C), lambda b, c: (b, c, 0)),
                   pl.BlockSpec((1, cps * L, RW_C), lambda b, c: (b, nsteps - 1 - c, 0)),
                   pl.BlockSpec(sblk, lambda b, c: (b, 0, 0, 0))],
        out_shape=[jax.ShapeDtypeStruct((bsz, t, RW_C), F32),
                   jax.ShapeDtypeStruct((bsz, t, RW_C), F32),
                   jax.ShapeDtypeStruct((bsz, 2, RW_HEAD, RW_C), F32)],
        scratch_shapes=[pltpu.VMEM((2, RW_HEAD, RW_C), F32)],
        compiler_params=_params("parallel", "arbitrary"),
        name="rwkv_state_scan",
    )(qt, o0, g, h, qt, o0, g, h, s0)
    return (o_f, o_b), bonus, s_fin


CONV_HALO = 16


def _conv_kernel(cur_ref, prev_ref, next_ref, w_ref, b_ref, lnw_ref, lnb_ref, o_ref, u_ref, *, nt, tt):
    i = pl.program_id(1)

    def glu(x):
        x = x.astype(F32)
        return x[:, :CONV_CH] * _sigmoid(x[:, CONV_CH:])

    u_ref[0:CONV_HALO, :] = jnp.where(i > 0, glu(prev_ref[0]), 0.0)
    u_ref[CONV_HALO:CONV_HALO + tt, :] = glu(cur_ref[0])
    u_ref[CONV_HALO + tt:, :] = jnp.where(i < nt - 1, glu(next_ref[0]), 0.0)
    acc = jnp.zeros((tt, CONV_CH), F32) + b_ref[...]
    base = CONV_HALO - CONV_K // 2
    for j in range(CONV_K):
        acc = acc + u_ref[base + j:base + j + tt, :] * w_ref[j:j + 1, :]
    mu = jnp.mean(acc, axis=-1, keepdims=True)
    xc = acc - mu
    var = jnp.mean(xc * xc, axis=-1, keepdims=True)
    un = (xc * lax.rsqrt(var + CONV_LN_EPS)) * lnw_ref[...] + lnb_ref[...]
    o_ref[0] = un * _sigmoid(un)


def _conv(p, conv_w, conv_b, ln_w, ln_b):
    bsz, t, _ = p.shape
    tt = min(512, t)
    nt = t // tt
    hb = tt // CONV_HALO
    nh = t // CONV_HALO
    blk = _cblk("cv")
    w = COL["cv"][1]
    vec = lambda a: a.reshape(1, CONV_CH)
    return pl.pallas_call(
        functools.partial(_conv_kernel, nt=nt, tt=tt),
        grid=(bsz, nt),
        in_specs=[pl.BlockSpec((1, tt, w), lambda b, i: (b, i, blk)),
                  pl.BlockSpec((1, CONV_HALO, w), lambda b, i: (b, jnp.maximum(i * hb - 1, 0), blk)),
                  pl.BlockSpec((1, CONV_HALO, w), lambda b, i: (b, jnp.minimum((i + 1) * hb, nh - 1), blk)),
                  pl.BlockSpec((CONV_K, CONV_CH), lambda b, i: (0, 0)),
                  pl.BlockSpec((1, CONV_CH), lambda b, i: (0, 0)),
                  pl.BlockSpec((1, CONV_CH), lambda b, i: (0, 0)),
                  pl.BlockSpec((1, CONV_CH), lambda b, i: (0, 0))],
        out_specs=pl.BlockSpec((1, tt, CONV_CH), lambda b, i: (b, i, 0)),
        out_shape=jax.ShapeDtypeStruct((bsz, t, CONV_CH), F32),
        scratch_shapes=[pltpu.VMEM((tt + 2 * CONV_HALO, CONV_CH), F32)],
        compiler_params=_params("parallel", "parallel"),
        name="conv_module",
    )(p, p, p, conv_w, vec(conv_b), vec(ln_w), vec(ln_b))


def _swap_halves(x, hs):
    slabs = []
    for j in range(x.shape[-1] // LANE):
        s = x[:, LANE * j:LANE * (j + 1)]
        lane = lax.broadcasted_iota(jnp.int32, s.shape, 1)
        slabs.append(jnp.where((lane % (2 * hs)) < hs, pltpu.roll(s, LANE - hs, 1), pltpu.roll(s, hs, 1)))
    return slabs[0] if len(slabs) == 1 else jnp.concatenate(slabs, axis=-1)


def _rope(x, cos, sin, hs):
    reps = x.shape[-1] // LANE
    if reps > 1:
        cos = jnp.concatenate([cos] * reps, axis=-1)
        sin = jnp.concatenate([sin] * reps, axis=-1)
    return x * cos + _swap_halves(x, hs) * sin


def _rms(x, w):
    ms = jnp.mean(x * x, axis=-1, keepdims=True)
    return (x * lax.rsqrt(ms + NORM_EPS)) * w


def _prep_kernel(*refs, rope):
    if rope:
        (dq_ref, dkv_ref, kr_ref, wq_ref, wk_ref, c64_ref, s64_ref, c32_ref, s32_ref,
         qnw_ref, qup_ref, kvnw_ref, kvup_ref, qm_ref, kmt_ref, vm_ref, wqo_ref, wko_ref) = refs
    else:
        (dq_ref, dkv_ref, kr_ref, wq_ref, wk_ref,
         qnw_ref, qup_ref, kvnw_ref, kvup_ref, qm_ref, kmt_ref, vm_ref, wqo_ref, wko_ref) = refs
    hw = MLA_HEADS * MLA_NOPE
    q = jnp.dot(_rms(dq_ref[0].astype(F32), qnw_ref[...]).astype(BF16), qup_ref[...],
                preferred_element_type=F32)
    kv = jnp.dot(_rms(dkv_ref[0].astype(F32), kvnw_ref[...]).astype(BF16), kvup_ref[...],
                 preferred_element_type=F32)
    qn, qr = q[:, :hw], q[:, hw:]
    kn, vv = kv[:, :hw], kv[:, hw:]
    kr = kr_ref[0].astype(F32)
    wq = wq_ref[0].astype(F32)
    wk = wk_ref[0].astype(F32)
    if rope:
        qr = _rope(qr, c32_ref[...], s32_ref[...], MLA_ROPE // 4)
        kr = _rope(kr, c32_ref[...], s32_ref[...], MLA_ROPE // 4)
        wq = _rope(wq, c64_ref[...], s64_ref[...], WA_HEAD // 4)
        wk = _rope(wk, c64_ref[...], s64_ref[...], WA_HEAD // 4)
    wqo_ref[0] = wq.astype(BF16)
    wko_ref[0] = wk.astype(BF16)
    qn = qn * MLA_QSCALE
    qr = qr * MLA_QSCALE
    lane = lax.broadcasted_iota(jnp.int32, (qn.shape[0], MLA_V), 1)
    ones_col = jnp.where(lane == 0, 1.0, 0.0)
    for h in range(MLA_HEADS):
        sl = slice(MLA_NOPE * h, MLA_NOPE * (h + 1))
        qm_ref[0, h] = jnp.concatenate([qn[:, sl], qr[:, sl]], axis=-1).astype(BF16)
        kh = jnp.concatenate([kn[:, sl], kr[:, :MLA_NOPE]], axis=-1)
        kmt_ref[0, h, 0] = kh.T.astype(BF16)
        vm_ref[0, h] = jnp.concatenate([vv[:, sl], ones_col], axis=-1).astype(BF16)


def _attn_prep(p32, p, tables, q_norm, q_up_p, kv_norm, kv_up_p, tk):
    bsz, t, _ = p.shape
    tm = tk
    rope = tables is not None

    def slab(name):
        blk = _cblk(name)
        return pl.BlockSpec((1, tm, COL[name][1]), lambda b, i: (b, i, blk))

    def full(a):
        nd = a.ndim
        return pl.BlockSpec(a.shape, lambda b, i: (0,) * nd)

    ins = [p, p, p, p32, p]
    in_specs = [slab(n) for n in ("dq", "dkv", "kr", "wq", "wk")]
    if rope:
        ins += list(tables)
        in_specs += [pl.BlockSpec((tm, LANE), lambda b, i: (i, 0))] * 4
    small = [q_norm.reshape(1, -1), q_up_p, kv_norm.reshape(1, -1), kv_up_p]
    ins += small
    in_specs += [full(a) for a in small]
    return pl.pallas_call(
        functools.partial(_prep_kernel, rope=rope),
        grid=(bsz, t // tm),
        in_specs=in_specs,
        out_specs=[pl.BlockSpec((1, MLA_HEADS, tm, LANE), lambda b, i: (b, 0, i, 0)),
                   pl.BlockSpec((1, MLA_HEADS, 1, LANE, tm), lambda b, i: (b, 0, i, 0, 0)),
                   pl.BlockSpec((1, MLA_HEADS, tm, LANE), lambda b, i: (b, 0, i, 0)),
                   pl.BlockSpec((1, tm, WA_HEADS * WA_HEAD), lambda b, i: (b, i, 0)),
                   pl.BlockSpec((1, tm, WA_KV_HEADS * WA_HEAD), lambda b, i: (b, i, 0))],
        out_shape=[jax.ShapeDtypeStruct((bsz, MLA_HEADS, t, LANE), BF16),
                   jax.ShapeDtypeStruct((bsz, MLA_HEADS, t // tm, LANE, tm), BF16),
                   jax.ShapeDtypeStruct((bsz, MLA_HEADS, t, LANE), BF16),
                   jax.ShapeDtypeStruct((bsz, t, WA_HEADS * WA_HEAD), BF16),
                   jax.ShapeDtypeStruct((bsz, t, WA_KV_HEADS * WA_HEAD), BF16)],
        compiler_params=_params("parallel", "parallel"),
        name="attn_prep",
    )(*ins)


def _rope_tables(t):
    pos = jnp.arange(t)
    row = (pos // GRID_W).astype(F32)[:, None]
    col = (pos % GRID_W).astype(F32)[:, None]
    out = []
    for d in (WA_HEAD, MLA_ROPE):
        q4 = d // 4
        freqs = ROPE_BASE ** (-jnp.arange(q4, dtype=F32) / q4)
        ar, ac = row * freqs[None, :], col * freqs[None, :]
        cos = jnp.concatenate([jnp.cos(ar), jnp.cos(ar), jnp.cos(ac), jnp.cos(ac)], axis=-1)
        sin = jnp.concatenate([-jnp.sin(ar), jnp.sin(ar), -jnp.sin(ac), jnp.sin(ac)], axis=-1)
        out += [jnp.tile(cos, (1, LANE // d)), jnp.tile(sin, (1, LANE // d))]
    return tuple(out)


WA_QBLOCKS = 4


def _wattn_kernel(*refs, band, nbt):
    if band:
        q_ref, kp_ref, kc_ref, kn_ref, vp_ref, vc_ref, vn_ref, kx_ref, vx_ref, sink_ref, o_ref = refs
    else:
        q_ref, kx_ref, vx_ref, sink_ref, o_ref = refs
    n = pl.program_id(1)
    tq = q_ref.shape[1]
    nq = tq // BLOCK if band else 1
    rows = tq // nq
    q = q_ref[0]
    kx = kx_ref[0]
    vx = vx_ref[0]
    if band:
        kc, vc = kc_ref[0], vc_ref[0]
        kblocks = [kp_ref[0]] + [kc[BLOCK * j:BLOCK * (j + 1)] for j in range(nq)] + [kn_ref[0]]
        vblocks = [vp_ref[0]] + [vc[BLOCK * j:BLOCK * (j + 1)] for j in range(nq)] + [vn_ref[0]]
        nk = 3 * BLOCK + kx.shape[0]
        qi = lax.broadcasted_iota(jnp.int32, (rows, nk), 0)
        kj = lax.broadcasted_iota(jnp.int32, (rows, nk), 1)
        in_band = jnp.abs(kj - BLOCK - qi) <= WINDOW
        is_ctx = kj >= 3 * BLOCK

    chains = [(j, g) for j in range(nq) for g in range(WA_KV_HEADS)]

    def scores(ch):
        j, g = ch
        ks = slice(WA_HEAD * g, WA_HEAD * (g + 1))
        qg = jnp.concatenate([q[rows * j:rows * (j + 1), WA_HEAD * (WA_GROUP * g + i):WA_HEAD * (WA_GROUP * g + i + 1)]
                              for i in range(WA_GROUP)], axis=0)
        if band:
            kb = jnp.concatenate([kblocks[j][:, ks], kblocks[j + 1][:, ks], kblocks[j + 2][:, ks], kx[:, ks]], axis=0)
        else:
            kb = kx[:, ks]
        s = jnp.einsum('qd,kd->qk', qg, kb, preferred_element_type=F32) * WA_SCALE
        if band:
            blk = n * nq + j
            ok = in_band & ((kj >= BLOCK) | (blk > 0)) & ((kj < 2 * BLOCK) | (blk < nbt - 1))
            ok = ok | is_ctx
            s = jnp.where(jnp.concatenate([ok] * WA_GROUP, axis=0), s, -jnp.inf)
        return s

    ss = [scores(ch) for ch in chains]

    def probs(ch, s):
        g = ch[1]
        sk = jnp.concatenate([jnp.full((rows, 1), sink_ref[0, WA_GROUP * g + i], F32) for i in range(WA_GROUP)],
                             axis=0)
        m = jnp.maximum(jnp.max(s, axis=-1, keepdims=True), sk)
        e = jnp.exp(s - m)
        inv = 1.0 / (jnp.sum(e, axis=-1, keepdims=True) + jnp.exp(sk - m))
        return (e * inv).astype(BF16)

    ps = [probs(ch, s) for ch, s in zip(chains, ss)]

    def values(ch, p):
        j, g = ch
        ks = slice(WA_HEAD * g, WA_HEAD * (g + 1))
        if band:
            vb = jnp.concatenate([vblocks[j][:, ks], vblocks[j + 1][:, ks], vblocks[j + 2][:, ks], vx[:, ks]], axis=0)
        else:
            vb = vx[:, ks]
        return jnp.dot(p, vb, preferred_element_type=F32)

    ogs = [values(ch, p) for ch, p in zip(chains, ps)]
    for j in range(nq):
        heads = []
        for g in range(WA_KV_HEADS):
            og = ogs[chains.index((j, g))]
            heads += [og[rows * i:rows * (i + 1)] for i in range(WA_GROUP)]
        o_ref[0, rows * j:rows * (j + 1), :] = jnp.concatenate(heads, axis=-1)


def _window_attn(wq, wk, p, wkx, px, sink, band):
    bsz, t, _ = wq.shape
    cx = wkx.shape[1]
    tq = WA_QBLOCKS * BLOCK if band else t
    nsteps = t // tq
    nbt = t // BLOCK
    vblk = _cblk("wv")
    kvw = WA_KV_HEADS * WA_HEAD
    ins = [wq]
    in_specs = [pl.BlockSpec((1, tq, WA_HEADS * WA_HEAD), lambda b, n: (b, n, 0))]
    if band:
        prev = lambda n: jnp.maximum(n * WA_QBLOCKS - 1, 0)
        nxt = lambda n: jnp.minimum((n + 1) * WA_QBLOCKS, nbt - 1)
        for src, cblk in ((wk, 0), (p, vblk)):
            ins += [src] * 3
            in_specs += [pl.BlockSpec((1, BLOCK, kvw), functools.partial(lambda c, b, n: (b, prev(n), c), cblk)),
                         pl.BlockSpec((1, tq, kvw), functools.partial(lambda c, b, n: (b, n, c), cblk)),
                         pl.BlockSpec((1, BLOCK, kvw), functools.partial(lambda c, b, n: (b, nxt(n), c), cblk))]
    ins += [wkx, px, sink.reshape(1, WA_HEADS)]
    in_specs += [pl.BlockSpec((1, cx, kvw), lambda b, n: (b, 0, 0)),
                 pl.BlockSpec((1, cx, kvw), lambda b, n: (b, 0, vblk)),
                 pl.BlockSpec(memory_space=pltpu.SMEM)]
    return pl.pallas_call(
        functools.partial(_wattn_kernel, band=band, nbt=nbt),
        grid=(bsz, nsteps),
        in_specs=in_specs,
        out_specs=pl.BlockSpec((1, tq, WA_HEADS * WA_HEAD), lambda b, n: (b, n, 0)),
        out_shape=jax.ShapeDtypeStruct((bsz, t, WA_HEADS * WA_HEAD), F32),
        compiler_params=_params("parallel", "parallel"),
        name="window_attn" if band else "context_gqa",
    )(*ins)


MLA_HPS = 2


def _mla_kernel(*refs, nkt, extra):
    if extra:
        q_ref, kt_ref, v_ref, ktx_ref, vx_ref, o_ref = refs
    else:
        q_ref, kt_ref, v_ref, o_ref = refs
    tq = q_ref.shape[2]
    tk = kt_ref.shape[4]

    heads = range(MLA_HPS)

    def update(carry, kts, vs, staged):
        def score(hh):
            return jnp.dot(q_ref[0, hh], kts[hh], preferred_element_type=F32)

        def finish(hh, s):
            m, acc = carry[hh]
            m_new = jnp.maximum(m, jnp.max(s, axis=-1, keepdims=True))
            p = jnp.exp2(s - m_new).astype(BF16)
            return m_new, acc * jnp.exp2(m - m_new) + jnp.dot(p, vs[hh], preferred_element_type=F32)

        if staged:
            ss = [score(hh) for hh in heads]
            return tuple(finish(hh, ss[hh]) for hh in heads)
        return tuple(finish(hh, score(hh)) for hh in heads)

    def body(j, carry):
        start = pl.multiple_of(j * tk, tk)
        return update(carry, [kt_ref[0, hh, j] for hh in heads],
                      [v_ref[0, hh, pl.ds(start, tk), :] for hh in heads], staged=False)

    init = (jnp.full((tq, 1), -jnp.inf, F32), jnp.zeros((tq, LANE), F32))
    carry = lax.fori_loop(0, nkt, body, (init,) * MLA_HPS)
    if extra:
        carry = update(carry, [ktx_ref[0, hh, 0] for hh in heads], [vx_ref[0, hh] for hh in heads], staged=True)
    o_ref[0] = jnp.concatenate([carry[hh][1][:, :MLA_V] / carry[hh][1][:, MLA_V:MLA_V + 1] for hh in heads],
                               axis=-1)


def _mla_attn(qm, kmt, vm, kmtx=None, vmx=None):
    bsz, nh, t, _ = qm.shape
    nkt, tk = kmt.shape[2], kmt.shape[4]
    tkv = vm.shape[2]
    tq = min(1024, t)
    extra = kmtx is not None
    ins = [qm, kmt, vm]
    in_specs = [pl.BlockSpec((1, MLA_HPS, tq, LANE), lambda b, h, i: (b, h, i, 0)),
                pl.BlockSpec((1, MLA_HPS, nkt, LANE, tk), lambda b, h, i: (b, h, 0, 0, 0)),
                pl.BlockSpec((1, MLA_HPS, tkv, LANE), lambda b, h, i: (b, h, 0, 0))]
    if extra:
        cx = vmx.shape[2]
        ins += [kmtx, vmx]
        in_specs += [pl.BlockSpec((1, MLA_HPS, 1, LANE, cx), lambda b, h, i: (b, h, 0, 0, 0)),
                     pl.BlockSpec((1, MLA_HPS, cx, LANE), lambda b, h, i: (b, h, 0, 0))]
    return pl.pallas_call(
        functools.partial(_mla_kernel, nkt=nkt, extra=extra),
        grid=(bsz, nh // MLA_HPS, t // tq),
        in_specs=in_specs,
        out_specs=pl.BlockSpec((1, tq, MLA_HPS * MLA_V), lambda b, h, i: (b, i, h)),
        out_shape=jax.ShapeDtypeStruct((bsz, t, nh * MLA_V), F32),
        compiler_params=_params("parallel", "parallel", "parallel"),
        name="mla_attn",
    )(*ins)


def _merge_kernel(x_ref, g_ref, of_ref, ob_ref, bonus_ref, yb_ref, yc_ref, yd_ref, z_ref, gate_ref,
                  gnw_ref, gnb_ref, wb_ref, wo_ref, fnw_ref, out_ref, *, final):
    o = of_ref[0] + ob_ref[0]
    oc = o - _head_sum(o) * (1.0 / RW_HEAD)
    var = _head_sum(oc * oc) * (1.0 / RW_HEAD)
    ya = (oc * lax.rsqrt(var + RW_GN_EPS)) * gnw_ref[...] + gnb_ref[...] + bonus_ref[0]
    ys = (ya, yb_ref[0], yc_ref[0], yd_ref[0])
    m = None
    for i, y in enumerate(ys):
        zh = z_ref[0, :, BRANCH_W * i:BRANCH_W * (i + 1)].astype(F32)
        gh = gate_ref[0, :, D_MODEL * i:D_MODEL * (i + 1)].astype(F32)
        yz = (y * (zh * jnp.tanh(zh) + zh)).astype(BF16)
        term = (jnp.tanh(gh) + 1.0) * jnp.dot(yz, wb_ref[i], preferred_element_type=F32)
        m = term if m is None else m + term
    xn = x_ref[0] + g_ref[0] * jnp.dot(m.astype(BF16), wo_ref[...], preferred_element_type=F32)
    if final:
        xn = _rms(xn, fnw_ref[...])
    out_ref[0] = xn


def _merge(x, g, o, bonus, yb, yc, yd, p, gn_w, gn_b, w_branch, w_out, final_norm_w, final):
    bsz, t, d = x.shape
    tm = min(256, t)
    zblk, gblk = _cblk("z"), _cblk("gate")
    row = lambda w: pl.BlockSpec((1, tm, w), lambda b, i: (b, i, 0))
    return pl.pallas_call(
        functools.partial(_merge_kernel, final=final),
        grid=(bsz, t // tm),
        in_specs=[row(d),
                  pl.BlockSpec((1, 1, d), lambda b, i: (b, 0, 0)),
                  row(RW_C), row(RW_C), row(RW_C),
                  row(BRANCH_W), row(BRANCH_W), row(BRANCH_W),
                  pl.BlockSpec((1, tm, COL["z"][1]), lambda b, i: (b, i, zblk)),
                  pl.BlockSpec((1, tm, COL["gate"][1]), lambda b, i: (b, i, gblk)),
                  pl.BlockSpec((1, RW_C), lambda b, i: (0, 0)),
                  pl.BlockSpec((1, RW_C), lambda b, i: (0, 0)),
                  pl.BlockSpec((N_BRANCH, BRANCH_W, d), lambda b, i: (0, 0, 0)),
                  pl.BlockSpec((d, d), lambda b, i: (0, 0)),
                  pl.BlockSpec((1, d), lambda b, i: (0, 0))],
        out_specs=row(d),
        out_shape=jax.ShapeDtypeStruct((bsz, t, d), F32),
        compiler_params=_params("parallel", "parallel"),
        name="merge",
    )(x, g, o[0], o[1], bonus, yb, yc, yd, p, p, gn_w.reshape(1, RW_C), gn_b.reshape(1, RW_C),
      w_branch, w_out, final_norm_w.reshape(1, d))


MLA_TK = 1024


def _permute_q_up(q_up):
    r = q_up.shape[0]
    w = q_up.reshape(r, MLA_HEADS, MLA_NOPE + MLA_ROPE)
    nope = w[:, :, :MLA_NOPE].reshape(r, MLA_HEADS * MLA_NOPE)
    rope = jnp.pad(w[:, :, MLA_NOPE:], ((0, 0), (0, 0), (0, MLA_NOPE - MLA_ROPE))).reshape(r, MLA_HEADS * MLA_NOPE)
    return jnp.concatenate([nope, rope], axis=-1)


def _permute_kv_up(kv_up):
    r = kv_up.shape[0]
    w = kv_up.reshape(r, MLA_HEADS, MLA_NOPE + MLA_V)
    return jnp.concatenate([w[:, :, :MLA_NOPE].reshape(r, -1), w[:, :, MLA_NOPE:].reshape(r, -1)], axis=-1)


def kernel(x, c, ctx, c_ctx, norm_w, ada_w, ada_b, w_in, rwkv_mu, rwkv_w0, rwkv_w_up, rwkv_a0, rwkv_a_up, rwkv_k_k, rwkv_k_a, rwkv_r_k, rwkv_gn_w, rwkv_gn_b, conv_w, conv_b, conv_ln_w, conv_ln_b, attn_sink, mla_q_norm, mla_q_up, mla_kv_norm, mla_kv_up, w_branch, w_out, final_norm_w):
    bsz, t, d = x.shape
    n_ctx = ctx.shape[1]
    depth = norm_w.shape[0]
    assert t % MLA_TK == 0 and t % BLOCK == 0 and n_ctx % RW_CHUNK == 0 and n_ctx <= MLA_TK

    cvec = jnp.concatenate([c, c_ctx[None, :], jnp.zeros((SUBLANE - bsz - 1, d), F32)], axis=0)
    mod = _modulation(cvec, ada_w, ada_b)
    tables = _rope_tables(t)
    s_zero = jnp.zeros((bsz, 2, RW_HEAD, RW_C), F32)

    for l in range(depth):
        last = l == depth - 1
        mx = mod[l, :bsz][:, None, :]
        mc = jnp.broadcast_to(mod[l, bsz][None, None, :], (bsz, 1, 3 * d))
        sh_x, sc_x, g_x = mx[..., :d], mx[..., d:2 * d], mx[..., 2 * d:]
        sh_c, sc_c, g_c = mc[..., :d], mc[..., d:2 * d], mc[..., 2 * d:]
        w_p = _pack_cols(w_in[l]).astype(BF16)
        px32, px = _in_proj(x, norm_w[l], sc_x, sh_x, w_p)
        pc32, pc = _in_proj(ctx, norm_w[l], sc_c, sh_c, w_p)

        rw_args = (rwkv_mu[l], rwkv_w0[l], rwkv_w_up[l], rwkv_a0[l], rwkv_a_up[l], rwkv_k_k[l], rwkv_k_a[l],
                   rwkv_r_k[l].reshape(RW_C))
        o_c, bonus_c, s_ctx = _rwkv(pc32, pc, s_zero, *rw_args)
        o_x, bonus_x, _ = _rwkv(px32, px, s_ctx, *rw_args)

        conv_args = (conv_w[l], conv_b[l], conv_ln_w[l], conv_ln_b[l])
        yb_x = _conv(px, *conv_args)

        q_up_p = _permute_q_up(mla_q_up[l]).astype(BF16)
        kv_up_p = _permute_kv_up(mla_kv_up[l]).astype(BF16)
        prep_w = (mla_q_norm[l], q_up_p, mla_kv_norm[l], kv_up_p)
        qm_x, kmt_x, vm_x, wq_x, wk_x = _attn_prep(px32, px, tables, *prep_w, tk=MLA_TK)
        qm_c, kmt_c, vm_c, wq_c, wk_c = _attn_prep(pc32, pc, None, *prep_w, tk=n_ctx)

        yc_x = _window_attn(wq_x, wk_x, px, wk_c, pc, attn_sink[l], band=True)
        yd_x = _mla_attn(qm_x, kmt_x, vm_x, kmt_c, vm_c)

        merge_w = (rwkv_gn_w[l], rwkv_gn_b[l], w_branch[l].astype(BF16), (0.5 * w_out[l]).astype(BF16),
                   final_norm_w)
        x = _merge(x, g_x, o_x, bonus_x, yb_x, yc_x, yd_x, px, *merge_w, final=last)

        if not last:
            yb_c = _conv(pc, *conv_args)
            yc_c = _window_attn(wq_c, wk_c, pc, wk_c, pc, attn_sink[l], band=False)
            yd_c = _mla_attn(qm_c, kmt_c, vm_c)
            ctx = _merge(ctx, g_c, o_c, bonus_c, yb_c, yc_c, yd_c, pc, *merge_w, final=False)

    return x
```

```python
import functools

import jax
import jax.numpy as jnp
from jax import lax
from jax.experimental import pallas as pl
from jax.experimental.pallas import tpu as pltpu

F32 = jnp.float32
BF16 = jnp.bfloat16

D_MODEL = 1024
GRID_W = 64
N_BRANCH = 4
BRANCH_W = 512
RW_HEADS = 8
RW_HEAD = 64
RW_C = RW_HEADS * RW_HEAD
RW_LORA = 64
RW_GN_EPS = 64e-5
CONV_CH = 512
CONV_K = 31
CONV_LN_EPS = 1e-5
WA_HEADS = 8
WA_KV_HEADS = 2
WA_HEAD = 64
WA_GROUP = WA_HEADS // WA_KV_HEADS
WINDOW = 128
BLOCK = 128
WA_SCALE = WA_HEAD ** -0.5
MLA_HEADS = 8
MLA_Q_RANK = 256
MLA_KV_RANK = 128
MLA_NOPE = 64
MLA_ROPE = 32
MLA_V = 64
MLA_SCALE = (MLA_NOPE + MLA_ROPE) ** -0.5
MLA_QSCALE = MLA_SCALE * 1.4426950408889634
ROPE_BASE = 10000.0
NORM_EPS = 1e-6

LANE = 128
SUBLANE = 8
VMEM_LIMIT = 48 * 1024 * 1024

_SRC = {}
_off = 0
for _name, _size in (("r", RW_C), ("k", RW_C), ("v", RW_C), ("wd", RW_LORA), ("ad", RW_LORA),
                     ("cv", 2 * CONV_CH), ("wq", WA_HEADS * WA_HEAD), ("wk", WA_KV_HEADS * WA_HEAD),
                     ("wv", WA_KV_HEADS * WA_HEAD), ("dq", MLA_Q_RANK), ("dkv", MLA_KV_RANK),
                     ("kr", MLA_ROPE), ("z", N_BRANCH * BRANCH_W), ("gate", N_BRANCH * D_MODEL)):
    _SRC[_name] = (_off, _size)
    _off += _size
N_IN = _off

_PACK32 = (("r", 512), ("k", 512), ("v", 512), ("wq", 512))
_PACK16 = (("gate", 4096), ("z", 2048), ("cv", 1024), ("dq", 256), ("wdad", 128),
           ("wk", 128), ("wv", 128), ("dkv", 128), ("kr", 128), ("pad", 128))
_PACK = _PACK32 + _PACK16
COL = {}
for _pack in (_PACK32, _PACK16):
    _off = 0
    for _name, _size in _pack:
        assert _off % _size == 0
        COL[_name] = (_off, _size)
        _off += _size
N32 = sum(s for _, s in _PACK32)
N16 = sum(s for _, s in _PACK16)
N_PACK = N32 + N16


def _cblk(name):
    off, size = COL[name]
    return off // size


def _pack_cols(w):
    def src(name):
        o, s = _SRC[name]
        return w[..., o:o + s]
    pieces = []
    for name, size in _PACK:
        if name in ("z", "gate"):
            pieces.append(0.5 * src(name))
        elif name == "wdad":
            pieces += [src("wd"), src("ad")]
        elif name == "kr":
            pieces += [src("kr"), jnp.zeros(w.shape[:-1] + (size - MLA_ROPE,), w.dtype)]
        elif name == "pad":
            pieces.append(jnp.zeros(w.shape[:-1] + (size,), w.dtype))
        else:
            pieces.append(src(name))
    return jnp.concatenate(pieces, axis=-1)


def _params(*sem):
    return pltpu.CompilerParams(dimension_semantics=sem, vmem_limit_bytes=VMEM_LIMIT)


def _sigmoid(x):
    return 0.5 * jnp.tanh(0.5 * x) + 0.5


def _split2(x):
    hi = x.astype(BF16)
    lo = (x - hi.astype(F32)).astype(BF16)
    return hi, lo


def _split3(x):
    hi = x.astype(BF16)
    r1 = x - hi.astype(F32)
    mid = r1.astype(BF16)
    lo = (r1 - mid.astype(F32)).astype(BF16)
    return hi, mid, lo


def _mod_kernel(c_ref, w_ref, b_ref, o_ref):
    cv = c_ref[...]
    sc = cv * jax.nn.sigmoid(cv)
    o_ref[0] = jnp.dot(sc, w_ref[0], preferred_element_type=F32,
                       precision=lax.Precision.HIGHEST) + b_ref[0]


def _modulation(cvec, ada_w, ada_b):
    depth, d, d3 = ada_w.shape
    rows = cvec.shape[0]
    tn = 512
    return pl.pallas_call(
        _mod_kernel,
        grid=(depth, d3 // tn),
        in_specs=[pl.BlockSpec((rows, d), lambda l, j: (0, 0)),
                  pl.BlockSpec((1, d, tn), lambda l, j: (l, 0, j)),
                  pl.BlockSpec((1, 1, tn), lambda l, j: (l, 0, j))],
        out_specs=pl.BlockSpec((1, rows, tn), lambda l, j: (l, 0, j)),
        out_shape=jax.ShapeDtypeStruct((depth, rows, d3), F32),
        compiler_params=_params("parallel", "parallel"),
        name="modulation",
    )(cvec, ada_w, ada_b.reshape(depth, 1, d3))


INPROJ_TN = 1024
assert N32 % INPROJ_TN == 0 and N16 % INPROJ_TN == 0
N32_TILES = N32 // INPROJ_TN


def _inproj_kernel(x_ref, nw_ref, sc_ref, sh_ref, w_ref, o32_ref, o16_ref, h_ref):
    j = pl.program_id(2)

    @pl.when(j == 0)
    def _():
        xf = x_ref[0]
        ms = jnp.mean(xf * xf, axis=-1, keepdims=True)
        hn = (xf * lax.rsqrt(ms + NORM_EPS)) * nw_ref[...]
        h_ref[...] = (hn * (1.0 + sc_ref[0]) + sh_ref[0]).astype(BF16)

    @pl.when(j < N32_TILES)
    def _():
        o32_ref[0] = jnp.dot(h_ref[...], w_ref[...], preferred_element_type=F32)

    @pl.when(j >= N32_TILES)
    def _():
        o16_ref[0] = jnp.dot(h_ref[...], w_ref[...], preferred_element_type=F32).astype(BF16)


def _in_proj(x, norm_w, scale, shift, w_packed):
    bsz, t, d = x.shape
    tm = min(1024, t)
    tn = INPROJ_TN
    return pl.pallas_call(
        _inproj_kernel,
        grid=(bsz, t // tm, N_PACK // tn),
        in_specs=[pl.BlockSpec((1, tm, d), lambda b, i, j: (b, i, 0)),
                  pl.BlockSpec((1, d), lambda b, i, j: (0, 0)),
                  pl.BlockSpec((1, 1, d), lambda b, i, j: (b, 0, 0)),
                  pl.BlockSpec((1, 1, d), lambda b, i, j: (b, 0, 0)),
                  pl.BlockSpec((d, tn), lambda b, i, j: (0, j))],
        out_specs=[pl.BlockSpec((1, tm, tn), lambda b, i, j: (b, i, jnp.minimum(j, N32_TILES - 1))),
                   pl.BlockSpec((1, tm, tn), lambda b, i, j: (b, i, jnp.maximum(j - N32_TILES, 0)))],
        out_shape=[jax.ShapeDtypeStruct((bsz, t, N32), F32),
                   jax.ShapeDtypeStruct((bsz, t, N16), BF16)],
        scratch_shapes=[pltpu.VMEM((tm, d), BF16)],
        compiler_params=_params("parallel", "parallel", "arbitrary"),
        name="in_proj",
    )(x, norm_w.reshape(1, d), scale, shift, w_packed)


RW_CHUNK = 64
P_DATA = 1
P_STATE = 3


PAIR = 2 * RW_HEAD
N_PAIR = RW_C // PAIR


def _dot1(a, b, passes=1):
    def e(x, y):
        return jnp.dot(x, y, preferred_element_type=F32)
    if passes == 1:
        return e(a.astype(BF16), b.astype(BF16))
    ah, al = _split2(a)
    bh, bl = _split2(b)
    return e(ah, bh) + (e(ah, bl) + e(al, bh))


def _lo_lanes(shape):
    return lax.broadcasted_iota(jnp.int32, shape, len(shape) - 1) % PAIR < RW_HEAD


def _bd2(y):
    lo = _lo_lanes(y.shape)
    zero = jnp.zeros_like(y)
    return jnp.concatenate([jnp.where(lo, y, zero), jnp.where(lo, zero, y)], axis=0)


def _head_sum(x):
    outs = []
    for p in range(x.shape[-1] // PAIR):
        s = x[:, PAIR * p:PAIR * (p + 1)]
        lo = _lo_lanes(s.shape)
        s0 = jnp.sum(jnp.where(lo, s, 0.0), axis=-1, keepdims=True)
        s1 = jnp.sum(jnp.where(lo, 0.0, s), axis=-1, keepdims=True)
        outs.append(jnp.where(lo, s0, s1))
    return outs[0] if len(outs) == 1 else jnp.concatenate(outs, axis=-1)


def _rwkv_p1_kernel(r_ref, k_ref, v_ref, wa_ref, rp_ref, kp_ref, vp_ref, wap_ref,
                    rn_ref, kn_ref, vn_ref, wan_ref,
                    mur_ref, muk_ref, muv_ref, muwa_ref, w0_ref, wup_ref, a0_ref, aup_ref,
                    kkw_ref, kaw_ref, rkw_ref,
                    qt_ref, o0_ref, g_ref, h_ref, bonus_ref, *, nc):
    L = RW_CHUNK
    R = r_ref.shape[1]
    c = pl.program_id(1)

    def shift_mix(cur_ref, p_ref, n_ref, mu_ref):
        cur = cur_ref[0].astype(F32)
        hr = p_ref.shape[1]
        prow = jnp.where(c > 0, p_ref[0, hr - 1:hr, :].astype(F32), 0.0)
        nrow = jnp.where(c < nc - 1, n_ref[0, 0:1, :].astype(F32), 0.0)
        ridx = lax.broadcasted_iota(jnp.int32, cur.shape, 0)
        fprev = jnp.where(ridx == 0, prow, pltpu.roll(cur, 1, 0))
        fnext = jnp.where(ridx == R - 1, nrow, pltpu.roll(cur, R - 1, 0))
        return cur + mu_ref[...] * (0.5 * (fprev + fnext) - cur)

    r = shift_mix(r_ref, rp_ref, rn_ref, mur_ref)
    k = shift_mix(k_ref, kp_ref, kn_ref, muk_ref)
    v = shift_mix(v_ref, vp_ref, vn_ref, muv_ref)
    wa = shift_mix(wa_ref, wap_ref, wan_ref, muwa_ref)
    wd_t = jnp.tanh(wa[:, :RW_LORA]).astype(BF16)
    ad = wa[:, RW_LORA:].astype(BF16)

    kk = k * kkw_ref[...]
    kk = kk / jnp.maximum(jnp.sqrt(_head_sum(kk * kk)), 1e-12)

    row = lax.broadcasted_iota(jnp.int32, (L, PAIR), 0)
    col = lax.broadcasted_iota(jnp.int32, (L, PAIR), 1) % RW_HEAD
    eye_l = jnp.where(row == col, 1.0, 0.0)
    rown = lax.broadcasted_iota(jnp.int32, (RW_HEAD, PAIR), 0)
    coln = lax.broadcasted_iota(jnp.int32, (RW_HEAD, PAIR), 1) % RW_HEAD
    eye_n = rown == coln
    trow = lax.broadcasted_iota(jnp.int32, (R, R), 0)
    tcol = lax.broadcasted_iota(jnp.int32, (R, R), 1)
    same_chunk = (trow // L) == (tcol // L)
    steps = L.bit_length() - 1
    kdir_sum = None
    dense = []
    for d in range(2):
        tri = (same_chunk & ((tcol <= trow) if d == 0 else (tcol >= trow))).astype(BF16)
        last = L - 1 if d == 0 else 0

        w_pre = w0_ref[d:d + 1, :] + jnp.dot(wd_t, wup_ref[d].astype(BF16), preferred_element_type=F32)
        w_log = -(jnp.maximum(-w_pre, 0.0) + jnp.log1p(jnp.exp(-jnp.abs(w_pre)))) - 0.5
        logw = -jnp.exp(w_log)
        lh, lm, ll = _split3(logw)
        cum = (jnp.dot(tri, lh, preferred_element_type=F32)
               + (jnp.dot(tri, lm, preferred_element_type=F32) + jnp.dot(tri, ll, preferred_element_type=F32)))
        a = jax.nn.sigmoid(a0_ref[d:d + 1, :] + jnp.dot(ad, aup_ref[d].astype(BF16), preferred_element_type=F32))

        cum_ls = [cum[j * L + last:j * L + last + 1, :] for j in range(R // L)]
        cum_l = jnp.concatenate([jnp.broadcast_to(x, (L, RW_C)) for x in cum_ls], axis=0)
        e_neg = jnp.exp(-cum)
        e_l = jnp.exp(cum_l - cum)
        w_l = [jnp.exp(x) for x in cum_ls]
        kdir = k * (1.0 + (a - 1.0) * kaw_ref[...])
        kdir_sum = kdir if kdir_sum is None else kdir_sum + kdir
        b = kk * a
        al_c = kk * jnp.exp(cum - logw)
        rho_c = r * jnp.exp(cum)
        be_c = b * e_neg
        ka_c = kdir * e_neg
        bel_c = b * e_l
        kal_c = kdir * e_l
        dense.append((al_c, rho_c, be_c, ka_c, bel_c, kal_c, w_l))

    bonus_ref[0] = _head_sum(r * kdir_sum * rkw_ref[...]) * v

    chains = [(j, d, p) for j in range(R // L) for d in range(2) for p in range(N_PAIR)]

    def sub(x, ch):
        return x[ch[0] * L:(ch[0] + 1) * L, PAIR * ch[2]:PAIR * (ch[2] + 1)]

    def bd(y):
        return _bd2(y.astype(BF16))

    def each(fn, *lists):
        return [fn(ch, *(lst[i] for lst in lists)) for i, ch in enumerate(chains)]

    def products(ch):
        al, rho, be, ka = (sub(x, ch) for x in dense[ch[1]][:4])
        lo = _lo_lanes(be.shape)
        zero = jnp.zeros_like(be)
        rhs_nt = jnp.concatenate([jnp.where(lo, be, zero), jnp.where(lo, zero, be),
                                  jnp.where(lo, ka, zero), jnp.where(lo, zero, ka)], axis=0)
        lhs = jnp.concatenate([al, rho], axis=0)
        return jnp.einsum('ld,md->lm', lhs.astype(BF16), rhs_nt.astype(BF16),
                          preferred_element_type=F32)

    ms = each(products)

    def block_t(x):
        xt = x.T
        return jnp.concatenate([xt[:RW_HEAD], xt[RW_HEAD:]], axis=1)

    belts = each(lambda ch: block_t(sub(dense[ch[1]][4], ch)))
    kalts = each(lambda ch: block_t(sub(dense[ch[1]][5], ch)))

    def masks(ch, m):
        d = ch[1]
        incl = (col <= row) if d == 0 else (col >= row)
        strict = (col < row) if d == 0 else (col > row)
        return (-jnp.where(strict, m[:L, :PAIR], 0.0), jnp.where(strict, m[:L, PAIR:], 0.0),
                jnp.where(incl, m[L:, :PAIR], 0.0), jnp.where(incl, m[L:, PAIR:], 0.0))

    mk = each(masks, ms)
    negs, akss, rbis, rkis = ([x[i] for x in mk] for i in range(4))

    accs = [eye_l + neg for neg in negs]
    pws = each(lambda ch, neg: _dot1(neg, bd(neg), P_DATA), negs)
    for _ in range(1, steps - 1):
        sts = each(lambda ch, acc, pw: _dot1(jnp.concatenate([acc, pw], axis=0), bd(pw), P_DATA), accs, pws)
        accs = [acc + st[:L] for acc, st in zip(accs, sts)]
        pws = [st[L:] for st in sts]
    tmats = each(lambda ch, acc, pw: acc + _dot1(acc, bd(pw), P_DATA), accs, pws)

    vss = each(lambda ch, aks, rki, kalt: _dot1(jnp.concatenate([aks, rki, kalt], axis=0), bd(sub(v, ch)), P_DATA),
               akss, rkis, kalts)
    tus = each(lambda ch, tmat, vs: _dot1(tmat, jnp.concatenate(
        [bd(sub(dense[ch[1]][0], ch)), bd(vs[:L])], axis=1), P_DATA), tmats, vss)
    uas = [tu[:, :PAIR] for tu in tus]
    d0s = [-tu[:, PAIR:] for tu in tus]
    rus = each(lambda ch, rbi, belt, ua, d0: _dot1(jnp.concatenate([rbi, belt], axis=0),
                                                   jnp.concatenate([bd(ua), bd(d0)], axis=1), P_DATA),
               rbis, belts, uas, d0s)

    for i, ch in enumerate(chains):
        j, d, p = ch
        rows = slice(j * L, (j + 1) * L)
        sl = slice(PAIR * p, PAIR * (p + 1))
        qt_ref[0, d, rows, sl] = sub(dense[d][1], ch) - rus[i][:L, :PAIR]
        o0_ref[0, d, rows, sl] = rus[i][:L, PAIR:] + vss[i][L:2 * L]
        g_ref[0, d, j, :, sl] = jnp.where(eye_n, dense[d][6][j][:, sl], 0.0) - rus[i][L:, :PAIR]
        h_ref[0, d, j, :, sl] = rus[i][L:, PAIR:] + vss[i][2 * L:]


def _rwkv_p2_kernel(qtf_ref, o0f_ref, gf_ref, hf_ref, qtb_ref, o0b_ref, gb_ref, hb_ref, s0_ref,
                    of_ref, ob_ref, sfin_ref, s_ref, *, nsteps, cps):
    L = RW_CHUNK
    c = pl.program_id(1)
    dirs = ((qtf_ref, o0f_ref, gf_ref, hf_ref, of_ref), (qtb_ref, o0b_ref, gb_ref, hb_ref, ob_ref))

    @pl.when(c == 0)
    def _():
        s_ref[...] = s0_ref[0]

    def body(i, states):
        cis = (i, cps - 1 - i)
        rows = [pl.ds(pl.multiple_of(ci * L, L), L) for ci in cis]
        qts = [dirs[d][0][0, 0, rows[d], :] for d in range(2)]
        gs = [dirs[d][2][0, 0, cis[d]] for d in range(2)]
        prods = [[_dot1(jnp.concatenate([qts[d][:, PAIR * p:PAIR * (p + 1)], gs[d][:, PAIR * p:PAIR * (p + 1)]],
                                        axis=0), _bd2(states[d][:, PAIR * p:PAIR * (p + 1)]), P_STATE)
                  for p in range(N_PAIR)] for d in range(2)]
        new = []
        for d in range(2):
            st = jnp.concatenate(prods[d], axis=1)
            dirs[d][4][0, rows[d], :] = st[:L] + dirs[d][1][0, 0, rows[d], :]
            new.append(st[L:] + dirs[d][3][0, 0, cis[d]])
        return tuple(new)

    s_new = lax.fori_loop(0, cps, body, (s_ref[0], s_ref[1]), unroll=True)
    s_ref[0] = s_new[0]
    s_ref[1] = s_new[1]

    @pl.when(c == nsteps - 1)
    def _():
        sfin_ref[0, 0] = s_new[0]
        sfin_ref[0, 1] = s_new[1]


RW_PREP_CHUNKS = 4
RW_SCAN_CHUNKS = 8
HALO16 = 2 * SUBLANE


def _rwkv(p32, p16, s0, mu, w0, w_up, a0, a_up, k_k, k_a, r_k):
    bsz, t, _ = p32.shape
    L = RW_CHUNK
    nc = t // L

    def halo_rows(name):
        return HALO16 if name == "wdad" else SUBLANE

    cpb = min(RW_PREP_CHUNKS, nc)
    R = cpb * L
    nblk = t // R

    def cur(name):
        blk = _cblk(name)
        w = COL[name][1]
        return pl.BlockSpec((1, R, w), lambda b, c: (b, c, blk))

    def prev(name):
        blk = _cblk(name)
        w = COL[name][1]
        hr = halo_rows(name)
        return pl.BlockSpec((1, hr, w), lambda b, c: (b, jnp.maximum(c * (R // hr) - 1, 0), blk))

    def nxt(name):
        blk = _cblk(name)
        w = COL[name][1]
        hr = halo_rows(name)
        return pl.BlockSpec((1, hr, w), lambda b, c: (b, jnp.minimum((c + 1) * (R // hr), t // hr - 1), blk))

    def full(a):
        nd = a.ndim
        return pl.BlockSpec(a.shape, lambda b, c: (0,) * nd)

    names = ("r", "k", "v", "wdad")
    srcs = [p32, p32, p32, p16]
    mu_p = [mu[o:o + s].reshape(1, s) for o, s in
            ((0, RW_C), (RW_C, RW_C), (2 * RW_C, RW_C), (3 * RW_C, 2 * RW_LORA))]
    small = mu_p + [w0, w_up, a0, a_up, k_k.reshape(1, RW_C), k_a.reshape(1, RW_C), r_k.reshape(1, RW_C)]
    qt, o0, g, h, bonus = pl.pallas_call(
        functools.partial(_rwkv_p1_kernel, nc=nblk),
        grid=(bsz, nblk),
        in_specs=[cur(n) for n in names] + [prev(n) for n in names] + [nxt(n) for n in names]
                 + [full(a) for a in small],
        out_specs=[pl.BlockSpec((1, 2, R, RW_C), lambda b, c: (b, 0, c, 0)),
                   pl.BlockSpec((1, 2, R, RW_C), lambda b, c: (b, 0, c, 0)),
                   pl.BlockSpec((1, 2, cpb, RW_HEAD, RW_C), lambda b, c: (b, 0, c, 0, 0)),
                   pl.BlockSpec((1, 2, cpb, RW_HEAD, RW_C), lambda b, c: (b, 0, c, 0, 0)),
                   pl.BlockSpec((1, R, RW_C), lambda b, c: (b, c, 0))],
        out_shape=[jax.ShapeDtypeStruct((bsz, 2, t, RW_C), F32),
                   jax.ShapeDtypeStruct((bsz, 2, t, RW_C), F32),
                   jax.ShapeDtypeStruct((bsz, 2, nc, RW_HEAD, RW_C), F32),
                   jax.ShapeDtypeStruct((bsz, 2, nc, RW_HEAD, RW_C), F32),
                   jax.ShapeDtypeStruct((bsz, t, RW_C), F32)],
        compiler_params=_params("parallel", "parallel"),
        name="rwkv_chunk_prep",
    )(*(srcs * 3), *small)

    cps = min(RW_SCAN_CHUNKS, nc)
    nsteps = nc // cps

    tblk = (1, 1, cps * L, RW_C)
    gblk = (1, 1, cps, RW_HEAD, RW_C)
    sblk = (1, 2, RW_HEAD, RW_C)
    fwd_t = pl.BlockSpec(tblk, lambda b, c: (b, 0, c, 0))
    bwd_t = pl.BlockSpec(tblk, lambda b, c: (b, 1, nsteps - 1 - c, 0))
    fwd_g = pl.BlockSpec(gblk, lambda b, c: (b, 0, c, 0, 0))
    bwd_g = pl.BlockSpec(gblk, lambda b, c: (b, 1, nsteps - 1 - c, 0, 0))
    o_f, o_b, s_fin = pl.pallas_call(
        functools.partial(_rwkv_p2_kernel, nsteps=nsteps, cps=cps),
        grid=(bsz, nsteps),
        in_specs=[fwd_t, fwd_t, fwd_g, fwd_g, bwd_t, bwd_t, bwd_g, bwd_g,
                  pl.BlockSpec(sblk, lambda b, c: (b, 0, 0, 0))],
        out_specs=[pl.BlockSpec((1, cps * L, RW_C), lambda b, c: (b, c, 0)),
                   pl.BlockSpec((1, cps * L, RW_C), lambda b, c: (b, nsteps - 1 - c, 0)),
                   pl.BlockSpec(sblk, lambda b, c: (b, 0, 0, 0))],
        out_shape=[jax.ShapeDtypeStruct((bsz, t, RW_C), F32),
                   jax.ShapeDtypeStruct((bsz, t, RW_C), F32),
                   jax.ShapeDtypeStruct((bsz, 2, RW_HEAD, RW_C), F32)],
        scratch_shapes=[pltpu.VMEM((2, RW_HEAD, RW_C), F32)],
        compiler_params=_params("parallel", "arbitrary"),
        name="rwkv_state_scan",
    )(qt, o0, g, h, qt, o0, g, h, s0)
    return (o_f, o_b), bonus, s_fin


CONV_HALO = 16


def _conv_kernel(cur_ref, prev_ref, next_ref, w_ref, b_ref, lnw_ref, lnb_ref, o_ref, u_ref, *, nt, tt):
    i = pl.program_id(1)

    def glu(x):
        x = x.astype(F32)
        return x[:, :CONV_CH] * _sigmoid(x[:, CONV_CH:])

    u_ref[0:CONV_HALO, :] = jnp.where(i > 0, glu(prev_ref[0]), 0.0)
    u_ref[CONV_HALO:CONV_HALO + tt, :] = glu(cur_ref[0])
    u_ref[CONV_HALO + tt:, :] = jnp.where(i < nt - 1, glu(next_ref[0]), 0.0)
    acc = jnp.zeros((tt, CONV_CH), F32) + b_ref[...]
    base = CONV_HALO - CONV_K // 2
    for j in range(CONV_K):
        acc = acc + u_ref[base + j:base + j + tt, :] * w_ref[j:j + 1, :]
    mu = jnp.mean(acc, axis=-1, keepdims=True)
    xc = acc - mu
    var = jnp.mean(xc * xc, axis=-1, keepdims=True)
    un = (xc * lax.rsqrt(var + CONV_LN_EPS)) * lnw_ref[...] + lnb_ref[...]
    o_ref[0] = un * _sigmoid(un)


def _conv(p, conv_w, conv_b, ln_w, ln_b):
    bsz, t, _ = p.shape
    tt = min(512, t)
    nt = t // tt
    hb = tt // CONV_HALO
    nh = t // CONV_HALO
    blk = _cblk("cv")
    w = COL["cv"][1]
    vec = lambda a: a.reshape(1, CONV_CH)
    return pl.pallas_call(
        functools.partial(_conv_kernel, nt=nt, tt=tt),
        grid=(bsz, nt),
        in_specs=[pl.BlockSpec((1, tt, w), lambda b, i: (b, i, blk)),
                  pl.BlockSpec((1, CONV_HALO, w), lambda b, i: (b, jnp.maximum(i * hb - 1, 0), blk)),
                  pl.BlockSpec((1, CONV_HALO, w), lambda b, i: (b, jnp.minimum((i + 1) * hb, nh - 1), blk)),
                  pl.BlockSpec((CONV_K, CONV_CH), lambda b, i: (0, 0)),
                  pl.BlockSpec((1, CONV_CH), lambda b, i: (0, 0)),
                  pl.BlockSpec((1, CONV_CH), lambda b, i: (0, 0)),
                  pl.BlockSpec((1, CONV_CH), lambda b, i: (0, 0))],
        out_specs=pl.BlockSpec((1, tt, CONV_CH), lambda b, i: (b, i, 0)),
        out_shape=jax.ShapeDtypeStruct((bsz, t, CONV_CH), F32),
        scratch_shapes=[pltpu.VMEM((tt + 2 * CONV_HALO, CONV_CH), F32)],
        compiler_params=_params("parallel", "parallel"),
        name="conv_module",
    )(p, p, p, conv_w, vec(conv_b), vec(ln_w), vec(ln_b))


def _swap_halves(x, hs):
    slabs = []
    for j in range(x.shape[-1] // LANE):
        s = x[:, LANE * j:LANE * (j + 1)]
        lane = lax.broadcasted_iota(jnp.int32, s.shape, 1)
        slabs.append(jnp.where((lane % (2 * hs)) < hs, pltpu.roll(s, LANE - hs, 1), pltpu.roll(s, hs, 1)))
    return slabs[0] if len(slabs) == 1 else jnp.concatenate(slabs, axis=-1)


def _rope(x, cos, sin, hs):
    reps = x.shape[-1] // LANE
    if reps > 1:
        cos = jnp.concatenate([cos] * reps, axis=-1)
        sin = jnp.concatenate([sin] * reps, axis=-1)
    return x * cos + _swap_halves(x, hs) * sin


def _rms(x, w):
    ms = jnp.mean(x * x, axis=-1, keepdims=True)
    return (x * lax.rsqrt(ms + NORM_EPS)) * w


def _prep_kernel(*refs, rope):
    if rope:
        (dq_ref, dkv_ref, kr_ref, wq_ref, wk_ref, c64_ref, s64_ref, c32_ref, s32_ref,
         qnw_ref, qup_ref, kvnw_ref, kvup_ref, qm_ref, kmt_ref, vm_ref, wqo_ref, wko_ref) = refs
    else:
        (dq_ref, dkv_ref, kr_ref, wq_ref, wk_ref,
         qnw_ref, qup_ref, kvnw_ref, kvup_ref, qm_ref, kmt_ref, vm_ref, wqo_ref, wko_ref) = refs
    hw = MLA_HEADS * MLA_NOPE
    q = jnp.dot(_rms(dq_ref[0].astype(F32), qnw_ref[...]).astype(BF16), qup_ref[...],
                preferred_element_type=F32)
    kv = jnp.dot(_rms(dkv_ref[0].astype(F32), kvnw_ref[...]).astype(BF16), kvup_ref[...],
                 preferred_element_type=F32)
    qn, qr = q[:, :hw], q[:, hw:]
    kn, vv = kv[:, :hw], kv[:, hw:]
    kr = kr_ref[0].astype(F32)
    wq = wq_ref[0].astype(F32)
    wk = wk_ref[0].astype(F32)
    if rope:
        qr = _rope(qr, c32_ref[...], s32_ref[...], MLA_ROPE // 4)
        kr = _rope(kr, c32_ref[...], s32_ref[...], MLA_ROPE // 4)
        wq = _rope(wq, c64_ref[...], s64_ref[...], WA_HEAD // 4)
        wk = _rope(wk, c64_ref[...], s64_ref[...], WA_HEAD // 4)
    wqo_ref[0] = wq.astype(BF16)
    wko_ref[0] = wk.astype(BF16)
    qn = qn * MLA_QSCALE
    qr = qr * MLA_QSCALE
    lane = lax.broadcasted_iota(jnp.int32, (qn.shape[0], MLA_V), 1)
    ones_col = jnp.where(lane == 0, 1.0, 0.0)
    for h in range(MLA_HEADS):
        sl = slice(MLA_NOPE * h, MLA_NOPE * (h + 1))
        qm_ref[0, h] = jnp.concatenate([qn[:, sl], qr[:, sl]], axis=-1).astype(BF16)
        kh = jnp.concatenate([kn[:, sl], kr[:, :MLA_NOPE]], axis=-1)
        kmt_ref[0, h, 0] = kh.T.astype(BF16)
        vm_ref[0, h] = jnp.concatenate([vv[:, sl], ones_col], axis=-1).astype(BF16)


def _attn_prep(p32, p, tables, q_norm, q_up_p, kv_norm, kv_up_p, tk):
    bsz, t, _ = p.shape
    tm = tk
    rope = tables is not None

    def slab(name):
        blk = _cblk(name)
        return pl.BlockSpec((1, tm, COL[name][1]), lambda b, i: (b, i, blk))

    def full(a):
        nd = a.ndim
        return pl.BlockSpec(a.shape, lambda b, i: (0,) * nd)

    ins = [p, p, p, p32, p]
    in_specs = [slab(n) for n in ("dq", "dkv", "kr", "wq", "wk")]
    if rope:
        ins += list(tables)
        in_specs += [pl.BlockSpec((tm, LANE), lambda b, i: (i, 0))] * 4
    small = [q_norm.reshape(1, -1), q_up_p, kv_norm.reshape(1, -1), kv_up_p]
    ins += small
    in_specs += [full(a) for a in small]
    return pl.pallas_call(
        functools.partial(_prep_kernel, rope=rope),
        grid=(bsz, t // tm),
        in_specs=in_specs,
        out_specs=[pl.BlockSpec((1, MLA_HEADS, tm, LANE), lambda b, i: (b, 0, i, 0)),
                   pl.BlockSpec((1, MLA_HEADS, 1, LANE, tm), lambda b, i: (b, 0, i, 0, 0)),
                   pl.BlockSpec((1, MLA_HEADS, tm, LANE), lambda b, i: (b, 0, i, 0)),
                   pl.BlockSpec((1, tm, WA_HEADS * WA_HEAD), lambda b, i: (b, i, 0)),
                   pl.BlockSpec((1, tm, WA_KV_HEADS * WA_HEAD), lambda b, i: (b, i, 0))],
        out_shape=[jax.ShapeDtypeStruct((bsz, MLA_HEADS, t, LANE), BF16),
                   jax.ShapeDtypeStruct((bsz, MLA_HEADS, t // tm, LANE, tm), BF16),
                   jax.ShapeDtypeStruct((bsz, MLA_HEADS, t, LANE), BF16),
                   jax.ShapeDtypeStruct((bsz, t, WA_HEADS * WA_HEAD), BF16),
                   jax.ShapeDtypeStruct((bsz, t, WA_KV_HEADS * WA_HEAD), BF16)],
        compiler_params=_params("parallel", "parallel"),
        name="attn_prep",
    )(*ins)


def _rope_tables(t):
    pos = jnp.arange(t)
    row = (pos // GRID_W).astype(F32)[:, None]
    col = (pos % GRID_W).astype(F32)[:, None]
    out = []
    for d in (WA_HEAD, MLA_ROPE):
        q4 = d // 4
        freqs = ROPE_BASE ** (-jnp.arange(q4, dtype=F32) / q4)
        ar, ac = row * freqs[None, :], col * freqs[None, :]
        cos = jnp.concatenate([jnp.cos(ar), jnp.cos(ar), jnp.cos(ac), jnp.cos(ac)], axis=-1)
        sin = jnp.concatenate([-jnp.sin(ar), jnp.sin(ar), -jnp.sin(ac), jnp.sin(ac)], axis=-1)
        out += [jnp.tile(cos, (1, LANE // d)), jnp.tile(sin, (1, LANE // d))]
    return tuple(out)


WA_QBLOCKS = 4


def _wattn_kernel(*refs, band, nbt):
    if band:
        q_ref, kp_ref, kc_ref, kn_ref, vp_ref, vc_ref, vn_ref, kx_ref, vx_ref, sink_ref, o_ref = refs
    else:
        q_ref, kx_ref, vx_ref, sink_ref, o_ref = refs
    n = pl.program_id(1)
    tq = q_ref.shape[1]
    nq = tq // BLOCK if band else 1
    rows = tq // nq
    q = q_ref[0]
    kx = kx_ref[0]
    vx = vx_ref[0]
    if band:
        kc, vc = kc_ref[0], vc_ref[0]
        kblocks = [kp_ref[0]] + [kc[BLOCK * j:BLOCK * (j + 1)] for j in range(nq)] + [kn_ref[0]]
        vblocks = [vp_ref[0]] + [vc[BLOCK * j:BLOCK * (j + 1)] for j in range(nq)] + [vn_ref[0]]
        nk = 3 * BLOCK + kx.shape[0]
        qi = lax.broadcasted_iota(jnp.int32, (rows, nk), 0)
        kj = lax.broadcasted_iota(jnp.int32, (rows, nk), 1)
        in_band = jnp.abs(kj - BLOCK - qi) <= WINDOW
        is_ctx = kj >= 3 * BLOCK

    chains = [(j, g) for j in range(nq) for g in range(WA_KV_HEADS)]

    def scores(ch):
        j, g = ch
        ks = slice(WA_HEAD * g, WA_HEAD * (g + 1))
        qg = jnp.concatenate([q[rows * j:rows * (j + 1), WA_HEAD * (WA_GROUP * g + i):WA_HEAD * (WA_GROUP * g + i + 1)]
                              for i in range(WA_GROUP)], axis=0)
        if band:
            kb = jnp.concatenate([kblocks[j][:, ks], kblocks[j + 1][:, ks], kblocks[j + 2][:, ks], kx[:, ks]], axis=0)
        else:
            kb = kx[:, ks]
        s = jnp.einsum('qd,kd->qk', qg, kb, preferred_element_type=F32) * WA_SCALE
        if band:
            blk = n * nq + j
            ok = in_band & ((kj >= BLOCK) | (blk > 0)) & ((kj < 2 * BLOCK) | (blk < nbt - 1))
            ok = ok | is_ctx
            s = jnp.where(jnp.concatenate([ok] * WA_GROUP, axis=0), s, -jnp.inf)
        return s

    ss = [scores(ch) for ch in chains]

    def probs(ch, s):
        g = ch[1]
        sk = jnp.concatenate([jnp.full((rows, 1), sink_ref[0, WA_GROUP * g + i], F32) for i in range(WA_GROUP)],
                             axis=0)
        m = jnp.maximum(jnp.max(s, axis=-1, keepdims=True), sk)
        e = jnp.exp(s - m)
        inv = 1.0 / (jnp.sum(e, axis=-1, keepdims=True) + jnp.exp(sk - m))
        return (e * inv).astype(BF16)

    ps = [probs(ch, s) for ch, s in zip(chains, ss)]

    def values(ch, p):
        j, g = ch
        ks = slice(WA_HEAD * g, WA_HEAD * (g + 1))
        if band:
            vb = jnp.concatenate([vblocks[j][:, ks], vblocks[j + 1][:, ks], vblocks[j + 2][:, ks], vx[:, ks]], axis=0)
        else:
            vb = vx[:, ks]
        return jnp.dot(p, vb, preferred_element_type=F32)

    ogs = [values(ch, p) for ch, p in zip(chains, ps)]
    for j in range(nq):
        heads = []
        for g in range(WA_KV_HEADS):
            og = ogs[chains.index((j, g))]
            heads += [og[rows * i:rows * (i + 1)] for i in range(WA_GROUP)]
        o_ref[0, rows * j:rows * (j + 1), :] = jnp.concatenate(heads, axis=-1)


def _window_attn(wq, wk, p, wkx, px, sink, band):
    bsz, t, _ = wq.shape
    cx = wkx.shape[1]
    tq = WA_QBLOCKS * BLOCK if band else t
    nsteps = t // tq
    nbt = t // BLOCK
    vblk = _cblk("wv")
    kvw = WA_KV_HEADS * WA_HEAD
    ins = [wq]
    in_specs = [pl.BlockSpec((1, tq, WA_HEADS * WA_HEAD), lambda b, n: (b, n, 0))]
    if band:
        prev = lambda n: jnp.maximum(n * WA_QBLOCKS - 1, 0)
        nxt = lambda n: jnp.minimum((n + 1) * WA_QBLOCKS, nbt - 1)
        for src, cblk in ((wk, 0), (p, vblk)):
            ins += [src] * 3
            in_specs += [pl.BlockSpec((1, BLOCK, kvw), functools.partial(lambda c, b, n: (b, prev(n), c), cblk)),
                         pl.BlockSpec((1, tq, kvw), functools.partial(lambda c, b, n: (b, n, c), cblk)),
                         pl.BlockSpec((1, BLOCK, kvw), functools.partial(lambda c, b, n: (b, nxt(n), c), cblk))]
    ins += [wkx, px, sink.reshape(1, WA_HEADS)]
    in_specs += [pl.BlockSpec((1, cx, kvw), lambda b, n: (b, 0, 0)),
                 pl.BlockSpec((1, cx, kvw), lambda b, n: (b, 0, vblk)),
                 pl.BlockSpec(memory_space=pltpu.SMEM)]
    return pl.pallas_call(
        functools.partial(_wattn_kernel, band=band, nbt=nbt),
        grid=(bsz, nsteps),
        in_specs=in_specs,
        out_specs=pl.BlockSpec((1, tq, WA_HEADS * WA_HEAD), lambda b, n: (b, n, 0)),
        out_shape=jax.ShapeDtypeStruct((bsz, t, WA_HEADS * WA_HEAD), F32),
        compiler_params=_params("parallel", "parallel"),
        name="window_attn" if band else "context_gqa",
    )(*ins)


MLA_HPS = 2


def _mla_kernel(*refs, nkt, extra):
    if extra:
        q_ref, kt_ref, v_ref, ktx_ref, vx_ref, o_ref = refs
    else:
        q_ref, kt_ref, v_ref, o_ref = refs
    tq = q_ref.shape[2]
    tk = kt_ref.shape[4]

    heads = range(MLA_HPS)

    def update(carry, kts, vs, staged):
        def score(hh):
            return jnp.dot(q_ref[0, hh], kts[hh], preferred_element_type=F32)

        def finish(hh, s):
            m, acc = carry[hh]
            m_new = jnp.maximum(m, jnp.max(s, axis=-1, keepdims=True))
            p = jnp.exp2(s - m_new).astype(BF16)
            return m_new, acc * jnp.exp2(m - m_new) + jnp.dot(p, vs[hh], preferred_element_type=F32)

        if staged:
            ss = [score(hh) for hh in heads]
            return tuple(finish(hh, ss[hh]) for hh in heads)
        return tuple(finish(hh, score(hh)) for hh in heads)

    def body(j, carry):
        start = pl.multiple_of(j * tk, tk)
        return update(carry, [kt_ref[0, hh, j] for hh in heads],
                      [v_ref[0, hh, pl.ds(start, tk), :] for hh in heads], staged=False)

    init = (jnp.full((tq, 1), -jnp.inf, F32), jnp.zeros((tq, LANE), F32))
    carry = lax.fori_loop(0, nkt, body, (init,) * MLA_HPS, unroll=True)
    if extra:
        carry = update(carry, [ktx_ref[0, hh, 0] for hh in heads], [vx_ref[0, hh] for hh in heads], staged=True)
    o_ref[0] = jnp.concatenate([carry[hh][1][:, :MLA_V] / carry[hh][1][:, MLA_V:MLA_V + 1] for hh in heads],
                               axis=-1)


def _mla_attn(qm, kmt, vm, kmtx=None, vmx=None):
    bsz, nh, t, _ = qm.shape
    nkt, tk = kmt.shape[2], kmt.shape[4]
    tkv = vm.shape[2]
    tq = min(1024, t)
    extra = kmtx is not None
    ins = [qm, kmt, vm]
    in_specs = [pl.BlockSpec((1, MLA_HPS, tq, LANE), lambda b, h, i: (b, h, i, 0)),
                pl.BlockSpec((1, MLA_HPS, nkt, LANE, tk), lambda b, h, i: (b, h, 0, 0, 0)),
                pl.BlockSpec((1, MLA_HPS, tkv, LANE), lambda b, h, i: (b, h, 0, 0))]
    if extra:
        cx = vmx.shape[2]
        ins += [kmtx, vmx]
        in_specs += [pl.BlockSpec((1, MLA_HPS, 1, LANE, cx), lambda b, h, i: (b, h, 0, 0, 0)),
                     pl.BlockSpec((1, MLA_HPS, cx, LANE), lambda b, h, i: (b, h, 0, 0))]
    return pl.pallas_call(
        functools.partial(_mla_kernel, nkt=nkt, extra=extra),
        grid=(bsz, nh // MLA_HPS, t // tq),
        in_specs=in_specs,
        out_specs=pl.BlockSpec((1, tq, MLA_HPS * MLA_V), lambda b, h, i: (b, i, h)),
        out_shape=jax.ShapeDtypeStruct((bsz, t, nh * MLA_V), F32),
        compiler_params=_params("parallel", "parallel", "parallel"),
        name="mla_attn",
    )(*ins)


def _merge_kernel(x_ref, g_ref, of_ref, ob_ref, bonus_ref, yb_ref, yc_ref, yd_ref, z_ref, gate_ref,
                  gnw_ref, gnb_ref, wb_ref, wo_ref, fnw_ref, out_ref, *, final):
    o = of_ref[0] + ob_ref[0]
    oc = o - _head_sum(o) * (1.0 / RW_HEAD)
    var = _head_sum(oc * oc) * (1.0 / RW_HEAD)
    ya = (oc * lax.rsqrt(var + RW_GN_EPS)) * gnw_ref[...] + gnb_ref[...] + bonus_ref[0]
    ys = (ya, yb_ref[0], yc_ref[0], yd_ref[0])
    m = None
    for i, y in enumerate(ys):
        zh = z_ref[0, :, BRANCH_W * i:BRANCH_W * (i + 1)].astype(F32)
        gh = gate_ref[0, :, D_MODEL * i:D_MODEL * (i + 1)].astype(F32)
        yz = (y * (zh * jnp.tanh(zh) + zh)).astype(BF16)
        term = (jnp.tanh(gh) + 1.0) * jnp.dot(yz, wb_ref[i], preferred_element_type=F32)
        m = term if m is None else m + term
    xn = x_ref[0] + g_ref[0] * jnp.dot(m.astype(BF16), wo_ref[...], preferred_element_type=F32)
    if final:
        xn = _rms(xn, fnw_ref[...])
    out_ref[0] = xn


def _merge(x, g, o, bonus, yb, yc, yd, p, gn_w, gn_b, w_branch, w_out, final_norm_w, final):
    bsz, t, d = x.shape
    tm = min(256, t)
    zblk, gblk = _cblk("z"), _cblk("gate")
    row = lambda w: pl.BlockSpec((1, tm, w), lambda b, i: (b, i, 0))
    return pl.pallas_call(
        functools.partial(_merge_kernel, final=final),
        grid=(bsz, t // tm),
        in_specs=[row(d),
                  pl.BlockSpec((1, 1, d), lambda b, i: (b, 0, 0)),
                  row(RW_C), row(RW_C), row(RW_C),
                  row(BRANCH_W), row(BRANCH_W), row(BRANCH_W),
                  pl.BlockSpec((1, tm, COL["z"][1]), lambda b, i: (b, i, zblk)),
                  pl.BlockSpec((1, tm, COL["gate"][1]), lambda b, i: (b, i, gblk)),
                  pl.BlockSpec((1, RW_C), lambda b, i: (0, 0)),
                  pl.BlockSpec((1, RW_C), lambda b, i: (0, 0)),
                  pl.BlockSpec((N_BRANCH, BRANCH_W, d), lambda b, i: (0, 0, 0)),
                  pl.BlockSpec((d, d), lambda b, i: (0, 0)),
                  pl.BlockSpec((1, d), lambda b, i: (0, 0))],
        out_specs=row(d),
        out_shape=jax.ShapeDtypeStruct((bsz, t, d), F32),
        compiler_params=_params("parallel", "parallel"),
        name="merge",
    )(x, g, o[0], o[1], bonus, yb, yc, yd, p, p, gn_w.reshape(1, RW_C), gn_b.reshape(1, RW_C),
      w_branch, w_out, final_norm_w.reshape(1, d))


MLA_TK = 1024


def _permute_q_up(q_up):
    r = q_up.shape[0]
    w = q_up.reshape(r, MLA_HEADS, MLA_NOPE + MLA_ROPE)
    nope = w[:, :, :MLA_NOPE].reshape(r, MLA_HEADS * MLA_NOPE)
    rope = jnp.pad(w[:, :, MLA_NOPE:], ((0, 0), (0, 0), (0, MLA_NOPE - MLA_ROPE))).reshape(r, MLA_HEADS * MLA_NOPE)
    return jnp.concatenate([nope, rope], axis=-1)


def _permute_kv_up(kv_up):
    r = kv_up.shape[0]
    w = kv_up.reshape(r, MLA_HEADS, MLA_NOPE + MLA_V)
    return jnp.concatenate([w[:, :, :MLA_NOPE].reshape(r, -1), w[:, :, MLA_NOPE:].reshape(r, -1)], axis=-1)


def kernel(x, c, ctx, c_ctx, norm_w, ada_w, ada_b, w_in, rwkv_mu, rwkv_w0, rwkv_w_up, rwkv_a0, rwkv_a_up, rwkv_k_k, rwkv_k_a, rwkv_r_k, rwkv_gn_w, rwkv_gn_b, conv_w, conv_b, conv_ln_w, conv_ln_b, attn_sink, mla_q_norm, mla_q_up, mla_kv_norm, mla_kv_up, w_branch, w_out, final_norm_w):
    bsz, t, d = x.shape
    n_ctx = ctx.shape[1]
    depth = norm_w.shape[0]
    assert t % MLA_TK == 0 and t % BLOCK == 0 and n_ctx % RW_CHUNK == 0 and n_ctx <= MLA_TK

    cvec = jnp.concatenate([c, c_ctx[None, :], jnp.zeros((SUBLANE - bsz - 1, d), F32)], axis=0)
    mod = _modulation(cvec, ada_w, ada_b)
    tables = _rope_tables(t)
    s_zero = jnp.zeros((bsz, 2, RW_HEAD, RW_C), F32)

    for l in range(depth):
        last = l == depth - 1
        mx = mod[l, :bsz][:, None, :]
        mc = jnp.broadcast_to(mod[l, bsz][None, None, :], (bsz, 1, 3 * d))
        sh_x, sc_x, g_x = mx[..., :d], mx[..., d:2 * d], mx[..., 2 * d:]
        sh_c, sc_c, g_c = mc[..., :d], mc[..., d:2 * d], mc[..., 2 * d:]
        w_p = _pack_cols(w_in[l]).astype(BF16)
        px32, px = _in_proj(x, norm_w[l], sc_x, sh_x, w_p)
        pc32, pc = _in_proj(ctx, norm_w[l], sc_c, sh_c, w_p)

        rw_args = (rwkv_mu[l], rwkv_w0[l], rwkv_w_up[l], rwkv_a0[l], rwkv_a_up[l], rwkv_k_k[l], rwkv_k_a[l],
                   rwkv_r_k[l].reshape(RW_C))
        o_c, bonus_c, s_ctx = _rwkv(pc32, pc, s_zero, *rw_args)
        o_x, bonus_x, _ = _rwkv(px32, px, s_ctx, *rw_args)

        conv_args = (conv_w[l], conv_b[l], conv_ln_w[l], conv_ln_b[l])
        yb_x = _conv(px, *conv_args)

        q_up_p = _permute_q_up(mla_q_up[l]).astype(BF16)
        kv_up_p = _permute_kv_up(mla_kv_up[l]).astype(BF16)
        prep_w = (mla_q_norm[l], q_up_p, mla_kv_norm[l], kv_up_p)
        qm_x, kmt_x, vm_x, wq_x, wk_x = _attn_prep(px32, px, tables, *prep_w, tk=MLA_TK)
        qm_c, kmt_c, vm_c, wq_c, wk_c = _attn_prep(pc32, pc, None, *prep_w, tk=n_ctx)

        yc_x = _window_attn(wq_x, wk_x, px, wk_c, pc, attn_sink[l], band=True)
        yd_x = _mla_attn(qm_x, kmt_x, vm_x, kmt_c, vm_c)

        merge_w = (rwkv_gn_w[l], rwkv_gn_b[l], w_branch[l].astype(BF16), (0.5 * w_out[l]).astype(BF16),
                   final_norm_w)
        x = _merge(x, g_x, o_x, bonus_x, yb_x, yc_x, yd_x, px, *merge_w, final=last)

        if not last:
            yb_c = _conv(pc, *conv_args)
            yc_c = _window_attn(wq_c, wk_c, pc, wk_c, pc, attn_sink[l], band=False)
            yd_c = _mla_attn(qm_c, kmt_c, vm_c)
            ctx = _merge(ctx, g_c, o_c, bonus_c, yb_c, yc_c, yd_c, pc, *merge_w, final=False)

    return x
```

```python
import functools

import jax
import jax.numpy as jnp
from jax import lax
from jax.experimental import pallas as pl
from jax.experimental.pallas import tpu as pltpu

F32 = jnp.float32
BF16 = jnp.bfloat16

D_MODEL = 1024
GRID_W = 64
N_BRANCH = 4
BRANCH_W = 512
RW_HEADS = 8
RW_HEAD = 64
RW_C = RW_HEADS * RW_HEAD
RW_LORA = 64
RW_GN_EPS = 64e-5
CONV_CH = 512
CONV_K = 31
CONV_LN_EPS = 1e-5
WA_HEADS = 8
WA_KV_HEADS = 2
WA_HEAD = 64
WA_GROUP = WA_HEADS // WA_KV_HEADS
WINDOW = 128
BLOCK = 128
WA_SCALE = WA_HEAD ** -0.5
MLA_HEADS = 8
MLA_Q_RANK = 256
MLA_KV_RANK = 128
MLA_NOPE = 64
MLA_ROPE = 32
MLA_V = 64
MLA_SCALE = (MLA_NOPE + MLA_ROPE) ** -0.5
MLA_QSCALE = MLA_SCALE * 1.4426950408889634
ROPE_BASE = 10000.0
NORM_EPS = 1e-6

LANE = 128
SUBLANE = 8
VMEM_LIMIT = 48 * 1024 * 1024

_SRC = {}
_off = 0
for _name, _size in (("r", RW_C), ("k", RW_C), ("v", RW_C), ("wd", RW_LORA), ("ad", RW_LORA),
                     ("cv", 2 * CONV_CH), ("wq", WA_HEADS * WA_HEAD), ("wk", WA_KV_HEADS * WA_HEAD),
                     ("wv", WA_KV_HEADS * WA_HEAD), ("dq", MLA_Q_RANK), ("dkv", MLA_KV_RANK),
                     ("kr", MLA_ROPE), ("z", N_BRANCH * BRANCH_W), ("gate", N_BRANCH * D_MODEL)):
    _SRC[_name] = (_off, _size)
    _off += _size
N_IN = _off

_PACK32 = (("r", 512), ("k", 512), ("v", 512), ("wq", 512))
_PACK16 = (("gate", 4096), ("z", 2048), ("cv", 1024), ("dq", 256), ("wdad", 128),
           ("wk", 128), ("wv", 128), ("dkv", 128), ("kr", 128), ("pad", 128))
_PACK = _PACK32 + _PACK16
COL = {}
for _pack in (_PACK32, _PACK16):
    _off = 0
    for _name, _size in _pack:
        assert _off % _size == 0
        COL[_name] = (_off, _size)
        _off += _size
N32 = sum(s for _, s in _PACK32)
N16 = sum(s for _, s in _PACK16)
N_PACK = N32 + N16


def _cblk(name):
    off, size = COL[name]
    return off // size


def _pack_cols(w):
    def src(name):
        o, s = _SRC[name]
        return w[..., o:o + s]
    pieces = []
    for name, size in _PACK:
        if name in ("z", "gate"):
            pieces.append(0.5 * src(name))
        elif name == "wdad":
            pieces += [src("wd"), src("ad")]
        elif name == "kr":
            pieces += [src("kr"), jnp.zeros(w.shape[:-1] + (size - MLA_ROPE,), w.dtype)]
        elif name == "pad":
            pieces.append(jnp.zeros(w.shape[:-1] + (size,), w.dtype))
        else:
            pieces.append(src(name))
    return jnp.concatenate(pieces, axis=-1)


def _params(*sem):
    return pltpu.CompilerParams(dimension_semantics=sem, vmem_limit_bytes=VMEM_LIMIT)


def _sigmoid(x):
    return 0.5 * jnp.tanh(0.5 * x) + 0.5


def _split2(x):
    hi = x.astype(BF16)
    lo = (x - hi.astype(F32)).astype(BF16)
    return hi, lo


def _split3(x):
    hi = x.astype(BF16)
    r1 = x - hi.astype(F32)
    mid = r1.astype(BF16)
    lo = (r1 - mid.astype(F32)).astype(BF16)
    return hi, mid, lo


def _mod_kernel(c_ref, w_ref, b_ref, o_ref):
    cv = c_ref[...]
    sc = cv * jax.nn.sigmoid(cv)
    o_ref[0] = jnp.dot(sc, w_ref[0], preferred_element_type=F32,
                       precision=lax.Precision.HIGHEST) + b_ref[0]


def _modulation(cvec, ada_w, ada_b):
    depth, d, d3 = ada_w.shape
    rows = cvec.shape[0]
    tn = 512
    return pl.pallas_call(
        _mod_kernel,
        grid=(depth, d3 // tn),
        in_specs=[pl.BlockSpec((rows, d), lambda l, j: (0, 0)),
                  pl.BlockSpec((1, d, tn), lambda l, j: (l, 0, j)),
                  pl.BlockSpec((1, 1, tn), lambda l, j: (l, 0, j))],
        out_specs=pl.BlockSpec((1, rows, tn), lambda l, j: (l, 0, j)),
        out_shape=jax.ShapeDtypeStruct((depth, rows, d3), F32),
        compiler_params=_params("parallel", "parallel"),
        name="modulation",
    )(cvec, ada_w, ada_b.reshape(depth, 1, d3))


INPROJ_TN = 1024
assert N32 % INPROJ_TN == 0 and N16 % INPROJ_TN == 0
N32_TILES = N32 // INPROJ_TN


def _inproj_kernel(x_ref, nw_ref, sc_ref, sh_ref, w_ref, o32_ref, o16_ref, h_ref):
    j = pl.program_id(2)

    @pl.when(j == 0)
    def _():
        xf = x_ref[0]
        ms = jnp.mean(xf * xf, axis=-1, keepdims=True)
        hn = (xf * lax.rsqrt(ms + NORM_EPS)) * nw_ref[...]
        h_ref[...] = (hn * (1.0 + sc_ref[0]) + sh_ref[0]).astype(BF16)

    @pl.when(j < N32_TILES)
    def _():
        o32_ref[0] = jnp.dot(h_ref[...], w_ref[...], preferred_element_type=F32)

    @pl.when(j >= N32_TILES)
    def _():
        o16_ref[0] = jnp.dot(h_ref[...], w_ref[...], preferred_element_type=F32).astype(BF16)


def _in_proj(x, norm_w, scale, shift, w_packed):
    bsz, t, d = x.shape
    tm = min(1024, t)
    tn = INPROJ_TN
    return pl.pallas_call(
        _inproj_kernel,
        grid=(bsz, t // tm, N_PACK // tn),
        in_specs=[pl.BlockSpec((1, tm, d), lambda b, i, j: (b, i, 0)),
                  pl.BlockSpec((1, d), lambda b, i, j: (0, 0)),
                  pl.BlockSpec((1, 1, d), lambda b, i, j: (b, 0, 0)),
                  pl.BlockSpec((1, 1, d), lambda b, i, j: (b, 0, 0)),
                  pl.BlockSpec((d, tn), lambda b, i, j: (0, j))],
        out_specs=[pl.BlockSpec((1, tm, tn), lambda b, i, j: (b, i, jnp.minimum(j, N32_TILES - 1))),
                   pl.BlockSpec((1, tm, tn), lambda b, i, j: (b, i, jnp.maximum(j - N32_TILES, 0)))],
        out_shape=[jax.ShapeDtypeStruct((bsz, t, N32), F32),
                   jax.ShapeDtypeStruct((bsz, t, N16), BF16)],
        scratch_shapes=[pltpu.VMEM((tm, d), BF16)],
        compiler_params=_params("parallel", "parallel", "arbitrary"),
        name="in_proj",
    )(x, norm_w.reshape(1, d), scale, shift, w_packed)


RW_CHUNK = 64
P_DATA = 1
P_STATE = 3


PAIR = 2 * RW_HEAD
N_PAIR = RW_C // PAIR


def _dot1(a, b, passes=1):
    def e(x, y):
        return jnp.dot(x, y, preferred_element_type=F32)
    if passes == 1:
        return e(a.astype(BF16), b.astype(BF16))
    ah, al = _split2(a)
    bh, bl = _split2(b)
    return e(ah, bh) + (e(ah, bl) + e(al, bh))


def _lo_lanes(shape):
    return lax.broadcasted_iota(jnp.int32, shape, len(shape) - 1) % PAIR < RW_HEAD


def _bd2(y):
    lo = _lo_lanes(y.shape)
    zero = jnp.zeros_like(y)
    return jnp.concatenate([jnp.where(lo, y, zero), jnp.where(lo, zero, y)], axis=0)


def _head_sum(x):
    outs = []
    for p in range(x.shape[-1] // PAIR):
        s = x[:, PAIR * p:PAIR * (p + 1)]
        lo = _lo_lanes(s.shape)
        s0 = jnp.sum(jnp.where(lo, s, 0.0), axis=-1, keepdims=True)
        s1 = jnp.sum(jnp.where(lo, 0.0, s), axis=-1, keepdims=True)
        outs.append(jnp.where(lo, s0, s1))
    return outs[0] if len(outs) == 1 else jnp.concatenate(outs, axis=-1)


def _rwkv_p1_kernel(r_ref, k_ref, v_ref, wa_ref, rp_ref, kp_ref, vp_ref, wap_ref,
                    rn_ref, kn_ref, vn_ref, wan_ref,
                    mur_ref, muk_ref, muv_ref, muwa_ref, w0_ref, wup_ref, a0_ref, aup_ref,
                    kkw_ref, kaw_ref, rkw_ref,
                    qt_ref, o0_ref, g_ref, h_ref, bonus_ref, *, nc):
    L = RW_CHUNK
    R = r_ref.shape[1]
    c = pl.program_id(1)

    def shift_mix(cur_ref, p_ref, n_ref, mu_ref):
        cur = cur_ref[0].astype(F32)
        hr = p_ref.shape[1]
        prow = jnp.where(c > 0, p_ref[0, hr - 1:hr, :].astype(F32), 0.0)
        nrow = jnp.where(c < nc - 1, n_ref[0, 0:1, :].astype(F32), 0.0)
        ridx = lax.broadcasted_iota(jnp.int32, cur.shape, 0)
        fprev = jnp.where(ridx == 0, prow, pltpu.roll(cur, 1, 0))
        fnext = jnp.where(ridx == R - 1, nrow, pltpu.roll(cur, R - 1, 0))
        return cur + mu_ref[...] * (0.5 * (fprev + fnext) - cur)

    r = shift_mix(r_ref, rp_ref, rn_ref, mur_ref)
    k = shift_mix(k_ref, kp_ref, kn_ref, muk_ref)
    v = shift_mix(v_ref, vp_ref, vn_ref, muv_ref)
    wa = shift_mix(wa_ref, wap_ref, wan_ref, muwa_ref)
    wd_t = jnp.tanh(wa[:, :RW_LORA]).astype(BF16)
    ad = wa[:, RW_LORA:].astype(BF16)

    kk = k * kkw_ref[...]
    kk = kk / jnp.maximum(jnp.sqrt(_head_sum(kk * kk)), 1e-12)

    row = lax.broadcasted_iota(jnp.int32, (L, PAIR), 0)
    col = lax.broadcasted_iota(jnp.int32, (L, PAIR), 1) % RW_HEAD
    eye_l = jnp.where(row == col, 1.0, 0.0)
    rown = lax.broadcasted_iota(jnp.int32, (RW_HEAD, PAIR), 0)
    coln = lax.broadcasted_iota(jnp.int32, (RW_HEAD, PAIR), 1) % RW_HEAD
    eye_n = rown == coln
    trow = lax.broadcasted_iota(jnp.int32, (R, R), 0)
    tcol = lax.broadcasted_iota(jnp.int32, (R, R), 1)
    same_chunk = (trow // L) == (tcol // L)
    steps = L.bit_length() - 1
    kdir_sum = None
    dense = []
    for d in range(2):
        tri = (same_chunk & ((tcol <= trow) if d == 0 else (tcol >= trow))).astype(BF16)
        last = L - 1 if d == 0 else 0

        w_pre = w0_ref[d:d + 1, :] + jnp.dot(wd_t, wup_ref[d].astype(BF16), preferred_element_type=F32)
        w_log = -(jnp.maximum(-w_pre, 0.0) + jnp.log1p(jnp.exp(-jnp.abs(w_pre)))) - 0.5
        logw = -jnp.exp(w_log)
        lh, lm, ll = _split3(logw)
        cum = (jnp.dot(tri, lh, preferred_element_type=F32)
               + (jnp.dot(tri, lm, preferred_element_type=F32) + jnp.dot(tri, ll, preferred_element_type=F32)))
        a = jax.nn.sigmoid(a0_ref[d:d + 1, :] + jnp.dot(ad, aup_ref[d].astype(BF16), preferred_element_type=F32))

        cum_ls = [cum[j * L + last:j * L + last + 1, :] for j in range(R // L)]
        cum_l = jnp.concatenate([jnp.broadcast_to(x, (L, RW_C)) for x in cum_ls], axis=0)
        e_neg = jnp.exp(-cum)
        e_l = jnp.exp(cum_l - cum)
        w_l = [jnp.exp(x) for x in cum_ls]
        kdir = k * (1.0 + (a - 1.0) * kaw_ref[...])
        kdir_sum = kdir if kdir_sum is None else kdir_sum + kdir
        b = kk * a
        al_c = kk * jnp.exp(cum - logw)
        rho_c = r * jnp.exp(cum)
        be_c = b * e_neg
        ka_c = kdir * e_neg
        bel_c = b * e_l
        kal_c = kdir * e_l
        dense.append((al_c, rho_c, be_c, ka_c, bel_c, kal_c, w_l))

    bonus_ref[0] = _head_sum(r * kdir_sum * rkw_ref[...]) * v

    chains = [(j, d, p) for j in range(R // L) for d in range(2) for p in range(N_PAIR)]

    def sub(x, ch):
        return x[ch[0] * L:(ch[0] + 1) * L, PAIR * ch[2]:PAIR * (ch[2] + 1)]

    def bd(y):
        return _bd2(y.astype(BF16))

    def each(fn, *lists):
        return [fn(ch, *(lst[i] for lst in lists)) for i, ch in enumerate(chains)]

    def products(ch):
        al, rho, be, ka = (sub(x, ch) for x in dense[ch[1]][:4])
        lo = _lo_lanes(be.shape)
        zero = jnp.zeros_like(be)
        rhs_nt = jnp.concatenate([jnp.where(lo, be, zero), jnp.where(lo, zero, be),
                                  jnp.where(lo, ka, zero), jnp.where(lo, zero, ka)], axis=0)
        lhs = jnp.concatenate([al, rho], axis=0)
        return jnp.einsum('ld,md->lm', lhs.astype(BF16), rhs_nt.astype(BF16),
                          preferred_element_type=F32)

    ms = each(products)

    def block_t(x):
        xt = x.T
        return jnp.concatenate([xt[:RW_HEAD], xt[RW_HEAD:]], axis=1)

    belts = each(lambda ch: block_t(sub(dense[ch[1]][4], ch)))
    kalts = each(lambda ch: block_t(sub(dense[ch[1]][5], ch)))

    def masks(ch, m):
        d = ch[1]
        incl = (col <= row) if d == 0 else (col >= row)
        strict = (col < row) if d == 0 else (col > row)
        return (-jnp.where(strict, m[:L, :PAIR], 0.0), jnp.where(strict, m[:L, PAIR:], 0.0),
                jnp.where(incl, m[L:, :PAIR], 0.0), jnp.where(incl, m[L:, PAIR:], 0.0))

    mk = each(masks, ms)
    negs, akss, rbis, rkis = ([x[i] for x in mk] for i in range(4))

    accs = [eye_l + neg for neg in negs]
    pws = each(lambda ch, neg: _dot1(neg, bd(neg), P_DATA), negs)
    for _ in range(1, steps - 1):
        sts = each(lambda ch, acc, pw: _dot1(jnp.concatenate([acc, pw], axis=0), bd(pw), P_DATA), accs, pws)
        accs = [acc + st[:L] for acc, st in zip(accs, sts)]
        pws = [st[L:] for st in sts]
    tmats = each(lambda ch, acc, pw: acc + _dot1(acc, bd(pw), P_DATA), accs, pws)

    vss = each(lambda ch, aks, rki, kalt: _dot1(jnp.concatenate([aks, rki, kalt], axis=0), bd(sub(v, ch)), P_DATA),
               akss, rkis, kalts)
    tus = each(lambda ch, tmat, vs: _dot1(tmat, jnp.concatenate(
        [bd(sub(dense[ch[1]][0], ch)), bd(vs[:L])], axis=1), P_DATA), tmats, vss)
    uas = [tu[:, :PAIR] for tu in tus]
    d0s = [-tu[:, PAIR:] for tu in tus]
    rus = each(lambda ch, rbi, belt, ua, d0: _dot1(jnp.concatenate([rbi, belt], axis=0),
                                                   jnp.concatenate([bd(ua), bd(d0)], axis=1), P_DATA),
               rbis, belts, uas, d0s)

    for i, ch in enumerate(chains):
        j, d, p = ch
        rows = slice(j * L, (j + 1) * L)
        sl = slice(PAIR * p, PAIR * (p + 1))
        qt_ref[0, d, rows, sl] = sub(dense[d][1], ch) - rus[i][:L, :PAIR]
        o0_ref[0, d, rows, sl] = rus[i][:L, PAIR:] + vss[i][L:2 * L]
        g_ref[0, d, j, :, sl] = jnp.where(eye_n, dense[d][6][j][:, sl], 0.0) - rus[i][L:, :PAIR]
        h_ref[0, d, j, :, sl] = rus[i][L:, PAIR:] + vss[i][2 * L:]


def _rwkv_p2_kernel(qtf_ref, o0f_ref, gf_ref, hf_ref, qtb_ref, o0b_ref, gb_ref, hb_ref, s0_ref,
                    of_ref, ob_ref, sfin_ref, s_ref, *, nsteps, cps):
    L = RW_CHUNK
    c = pl.program_id(1)
    dirs = ((qtf_ref, o0f_ref, gf_ref, hf_ref, of_ref), (qtb_ref, o0b_ref, gb_ref, hb_ref, ob_ref))

    @pl.when(c == 0)
    def _():
        s_ref[...] = s0_ref[0]

    def body(i, states):
        cis = (i, cps - 1 - i)
        rows = [pl.ds(pl.multiple_of(ci * L, L), L) for ci in cis]
        qts = [dirs[d][0][0, 0, rows[d], :] for d in range(2)]
        gs = [dirs[d][2][0, 0, cis[d]] for d in range(2)]
        prods = [[_dot1(jnp.concatenate([qts[d][:, PAIR * p:PAIR * (p + 1)], gs[d][:, PAIR * p:PAIR * (p + 1)]],
                                        axis=0), _bd2(states[d][:, PAIR * p:PAIR * (p + 1)]), P_STATE)
                  for p in range(N_PAIR)] for d in range(2)]
        new = []
        for d in range(2):
            st = jnp.concatenate(prods[d], axis=1)
            dirs[d][4][0, rows[d], :] = st[:L] + dirs[d][1][0, 0, rows[d], :]
            new.append(st[L:] + dirs[d][3][0, 0, cis[d]])
        return tuple(new)

    s_new = lax.fori_loop(0, cps, body, (s_ref[0], s_ref[1]), unroll=True)
    s_ref[0] = s_new[0]
    s_ref[1] = s_new[1]

    @pl.when(c == nsteps - 1)
    def _():
        sfin_ref[0, 0] = s_new[0]
        sfin_ref[0, 1] = s_new[1]


RW_PREP_CHUNKS = 4
RW_SCAN_CHUNKS = 8
HALO16 = 2 * SUBLANE


def _rwkv(p32, p16, s0, mu, w0, w_up, a0, a_up, k_k, k_a, r_k):
    bsz, t, _ = p32.shape
    L = RW_CHUNK
    nc = t // L

    def halo_rows(name):
        return HALO16 if name == "wdad" else SUBLANE

    cpb = min(RW_PREP_CHUNKS, nc)
    R = cpb * L
    nblk = t // R

    def cur(name):
        blk = _cblk(name)
        w = COL[name][1]
        return pl.BlockSpec((1, R, w), lambda b, c: (b, c, blk))

    def prev(name):
        blk = _cblk(name)
        w = COL[name][1]
        hr = halo_rows(name)
        return pl.BlockSpec((1, hr, w), lambda b, c: (b, jnp.maximum(c * (R // hr) - 1, 0), blk))

    def nxt(name):
        blk = _cblk(name)
        w = COL[name][1]
        hr = halo_rows(name)
        return pl.BlockSpec((1, hr, w), lambda b, c: (b, jnp.minimum((c + 1) * (R // hr), t // hr - 1), blk))

    def full(a):
        nd = a.ndim
        return pl.BlockSpec(a.shape, lambda b, c: (0,) * nd)

    names = ("r", "k", "v", "wdad")
    srcs = [p32, p32, p32, p16]
    mu_p = [mu[o:o + s].reshape(1, s) for o, s in
            ((0, RW_C), (RW_C, RW_C), (2 * RW_C, RW_C), (3 * RW_C, 2 * RW_LORA))]
    small = mu_p + [w0, w_up, a0, a_up, k_k.reshape(1, RW_C), k_a.reshape(1, RW_C), r_k.reshape(1, RW_C)]
    qt, o0, g, h, bonus = pl.pallas_call(
        functools.partial(_rwkv_p1_kernel, nc=nblk),
        grid=(bsz, nblk),
        in_specs=[cur(n) for n in names] + [prev(n) for n in names] + [nxt(n) for n in names]
                 + [full(a) for a in small],
        out_specs=[pl.BlockSpec((1, 2, R, RW_C), lambda b, c: (b, 0, c, 0)),
                   pl.BlockSpec((1, 2, R, RW_C), lambda b, c: (b, 0, c, 0)),
                   pl.BlockSpec((1, 2, cpb, RW_HEAD, RW_C), lambda b, c: (b, 0, c, 0, 0)),
                   pl.BlockSpec((1, 2, cpb, RW_HEAD, RW_C), lambda b, c: (b, 0, c, 0, 0)),
                   pl.BlockSpec((1, R, RW_C), lambda b, c: (b, c, 0))],
        out_shape=[jax.ShapeDtypeStruct((bsz, 2, t, RW_C), F32),
                   jax.ShapeDtypeStruct((bsz, 2, t, RW_C), F32),
                   jax.ShapeDtypeStruct((bsz, 2, nc, RW_HEAD, RW_C), F32),
                   jax.ShapeDtypeStruct((bsz, 2, nc, RW_HEAD, RW_C), F32),
                   jax.ShapeDtypeStruct((bsz, t, RW_C), F32)],
        compiler_params=_params("parallel", "parallel"),
        name="rwkv_chunk_prep",
    )(*(srcs * 3), *small)

    cps = min(RW_SCAN_CHUNKS, nc)
    nsteps = nc // cps

    tblk = (1, 1, cps * L, RW_C)
    gblk = (1, 1, cps, RW_HEAD, RW_C)
    sblk = (1, 2, RW_HEAD, RW_C)
    fwd_t = pl.BlockSpec(tblk, lambda b, c: (b, 0, c, 0))
    bwd_t = pl.BlockSpec(tblk, lambda b, c: (b, 1, nsteps - 1 - c, 0))
    fwd_g = pl.BlockSpec(gblk, lambda b, c: (b, 0, c, 0, 0))
    bwd_g = pl.BlockSpec(gblk, lambda b, c: (b, 1, nsteps - 1 - c, 0, 0))
    o_f, o_b, s_fin = pl.pallas_call(
        functools.partial(_rwkv_p2_kernel, nsteps=nsteps, cps=cps),
        grid=(bsz, nsteps),
        in_specs=[fwd_t, fwd_t, fwd_g, fwd_g, bwd_t, bwd_t, bwd_g, bwd_g,
                  pl.BlockSpec(sblk, lambda b, c: (b, 0, 0, 0))],
        out_specs=[pl.BlockSpec((1, cps * L, RW_C), lambda b, c: (b, c, 0)),
                   pl.BlockSpec((1, cps * L, RW_C), lambda b, c: (b, nsteps - 1 - c, 0)),
                   pl.BlockSpec(sblk, lambda b, c: (b, 0, 0, 0))],
        out_shape=[jax.ShapeDtypeStruct((bsz, t, RW_C), F32),
                   jax.ShapeDtypeStruct((bsz, t, RW_C), F32),
                   jax.ShapeDtypeStruct((bsz, 2, RW_HEAD, RW_C), F32)],
        scratch_shapes=[pltpu.VMEM((2, RW_HEAD, RW_C), F32)],
        compiler_params=_params("parallel", "arbitrary"),
        name="rwkv_state_scan",
    )(qt, o0, g, h, qt, o0, g, h, s0)
    return (o_f, o_b), bonus, s_fin


CONV_HALO = 16


def _conv_kernel(cur_ref, prev_ref, next_ref, w_ref, b_ref, lnw_ref, lnb_ref, o_ref, u_ref, *, nt, tt):
    i = pl.program_id(1)

    def glu(x):
        x = x.astype(F32)
        return x[:, :CONV_CH] * _sigmoid(x[:, CONV_CH:])

    u_ref[0, 0:CONV_HALO, :] = jnp.where(i > 0, glu(prev_ref[0]), 0.0)
    u_ref[0, CONV_HALO:CONV_HALO + tt, :] = glu(cur_ref[0])
    u_ref[0, CONV_HALO + tt:, :] = jnp.where(i < nt - 1, glu(next_ref[0]), 0.0)
    span = tt + 2 * CONV_HALO - SUBLANE
    for s in range(1, SUBLANE):
        u_ref[s, 0:span, :] = u_ref[0, s:s + span, :]
    acc = jnp.zeros((tt, CONV_CH), F32) + b_ref[...]
    base = CONV_HALO - CONV_K // 2
    for j in range(CONV_K):
        s = (base + j) % SUBLANE
        start = base + j - s
        acc = acc + u_ref[s, start:start + tt, :] * w_ref[j:j + 1, :]
    mu = jnp.mean(acc, axis=-1, keepdims=True)
    xc = acc - mu
    var = jnp.mean(xc * xc, axis=-1, keepdims=True)
    un = (xc * lax.rsqrt(var + CONV_LN_EPS)) * lnw_ref[...] + lnb_ref[...]
    o_ref[0] = un * _sigmoid(un)


def _conv(p, conv_w, conv_b, ln_w, ln_b):
    bsz, t, _ = p.shape
    tt = min(512, t)
    nt = t // tt
    hb = tt // CONV_HALO
    nh = t // CONV_HALO
    blk = _cblk("cv")
    w = COL["cv"][1]
    vec = lambda a: a.reshape(1, CONV_CH)
    return pl.pallas_call(
        functools.partial(_conv_kernel, nt=nt, tt=tt),
        grid=(bsz, nt),
        in_specs=[pl.BlockSpec((1, tt, w), lambda b, i: (b, i, blk)),
                  pl.BlockSpec((1, CONV_HALO, w), lambda b, i: (b, jnp.maximum(i * hb - 1, 0), blk)),
                  pl.BlockSpec((1, CONV_HALO, w), lambda b, i: (b, jnp.minimum((i + 1) * hb, nh - 1), blk)),
                  pl.BlockSpec((CONV_K, CONV_CH), lambda b, i: (0, 0)),
                  pl.BlockSpec((1, CONV_CH), lambda b, i: (0, 0)),
                  pl.BlockSpec((1, CONV_CH), lambda b, i: (0, 0)),
                  pl.BlockSpec((1, CONV_CH), lambda b, i: (0, 0))],
        out_specs=pl.BlockSpec((1, tt, CONV_CH), lambda b, i: (b, i, 0)),
        out_shape=jax.ShapeDtypeStruct((bsz, t, CONV_CH), F32),
        scratch_shapes=[pltpu.VMEM((SUBLANE, tt + 2 * CONV_HALO, CONV_CH), F32)],
        compiler_params=_params("parallel", "parallel"),
        name="conv_module",
    )(p, p, p, conv_w, vec(conv_b), vec(ln_w), vec(ln_b))


def _swap_halves(x, hs):
    slabs = []
    for j in range(x.shape[-1] // LANE):
        s = x[:, LANE * j:LANE * (j + 1)]
        lane = lax.broadcasted_iota(jnp.int32, s.shape, 1)
        slabs.append(jnp.where((lane % (2 * hs)) < hs, pltpu.roll(s, LANE - hs, 1), pltpu.roll(s, hs, 1)))
    return slabs[0] if len(slabs) == 1 else jnp.concatenate(slabs, axis=-1)


def _rope(x, cos, sin, hs):
    reps = x.shape[-1] // LANE
    if reps > 1:
        cos = jnp.concatenate([cos] * reps, axis=-1)
        sin = jnp.concatenate([sin] * reps, axis=-1)
    return x * cos + _swap_halves(x, hs) * sin


def _rms(x, w):
    ms = jnp.mean(x * x, axis=-1, keepdims=True)
    return (x * lax.rsqrt(ms + NORM_EPS)) * w


def _prep_kernel(*refs, rope):
    if rope:
        (dq_ref, dkv_ref, kr_ref, wq_ref, wk_ref, c64_ref, s64_ref, c32_ref, s32_ref,
         qnw_ref, qup_ref, kvnw_ref, kvup_ref, qm_ref, kmt_ref, vm_ref, wqo_ref, wko_ref) = refs
    else:
        (dq_ref, dkv_ref, kr_ref, wq_ref, wk_ref,
         qnw_ref, qup_ref, kvnw_ref, kvup_ref, qm_ref, kmt_ref, vm_ref, wqo_ref, wko_ref) = refs
    hw = MLA_HEADS * MLA_NOPE
    q = jnp.dot(_rms(dq_ref[0].astype(F32), qnw_ref[...]).astype(BF16), qup_ref[...],
                preferred_element_type=F32)
    kv = jnp.dot(_rms(dkv_ref[0].astype(F32), kvnw_ref[...]).astype(BF16), kvup_ref[...],
                 preferred_element_type=F32)
    qn, qr = q[:, :hw], q[:, hw:]
    kn, vv = kv[:, :hw], kv[:, hw:]
    kr = kr_ref[0].astype(F32)
    wq = wq_ref[0].astype(F32)
    wk = wk_ref[0].astype(F32)
    if rope:
        qr = _rope(qr, c32_ref[...], s32_ref[...], MLA_ROPE // 4)
        kr = _rope(kr, c32_ref[...], s32_ref[...], MLA_ROPE // 4)
        wq = _rope(wq, c64_ref[...], s64_ref[...], WA_HEAD // 4)
        wk = _rope(wk, c64_ref[...], s64_ref[...], WA_HEAD // 4)
    wqo_ref[0] = wq.astype(BF16)
    wko_ref[0] = wk.astype(BF16)
    qn = qn * MLA_QSCALE
    qr = qr * MLA_QSCALE
    lane = lax.broadcasted_iota(jnp.int32, (qn.shape[0], MLA_V), 1)
    ones_col = jnp.where(lane == 0, 1.0, 0.0)
    for h in range(MLA_HEADS):
        sl = slice(MLA_NOPE * h, MLA_NOPE * (h + 1))
        qm_ref[0, h] = jnp.concatenate([qn[:, sl], qr[:, sl]], axis=-1).astype(BF16)
        kh = jnp.concatenate([kn[:, sl], kr[:, :MLA_NOPE]], axis=-1)
        kmt_ref[0, h, 0] = kh.T.astype(BF16)
        vm_ref[0, h] = jnp.concatenate([vv[:, sl], ones_col], axis=-1).astype(BF16)


def _attn_prep(p32, p, tables, q_norm, q_up_p, kv_norm, kv_up_p, tk):
    bsz, t, _ = p.shape
    tm = tk
    rope = tables is not None

    def slab(name):
        blk = _cblk(name)
        return pl.BlockSpec((1, tm, COL[name][1]), lambda b, i: (b, i, blk))

    def full(a):
        nd = a.ndim
        return pl.BlockSpec(a.shape, lambda b, i: (0,) * nd)

    ins = [p, p, p, p32, p]
    in_specs = [slab(n) for n in ("dq", "dkv", "kr", "wq", "wk")]
    if rope:
        ins += list(tables)
        in_specs += [pl.BlockSpec((tm, LANE), lambda b, i: (i, 0))] * 4
    small = [q_norm.reshape(1, -1), q_up_p, kv_norm.reshape(1, -1), kv_up_p]
    ins += small
    in_specs += [full(a) for a in small]
    return pl.pallas_call(
        functools.partial(_prep_kernel, rope=rope),
        grid=(bsz, t // tm),
        in_specs=in_specs,
        out_specs=[pl.BlockSpec((1, MLA_HEADS, tm, LANE), lambda b, i: (b, 0, i, 0)),
                   pl.BlockSpec((1, MLA_HEADS, 1, LANE, tm), lambda b, i: (b, 0, i, 0, 0)),
                   pl.BlockSpec((1, MLA_HEADS, tm, LANE), lambda b, i: (b, 0, i, 0)),
                   pl.BlockSpec((1, tm, WA_HEADS * WA_HEAD), lambda b, i: (b, i, 0)),
                   pl.BlockSpec((1, tm, WA_KV_HEADS * WA_HEAD), lambda b, i: (b, i, 0))],
        out_shape=[jax.ShapeDtypeStruct((bsz, MLA_HEADS, t, LANE), BF16),
                   jax.ShapeDtypeStruct((bsz, MLA_HEADS, t // tm, LANE, tm), BF16),
                   jax.ShapeDtypeStruct((bsz, MLA_HEADS, t, LANE), BF16),
                   jax.ShapeDtypeStruct((bsz, t, WA_HEADS * WA_HEAD), BF16),
                   jax.ShapeDtypeStruct((bsz, t, WA_KV_HEADS * WA_HEAD), BF16)],
        compiler_params=_params("parallel", "parallel"),
        name="attn_prep",
    )(*ins)


def _rope_tables(t):
    pos = jnp.arange(t)
    row = (pos // GRID_W).astype(F32)[:, None]
    col = (pos % GRID_W).astype(F32)[:, None]
    out = []
    for d in (WA_HEAD, MLA_ROPE):
        q4 = d // 4
        freqs = ROPE_BASE ** (-jnp.arange(q4, dtype=F32) / q4)
        ar, ac = row * freqs[None, :], col * freqs[None, :]
        cos = jnp.concatenate([jnp.cos(ar), jnp.cos(ar), jnp.cos(ac), jnp.cos(ac)], axis=-1)
        sin = jnp.concatenate([-jnp.sin(ar), jnp.sin(ar), -jnp.sin(ac), jnp.sin(ac)], axis=-1)
        out += [jnp.tile(cos, (1, LANE // d)), jnp.tile(sin, (1, LANE // d))]
    return tuple(out)


WA_QBLOCKS = 8


def _wattn_kernel(*refs, band, nbt):
    if band:
        q_ref, kp_ref, kc_ref, kn_ref, vp_ref, vc_ref, vn_ref, kx_ref, vx_ref, sink_ref, o_ref = refs
    else:
        q_ref, kx_ref, vx_ref, sink_ref, o_ref = refs
    n = pl.program_id(1)
    tq = q_ref.shape[1]
    nq = tq // BLOCK if band else 1
    rows = tq // nq
    q = q_ref[0]
    kx = kx_ref[0]
    vx = vx_ref[0]
    if band:
        kc, vc = kc_ref[0], vc_ref[0]
        kblocks = [kp_ref[0]] + [kc[BLOCK * j:BLOCK * (j + 1)] for j in range(nq)] + [kn_ref[0]]
        vblocks = [vp_ref[0]] + [vc[BLOCK * j:BLOCK * (j + 1)] for j in range(nq)] + [vn_ref[0]]
        nk = 3 * BLOCK + kx.shape[0]
        qi = lax.broadcasted_iota(jnp.int32, (rows, nk), 0)
        kj = lax.broadcasted_iota(jnp.int32, (rows, nk), 1)
        in_band = jnp.abs(kj - BLOCK - qi) <= WINDOW
        is_ctx = kj >= 3 * BLOCK

    chains = [(j, g) for j in range(nq) for g in range(WA_KV_HEADS)]

    def scores(ch):
        j, g = ch
        ks = slice(WA_HEAD * g, WA_HEAD * (g + 1))
        qg = jnp.concatenate([q[rows * j:rows * (j + 1), WA_HEAD * (WA_GROUP * g + i):WA_HEAD * (WA_GROUP * g + i + 1)]
                              for i in range(WA_GROUP)], axis=0)
        if band:
            kb = jnp.concatenate([kblocks[j][:, ks], kblocks[j + 1][:, ks], kblocks[j + 2][:, ks], kx[:, ks]], axis=0)
        else:
            kb = kx[:, ks]
        s = jnp.einsum('qd,kd->qk', qg, kb, preferred_element_type=F32) * WA_SCALE
        if band:
            blk = n * nq + j
            ok = in_band & ((kj >= BLOCK) | (blk > 0)) & ((kj < 2 * BLOCK) | (blk < nbt - 1))
            ok = ok | is_ctx
            s = jnp.where(jnp.concatenate([ok] * WA_GROUP, axis=0), s, -jnp.inf)
        return s

    ss = [scores(ch) for ch in chains]

    def probs(ch, s):
        g = ch[1]
        sk = jnp.concatenate([jnp.full((rows, 1), sink_ref[0, WA_GROUP * g + i], F32) for i in range(WA_GROUP)],
                             axis=0)
        m = jnp.maximum(jnp.max(s, axis=-1, keepdims=True), sk)
        e = jnp.exp(s - m)
        inv = 1.0 / (jnp.sum(e, axis=-1, keepdims=True) + jnp.exp(sk - m))
        return (e * inv).astype(BF16)

    ps = [probs(ch, s) for ch, s in zip(chains, ss)]

    def values(ch, p):
        j, g = ch
        ks = slice(WA_HEAD * g, WA_HEAD * (g + 1))
        if band:
            vb = jnp.concatenate([vblocks[j][:, ks], vblocks[j + 1][:, ks], vblocks[j + 2][:, ks], vx[:, ks]], axis=0)
        else:
            vb = vx[:, ks]
        return jnp.dot(p, vb, preferred_element_type=F32)

    ogs = [values(ch, p) for ch, p in zip(chains, ps)]
    for j in range(nq):
        heads = []
        for g in range(WA_KV_HEADS):
            og = ogs[chains.index((j, g))]
            heads += [og[rows * i:rows * (i + 1)] for i in range(WA_GROUP)]
        o_ref[0, rows * j:rows * (j + 1), :] = jnp.concatenate(heads, axis=-1)


def _window_attn(wq, wk, p, wkx, px, sink, band):
    bsz, t, _ = wq.shape
    cx = wkx.shape[1]
    tq = WA_QBLOCKS * BLOCK if band else t
    nsteps = t // tq
    nbt = t // BLOCK
    vblk = _cblk("wv")
    kvw = WA_KV_HEADS * WA_HEAD
    ins = [wq]
    in_specs = [pl.BlockSpec((1, tq, WA_HEADS * WA_HEAD), lambda b, n: (b, n, 0))]
    if band:
        prev = lambda n: jnp.maximum(n * WA_QBLOCKS - 1, 0)
        nxt = lambda n: jnp.minimum((n + 1) * WA_QBLOCKS, nbt - 1)
        for src, cblk in ((wk, 0), (p, vblk)):
            ins += [src] * 3
            in_specs += [pl.BlockSpec((1, BLOCK, kvw), functools.partial(lambda c, b, n: (b, prev(n), c), cblk)),
                         pl.BlockSpec((1, tq, kvw), functools.partial(lambda c, b, n: (b, n, c), cblk)),
                         pl.BlockSpec((1, BLOCK, kvw), functools.partial(lambda c, b, n: (b, nxt(n), c), cblk))]
    ins += [wkx, px, sink.reshape(1, WA_HEADS)]
    in_specs += [pl.BlockSpec((1, cx, kvw), lambda b, n: (b, 0, 0)),
                 pl.BlockSpec((1, cx, kvw), lambda b, n: (b, 0, vblk)),
                 pl.BlockSpec(memory_space=pltpu.SMEM)]
    return pl.pallas_call(
        functools.partial(_wattn_kernel, band=band, nbt=nbt),
        grid=(bsz, nsteps),
        in_specs=in_specs,
        out_specs=pl.BlockSpec((1, tq, WA_HEADS * WA_HEAD), lambda b, n: (b, n, 0)),
        out_shape=jax.ShapeDtypeStruct((bsz, t, WA_HEADS * WA_HEAD), F32),
        compiler_params=_params("parallel", "parallel"),
        name="window_attn" if band else "context_gqa",
    )(*ins)


MLA_HPS = 2


def _mla_kernel(*refs, nkt, extra):
    if extra:
        q_ref, kt_ref, v_ref, ktx_ref, vx_ref, o_ref = refs
    else:
        q_ref, kt_ref, v_ref, o_ref = refs
    tq = q_ref.shape[2]
    tk = kt_ref.shape[4]

    heads = range(MLA_HPS)

    def update(carry, kts, vs, staged):
        def score(hh):
            return jnp.dot(q_ref[0, hh], kts[hh], preferred_element_type=F32)

        def finish(hh, s):
            m, acc = carry[hh]
            m_new = jnp.maximum(m, jnp.max(s, axis=-1, keepdims=True))
            p = jnp.exp2(s - m_new).astype(BF16)
            return m_new, acc * jnp.exp2(m - m_new) + jnp.dot(p, vs[hh], preferred_element_type=F32)

        if staged:
            ss = [score(hh) for hh in heads]
            return tuple(finish(hh, ss[hh]) for hh in heads)
        return tuple(finish(hh, score(hh)) for hh in heads)

    def body(j, carry):
        start = pl.multiple_of(j * tk, tk)
        return update(carry, [kt_ref[0, hh, j] for hh in heads],
                      [v_ref[0, hh, pl.ds(start, tk), :] for hh in heads], staged=False)

    init = (jnp.full((tq, 1), -jnp.inf, F32), jnp.zeros((tq, LANE), F32))
    carry = lax.fori_loop(0, nkt, body, (init,) * MLA_HPS, unroll=True)
    if extra:
        carry = update(carry, [ktx_ref[0, hh, 0] for hh in heads], [vx_ref[0, hh] for hh in heads], staged=True)
    o_ref[0] = jnp.concatenate([carry[hh][1][:, :MLA_V] / carry[hh][1][:, MLA_V:MLA_V + 1] for hh in heads],
                               axis=-1)


def _mla_attn(qm, kmt, vm, kmtx=None, vmx=None):
    bsz, nh, t, _ = qm.shape
    nkt, tk = kmt.shape[2], kmt.shape[4]
    tkv = vm.shape[2]
    tq = min(1024, t)
    extra = kmtx is not None
    ins = [qm, kmt, vm]
    in_specs = [pl.BlockSpec((1, MLA_HPS, tq, LANE), lambda b, h, i: (b, h, i, 0)),
                pl.BlockSpec((1, MLA_HPS, nkt, LANE, tk), lambda b, h, i: (b, h, 0, 0, 0)),
                pl.BlockSpec((1, MLA_HPS, tkv, LANE), lambda b, h, i: (b, h, 0, 0))]
    if extra:
        cx = vmx.shape[2]
        ins += [kmtx, vmx]
        in_specs += [pl.BlockSpec((1, MLA_HPS, 1, LANE, cx), lambda b, h, i: (b, h, 0, 0, 0)),
                     pl.BlockSpec((1, MLA_HPS, cx, LANE), lambda b, h, i: (b, h, 0, 0))]
    return pl.pallas_call(
        functools.partial(_mla_kernel, nkt=nkt, extra=extra),
        grid=(bsz, nh // MLA_HPS, t // tq),
        in_specs=in_specs,
        out_specs=pl.BlockSpec((1, tq, MLA_HPS * MLA_V), lambda b, h, i: (b, i, h)),
        out_shape=jax.ShapeDtypeStruct((bsz, t, nh * MLA_V), F32),
        compiler_params=_params("parallel", "parallel", "parallel"),
        name="mla_attn",
    )(*ins)


def _merge_kernel(x_ref, g_ref, of_ref, ob_ref, bonus_ref, yb_ref, yc_ref, yd_ref, z_ref, gate_ref,
                  gnw_ref, gnb_ref, wb_ref, wo_ref, fnw_ref, out_ref, *, final):
    o = of_ref[0] + ob_ref[0]
    oc = o - _head_sum(o) * (1.0 / RW_HEAD)
    var = _head_sum(oc * oc) * (1.0 / RW_HEAD)
    ya = (oc * lax.rsqrt(var + RW_GN_EPS)) * gnw_ref[...] + gnb_ref[...] + bonus_ref[0]
    ys = (ya, yb_ref[0], yc_ref[0], yd_ref[0])
    m = None
    for i, y in enumerate(ys):
        zh = z_ref[0, :, BRANCH_W * i:BRANCH_W * (i + 1)].astype(F32)
        gh = gate_ref[0, :, D_MODEL * i:D_MODEL * (i + 1)].astype(F32)
        yz = (y * (zh * jnp.tanh(zh) + zh)).astype(BF16)
        term = (jnp.tanh(gh) + 1.0) * jnp.dot(yz, wb_ref[i], preferred_element_type=F32)
        m = term if m is None else m + term
    xn = x_ref[0] + g_ref[0] * jnp.dot(m.astype(BF16), wo_ref[...], preferred_element_type=F32)
    if final:
        xn = _rms(xn, fnw_ref[...])
    out_ref[0] = xn


def _merge(x, g, o, bonus, yb, yc, yd, p, gn_w, gn_b, w_branch, w_out, final_norm_w, final):
    bsz, t, d = x.shape
    tm = min(256, t)
    zblk, gblk = _cblk("z"), _cblk("gate")
    row = lambda w: pl.BlockSpec((1, tm, w), lambda b, i: (b, i, 0))
    return pl.pallas_call(
        functools.partial(_merge_kernel, final=final),
        grid=(bsz, t // tm),
        in_specs=[row(d),
                  pl.BlockSpec((1, 1, d), lambda b, i: (b, 0, 0)),
                  row(RW_C), row(RW_C), row(RW_C),
                  row(BRANCH_W), row(BRANCH_W), row(BRANCH_W),
                  pl.BlockSpec((1, tm, COL["z"][1]), lambda b, i: (b, i, zblk)),
                  pl.BlockSpec((1, tm, COL["gate"][1]), lambda b, i: (b, i, gblk)),
                  pl.BlockSpec((1, RW_C), lambda b, i: (0, 0)),
                  pl.BlockSpec((1, RW_C), lambda b, i: (0, 0)),
                  pl.BlockSpec((N_BRANCH, BRANCH_W, d), lambda b, i: (0, 0, 0)),
                  pl.BlockSpec((d, d), lambda b, i: (0, 0)),
                  pl.BlockSpec((1, d), lambda b, i: (0, 0))],
        out_specs=row(d),
        out_shape=jax.ShapeDtypeStruct((bsz, t, d), F32),
        compiler_params=_params("parallel", "parallel"),
        name="merge",
    )(x, g, o[0], o[1], bonus, yb, yc, yd, p, p, gn_w.reshape(1, RW_C), gn_b.reshape(1, RW_C),
      w_branch, w_out, final_norm_w.reshape(1, d))


MLA_TK = 1024


def _permute_q_up(q_up):
    r = q_up.shape[0]
    w = q_up.reshape(r, MLA_HEADS, MLA_NOPE + MLA_ROPE)
    nope = w[:, :, :MLA_NOPE].reshape(r, MLA_HEADS * MLA_NOPE)
    rope = jnp.pad(w[:, :, MLA_NOPE:], ((0, 0), (0, 0), (0, MLA_NOPE - MLA_ROPE))).reshape(r, MLA_HEADS * MLA_NOPE)
    return jnp.concatenate([nope, rope], axis=-1)


def _permute_kv_up(kv_up):
    r = kv_up.shape[0]
    w = kv_up.reshape(r, MLA_HEADS, MLA_NOPE + MLA_V)
    return jnp.concatenate([w[:, :, :MLA_NOPE].reshape(r, -1), w[:, :, MLA_NOPE:].reshape(r, -1)], axis=-1)


def kernel(x, c, ctx, c_ctx, norm_w, ada_w, ada_b, w_in, rwkv_mu, rwkv_w0, rwkv_w_up, rwkv_a0, rwkv_a_up, rwkv_k_k, rwkv_k_a, rwkv_r_k, rwkv_gn_w, rwkv_gn_b, conv_w, conv_b, conv_ln_w, conv_ln_b, attn_sink, mla_q_norm, mla_q_up, mla_kv_norm, mla_kv_up, w_branch, w_out, final_norm_w):
    bsz, t, d = x.shape
    n_ctx = ctx.shape[1]
    depth = norm_w.shape[0]
    assert t % MLA_TK == 0 and t % BLOCK == 0 and n_ctx % RW_CHUNK == 0 and n_ctx <= MLA_TK

    cvec = jnp.concatenate([c, c_ctx[None, :], jnp.zeros((SUBLANE - bsz - 1, d), F32)], axis=0)
    mod = _modulation(cvec, ada_w, ada_b)
    tables = _rope_tables(t)
    s_zero = jnp.zeros((bsz, 2, RW_HEAD, RW_C), F32)

    for l in range(depth):
        last = l == depth - 1
        mx = mod[l, :bsz][:, None, :]
        mc = jnp.broadcast_to(mod[l, bsz][None, None, :], (bsz, 1, 3 * d))
        sh_x, sc_x, g_x = mx[..., :d], mx[..., d:2 * d], mx[..., 2 * d:]
        sh_c, sc_c, g_c = mc[..., :d], mc[..., d:2 * d], mc[..., 2 * d:]
        w_p = _pack_cols(w_in[l]).astype(BF16)
        px32, px = _in_proj(x, norm_w[l], sc_x, sh_x, w_p)
        pc32, pc = _in_proj(ctx, norm_w[l], sc_c, sh_c, w_p)

        rw_args = (rwkv_mu[l], rwkv_w0[l], rwkv_w_up[l], rwkv_a0[l], rwkv_a_up[l], rwkv_k_k[l], rwkv_k_a[l],
                   rwkv_r_k[l].reshape(RW_C))
        o_c, bonus_c, s_ctx = _rwkv(pc32, pc, s_zero, *rw_args)
        o_x, bonus_x, _ = _rwkv(px32, px, s_ctx, *rw_args)

        conv_args = (conv_w[l], conv_b[l], conv_ln_w[l], conv_ln_b[l])
        yb_x = _conv(px, *conv_args)

        q_up_p = _permute_q_up(mla_q_up[l]).astype(BF16)
        kv_up_p = _permute_kv_up(mla_kv_up[l]).astype(BF16)
        prep_w = (mla_q_norm[l], q_up_p, mla_kv_norm[l], kv_up_p)
        qm_x, kmt_x, vm_x, wq_x, wk_x = _attn_prep(px32, px, tables, *prep_w, tk=MLA_TK)
        qm_c, kmt_c, vm_c, wq_c, wk_c = _attn_prep(pc32, pc, None, *prep_w, tk=n_ctx)

        yc_x = _window_attn(wq_x, wk_x, px, wk_c, pc, attn_sink[l], band=True)
        yd_x = _mla_attn(qm_x, kmt_x, vm_x, kmt_c, vm_c)

        merge_w = (rwkv_gn_w[l], rwkv_gn_b[l], w_branch[l].astype(BF16), (0.5 * w_out[l]).astype(BF16),
                   final_norm_w)
        x = _merge(x, g_x, o_x, bonus_x, yb_x, yc_x, yd_x, px, *merge_w, final=last)

        if not last:
            yb_c = _conv(pc, *conv_args)
            yc_c = _window_attn(wq_c, wk_c, pc, wk_c, pc, attn_sink[l], band=False)
            yd_c = _mla_attn(qm_c, kmt_c, vm_c)
            ctx = _merge(ctx, g_c, o_c, bonus_c, yb_c, yc_c, yd_c, pc, *merge_w, final=False)

    return x
```

```python
import functools

import jax
import jax.numpy as jnp
from jax import lax
from jax.experimental import pallas as pl
from jax.experimental.pallas import tpu as pltpu

F32 = jnp.float32
BF16 = jnp.bfloat16

D_MODEL = 1024
GRID_W = 64
N_BRANCH = 4
BRANCH_W = 512
RW_HEADS = 8
RW_HEAD = 64
RW_C = RW_HEADS * RW_HEAD
RW_LORA = 64
RW_GN_EPS = 64e-5
CONV_CH = 512
CONV_K = 31
CONV_LN_EPS = 1e-5
WA_HEADS = 8
WA_KV_HEADS = 2
WA_HEAD = 64
WA_GROUP = WA_HEADS // WA_KV_HEADS
WINDOW = 128
BLOCK = 128
WA_SCALE = WA_HEAD ** -0.5
MLA_HEADS = 8
MLA_Q_RANK = 256
MLA_KV_RANK = 128
MLA_NOPE = 64
MLA_ROPE = 32
MLA_V = 64
MLA_SCALE = (MLA_NOPE + MLA_ROPE) ** -0.5
MLA_QSCALE = MLA_SCALE * 1.4426950408889634
ROPE_BASE = 10000.0
NORM_EPS = 1e-6

LANE = 128
SUBLANE = 8
VMEM_LIMIT = 48 * 1024 * 1024

_SRC = {}
_off = 0
for _name, _size in (("r", RW_C), ("k", RW_C), ("v", RW_C), ("wd", RW_LORA), ("ad", RW_LORA),
                     ("cv", 2 * CONV_CH), ("wq", WA_HEADS * WA_HEAD), ("wk", WA_KV_HEADS * WA_HEAD),
                     ("wv", WA_KV_HEADS * WA_HEAD), ("dq", MLA_Q_RANK), ("dkv", MLA_KV_RANK),
                     ("kr", MLA_ROPE), ("z", N_BRANCH * BRANCH_W), ("gate", N_BRANCH * D_MODEL)):
    _SRC[_name] = (_off, _size)
    _off += _size
N_IN = _off

_PACK32 = (("r", 512), ("k", 512), ("v", 512), ("wq", 512))
_PACK16 = (("gate", 4096), ("z", 2048), ("cv", 1024), ("dq", 256), ("wdad", 128),
           ("wk", 128), ("wv", 128), ("dkv", 128), ("kr", 128), ("pad", 128))
_PACK = _PACK32 + _PACK16
COL = {}
for _pack in (_PACK32, _PACK16):
    _off = 0
    for _name, _size in _pack:
        assert _off % _size == 0
        COL[_name] = (_off, _size)
        _off += _size
N32 = sum(s for _, s in _PACK32)
N16 = sum(s for _, s in _PACK16)
N_PACK = N32 + N16


def _cblk(name):
    off, size = COL[name]
    return off // size


def _pack_cols(w):
    def src(name):
        o, s = _SRC[name]
        return w[..., o:o + s]
    pieces = []
    for name, size in _PACK:
        if name in ("z", "gate"):
            pieces.append(0.5 * src(name))
        elif name == "wdad":
            pieces += [src("wd"), src("ad")]
        elif name == "kr":
            pieces += [src("kr"), jnp.zeros(w.shape[:-1] + (size - MLA_ROPE,), w.dtype)]
        elif name == "pad":
            pieces.append(jnp.zeros(w.shape[:-1] + (size,), w.dtype))
        else:
            pieces.append(src(name))
    return jnp.concatenate(pieces, axis=-1)


def _params(*sem):
    return pltpu.CompilerParams(dimension_semantics=sem, vmem_limit_bytes=VMEM_LIMIT)


def _sigmoid(x):
    return 0.5 * jnp.tanh(0.5 * x) + 0.5


def _split2(x):
    hi = x.astype(BF16)
    lo = (x - hi.astype(F32)).astype(BF16)
    return hi, lo


def _split3(x):
    hi = x.astype(BF16)
    r1 = x - hi.astype(F32)
    mid = r1.astype(BF16)
    lo = (r1 - mid.astype(F32)).astype(BF16)
    return hi, mid, lo


def _mod_kernel(c_ref, w_ref, b_ref, o_ref):
    cv = c_ref[...]
    sc = cv * jax.nn.sigmoid(cv)
    o_ref[0] = jnp.dot(sc, w_ref[0], preferred_element_type=F32,
                       precision=lax.Precision.HIGHEST) + b_ref[0]


def _modulation(cvec, ada_w, ada_b):
    depth, d, d3 = ada_w.shape
    rows = cvec.shape[0]
    tn = 512
    return pl.pallas_call(
        _mod_kernel,
        grid=(depth, d3 // tn),
        in_specs=[pl.BlockSpec((rows, d), lambda l, j: (0, 0)),
                  pl.BlockSpec((1, d, tn), lambda l, j: (l, 0, j)),
                  pl.BlockSpec((1, 1, tn), lambda l, j: (l, 0, j))],
        out_specs=pl.BlockSpec((1, rows, tn), lambda l, j: (l, 0, j)),
        out_shape=jax.ShapeDtypeStruct((depth, rows, d3), F32),
        compiler_params=_params("parallel", "parallel"),
        name="modulation",
    )(cvec, ada_w, ada_b.reshape(depth, 1, d3))


INPROJ_TN = 1024
assert N32 % INPROJ_TN == 0 and N16 % INPROJ_TN == 0
N32_TILES = N32 // INPROJ_TN


def _inproj_kernel(x_ref, nw_ref, sc_ref, sh_ref, w_ref, o32_ref, o16_ref, h_ref):
    j = pl.program_id(2)

    @pl.when(j == 0)
    def _():
        xf = x_ref[0]
        ms = jnp.mean(xf * xf, axis=-1, keepdims=True)
        hn = (xf * lax.rsqrt(ms + NORM_EPS)) * nw_ref[...]
        h_ref[...] = (hn * (1.0 + sc_ref[0]) + sh_ref[0]).astype(BF16)

    @pl.when(j < N32_TILES)
    def _():
        o32_ref[0] = jnp.dot(h_ref[...], w_ref[...], preferred_element_type=F32)

    @pl.when(j >= N32_TILES)
    def _():
        o16_ref[0] = jnp.dot(h_ref[...], w_ref[...], preferred_element_type=F32).astype(BF16)


def _in_proj(x, norm_w, scale, shift, w_packed):
    bsz, t, d = x.shape
    tm = min(1024, t)
    tn = INPROJ_TN
    return pl.pallas_call(
        _inproj_kernel,
        grid=(bsz, t // tm, N_PACK // tn),
        in_specs=[pl.BlockSpec((1, tm, d), lambda b, i, j: (b, i, 0)),
                  pl.BlockSpec((1, d), lambda b, i, j: (0, 0)),
                  pl.BlockSpec((1, 1, d), lambda b, i, j: (b, 0, 0)),
                  pl.BlockSpec((1, 1, d), lambda b, i, j: (b, 0, 0)),
                  pl.BlockSpec((d, tn), lambda b, i, j: (0, j))],
        out_specs=[pl.BlockSpec((1, tm, tn), lambda b, i, j: (b, i, jnp.minimum(j, N32_TILES - 1))),
                   pl.BlockSpec((1, tm, tn), lambda b, i, j: (b, i, jnp.maximum(j - N32_TILES, 0)))],
        out_shape=[jax.ShapeDtypeStruct((bsz, t, N32), F32),
                   jax.ShapeDtypeStruct((bsz, t, N16), BF16)],
        scratch_shapes=[pltpu.VMEM((tm, d), BF16)],
        compiler_params=_params("parallel", "parallel", "arbitrary"),
        name="in_proj",
    )(x, norm_w.reshape(1, d), scale, shift, w_packed)


RW_CHUNK = 64
P_DATA = 1
P_STATE = 3


PAIR = 2 * RW_HEAD
N_PAIR = RW_C // PAIR


def _dot1(a, b, passes=1):
    def e(x, y):
        return jnp.dot(x, y, preferred_element_type=F32)
    if passes == 1:
        return e(a.astype(BF16), b.astype(BF16))
    ah, al = _split2(a)
    bh, bl = _split2(b)
    return e(ah, bh) + (e(ah, bl) + e(al, bh))


def _lo_lanes(shape):
    return lax.broadcasted_iota(jnp.int32, shape, len(shape) - 1) % PAIR < RW_HEAD


def _bd2(y):
    lo = _lo_lanes(y.shape)
    zero = jnp.zeros_like(y)
    return jnp.concatenate([jnp.where(lo, y, zero), jnp.where(lo, zero, y)], axis=0)


def _head_sum(x):
    outs = []
    for p in range(x.shape[-1] // PAIR):
        s = x[:, PAIR * p:PAIR * (p + 1)]
        lo = _lo_lanes(s.shape)
        s0 = jnp.sum(jnp.where(lo, s, 0.0), axis=-1, keepdims=True)
        s1 = jnp.sum(jnp.where(lo, 0.0, s), axis=-1, keepdims=True)
        outs.append(jnp.where(lo, s0, s1))
    return outs[0] if len(outs) == 1 else jnp.concatenate(outs, axis=-1)


def _rwkv_p1_kernel(r_ref, k_ref, v_ref, wa_ref, rp_ref, kp_ref, vp_ref, wap_ref,
                    rn_ref, kn_ref, vn_ref, wan_ref,
                    mur_ref, muk_ref, muv_ref, muwa_ref, w0_ref, wup_ref, a0_ref, aup_ref,
                    kkw_ref, kaw_ref, rkw_ref,
                    qt_ref, o0_ref, g_ref, h_ref, bonus_ref, *, nc):
    L = RW_CHUNK
    R = r_ref.shape[1]
    c = pl.program_id(1)

    def shift_mix(cur_ref, p_ref, n_ref, mu_ref):
        cur = cur_ref[0].astype(F32)
        hr = p_ref.shape[1]
        prow = jnp.where(c > 0, p_ref[0, hr - 1:hr, :].astype(F32), 0.0)
        nrow = jnp.where(c < nc - 1, n_ref[0, 0:1, :].astype(F32), 0.0)
        ridx = lax.broadcasted_iota(jnp.int32, cur.shape, 0)
        fprev = jnp.where(ridx == 0, prow, pltpu.roll(cur, 1, 0))
        fnext = jnp.where(ridx == R - 1, nrow, pltpu.roll(cur, R - 1, 0))
        return cur + mu_ref[...] * (0.5 * (fprev + fnext) - cur)

    r = shift_mix(r_ref, rp_ref, rn_ref, mur_ref)
    k = shift_mix(k_ref, kp_ref, kn_ref, muk_ref)
    v = shift_mix(v_ref, vp_ref, vn_ref, muv_ref)
    wa = shift_mix(wa_ref, wap_ref, wan_ref, muwa_ref)
    wd_t = jnp.tanh(wa[:, :RW_LORA]).astype(BF16)
    ad = wa[:, RW_LORA:].astype(BF16)

    kk = k * kkw_ref[...]
    kk = kk / jnp.maximum(jnp.sqrt(_head_sum(kk * kk)), 1e-12)

    row = lax.broadcasted_iota(jnp.int32, (L, PAIR), 0)
    col = lax.broadcasted_iota(jnp.int32, (L, PAIR), 1) % RW_HEAD
    eye_l = jnp.where(row == col, 1.0, 0.0)
    rown = lax.broadcasted_iota(jnp.int32, (RW_HEAD, PAIR), 0)
    coln = lax.broadcasted_iota(jnp.int32, (RW_HEAD, PAIR), 1) % RW_HEAD
    eye_n = rown == coln
    trow = lax.broadcasted_iota(jnp.int32, (R, R), 0)
    tcol = lax.broadcasted_iota(jnp.int32, (R, R), 1)
    same_chunk = (trow // L) == (tcol // L)
    steps = L.bit_length() - 1
    kdir_sum = None
    dense = []
    for d in range(2):
        tri = (same_chunk & ((tcol <= trow) if d == 0 else (tcol >= trow))).astype(BF16)
        last = L - 1 if d == 0 else 0

        w_pre = w0_ref[d:d + 1, :] + jnp.dot(wd_t, wup_ref[d].astype(BF16), preferred_element_type=F32)
        w_log = -(jnp.maximum(-w_pre, 0.0) + jnp.log1p(jnp.exp(-jnp.abs(w_pre)))) - 0.5
        logw = -jnp.exp(w_log)
        lh, lm, ll = _split3(logw)
        cum = (jnp.dot(tri, lh, preferred_element_type=F32)
               + (jnp.dot(tri, lm, preferred_element_type=F32) + jnp.dot(tri, ll, preferred_element_type=F32)))
        a = jax.nn.sigmoid(a0_ref[d:d + 1, :] + jnp.dot(ad, aup_ref[d].astype(BF16), preferred_element_type=F32))

        cum_ls = [cum[j * L + last:j * L + last + 1, :] for j in range(R // L)]
        cum_l = jnp.concatenate([jnp.broadcast_to(x, (L, RW_C)) for x in cum_ls], axis=0)
        e_neg = jnp.exp(-cum)
        e_l = jnp.exp(cum_l - cum)
        w_l = [jnp.exp(x) for x in cum_ls]
        kdir = k * (1.0 + (a - 1.0) * kaw_ref[...])
        kdir_sum = kdir if kdir_sum is None else kdir_sum + kdir
        b = kk * a
        al_c = kk * jnp.exp(cum - logw)
        rho_c = r * jnp.exp(cum)
        be_c = b * e_neg
        ka_c = kdir * e_neg
        bel_c = b * e_l
        kal_c = kdir * e_l
        dense.append((al_c, rho_c, be_c, ka_c, bel_c, kal_c, w_l))

    bonus_ref[0] = _head_sum(r * kdir_sum * rkw_ref[...]) * v

    chains = [(j, d, p) for j in range(R // L) for d in range(2) for p in range(N_PAIR)]

    def sub(x, ch):
        return x[ch[0] * L:(ch[0] + 1) * L, PAIR * ch[2]:PAIR * (ch[2] + 1)]

    def bd(y):
        return _bd2(y.astype(BF16))

    def each(fn, *lists):
        return [fn(ch, *(lst[i] for lst in lists)) for i, ch in enumerate(chains)]

    def products(ch):
        al, rho, be, ka = (sub(x, ch) for x in dense[ch[1]][:4])
        lo = _lo_lanes(be.shape)
        zero = jnp.zeros_like(be)
        rhs_nt = jnp.concatenate([jnp.where(lo, be, zero), jnp.where(lo, zero, be),
                                  jnp.where(lo, ka, zero), jnp.where(lo, zero, ka)], axis=0)
        lhs = jnp.concatenate([al, rho], axis=0)
        return jnp.einsum('ld,md->lm', lhs.astype(BF16), rhs_nt.astype(BF16),
                          preferred_element_type=F32)

    ms = each(products)

    def block_t(x):
        xt = x.T
        return jnp.concatenate([xt[:RW_HEAD], xt[RW_HEAD:]], axis=1)

    belts = each(lambda ch: block_t(sub(dense[ch[1]][4], ch)))
    kalts = each(lambda ch: block_t(sub(dense[ch[1]][5], ch)))

    def masks(ch, m):
        d = ch[1]
        incl = (col <= row) if d == 0 else (col >= row)
        strict = (col < row) if d == 0 else (col > row)
        return (-jnp.where(strict, m[:L, :PAIR], 0.0), jnp.where(strict, m[:L, PAIR:], 0.0),
                jnp.where(incl, m[L:, :PAIR], 0.0), jnp.where(incl, m[L:, PAIR:], 0.0))

    mk = each(masks, ms)
    negs, akss, rbis, rkis = ([x[i] for x in mk] for i in range(4))

    accs = [eye_l + neg for neg in negs]
    pws = each(lambda ch, neg: _dot1(neg, bd(neg), P_DATA), negs)
    for _ in range(1, steps - 1):
        sts = each(lambda ch, acc, pw: _dot1(jnp.concatenate([acc, pw], axis=0), bd(pw), P_DATA), accs, pws)
        accs = [acc + st[:L] for acc, st in zip(accs, sts)]
        pws = [st[L:] for st in sts]
    tmats = each(lambda ch, acc, pw: acc + _dot1(acc, bd(pw), P_DATA), accs, pws)

    vss = each(lambda ch, aks, rki, kalt: _dot1(jnp.concatenate([aks, rki, kalt], axis=0), bd(sub(v, ch)), P_DATA),
               akss, rkis, kalts)
    tus = each(lambda ch, tmat, vs: _dot1(tmat, jnp.concatenate(
        [bd(sub(dense[ch[1]][0], ch)), bd(vs[:L])], axis=1), P_DATA), tmats, vss)
    uas = [tu[:, :PAIR] for tu in tus]
    d0s = [-tu[:, PAIR:] for tu in tus]
    rus = each(lambda ch, rbi, belt, ua, d0: _dot1(jnp.concatenate([rbi, belt], axis=0),
                                                   jnp.concatenate([bd(ua), bd(d0)], axis=1), P_DATA),
               rbis, belts, uas, d0s)

    for i, ch in enumerate(chains):
        j, d, p = ch
        rows = slice(j * L, (j + 1) * L)
        sl = slice(PAIR * p, PAIR * (p + 1))
        qt_ref[0, d, rows, sl] = sub(dense[d][1], ch) - rus[i][:L, :PAIR]
        o0_ref[0, d, rows, sl] = rus[i][:L, PAIR:] + vss[i][L:2 * L]
        g_ref[0, d, j, :, sl] = jnp.where(eye_n, dense[d][6][j][:, sl], 0.0) - rus[i][L:, :PAIR]
        h_ref[0, d, j, :, sl] = rus[i][L:, PAIR:] + vss[i][2 * L:]


def _rwkv_p2_kernel(qtf_ref, o0f_ref, gf_ref, hf_ref, qtb_ref, o0b_ref, gb_ref, hb_ref, s0_ref,
                    of_ref, ob_ref, sfin_ref, s_ref, *, nsteps, cps):
    L = RW_CHUNK
    c = pl.program_id(1)
    dirs = ((qtf_ref, o0f_ref, gf_ref, hf_ref, of_ref), (qtb_ref, o0b_ref, gb_ref, hb_ref, ob_ref))

    @pl.when(c == 0)
    def _():
        s_ref[...] = s0_ref[0]

    def body(i, states):
        cis = (i, cps - 1 - i)
        rows = [pl.ds(pl.multiple_of(ci * L, L), L) for ci in cis]
        qts = [dirs[d][0][0, 0, rows[d], :] for d in range(2)]
        gs = [dirs[d][2][0, 0, cis[d]] for d in range(2)]
        prods = [[_dot1(jnp.concatenate([qts[d][:, PAIR * p:PAIR * (p + 1)], gs[d][:, PAIR * p:PAIR * (p + 1)]],
                                        axis=0), _bd2(states[d][:, PAIR * p:PAIR * (p + 1)]), P_STATE)
                  for p in range(N_PAIR)] for d in range(2)]
        new = []
        for d in range(2):
            st = jnp.concatenate(prods[d], axis=1)
            dirs[d][4][0, rows[d], :] = st[:L] + dirs[d][1][0, 0, rows[d], :]
            new.append(st[L:] + dirs[d][3][0, 0, cis[d]])
        return tuple(new)

    s_new = lax.fori_loop(0, cps, body, (s_ref[0], s_ref[1]), unroll=True)
    s_ref[0] = s_new[0]
    s_ref[1] = s_new[1]

    @pl.when(c == nsteps - 1)
    def _():
        sfin_ref[0, 0] = s_new[0]
        sfin_ref[0, 1] = s_new[1]


RW_PREP_CHUNKS = 4
RW_SCAN_CHUNKS = 8
HALO16 = 2 * SUBLANE


def _rwkv(p32, p16, s0, mu, w0, w_up, a0, a_up, k_k, k_a, r_k):
    bsz, t, _ = p32.shape
    L = RW_CHUNK
    nc = t // L

    def halo_rows(name):
        return HALO16 if name == "wdad" else SUBLANE

    cpb = min(RW_PREP_CHUNKS, nc)
    R = cpb * L
    nblk = t // R

    def cur(name):
        blk = _cblk(name)
        w = COL[name][1]
        return pl.BlockSpec((1, R, w), lambda b, c: (b, c, blk))

    def prev(name):
        blk = _cblk(name)
        w = COL[name][1]
        hr = halo_rows(name)
        return pl.BlockSpec((1, hr, w), lambda b, c: (b, jnp.maximum(c * (R // hr) - 1, 0), blk))

    def nxt(name):
        blk = _cblk(name)
        w = COL[name][1]
        hr = halo_rows(name)
        return pl.BlockSpec((1, hr, w), lambda b, c: (b, jnp.minimum((c + 1) * (R // hr), t // hr - 1), blk))

    def full(a):
        nd = a.ndim
        return pl.BlockSpec(a.shape, lambda b, c: (0,) * nd)

    names = ("r", "k", "v", "wdad")
    srcs = [p32, p32, p32, p16]
    mu_p = [mu[o:o + s].reshape(1, s) for o, s in
            ((0, RW_C), (RW_C, RW_C), (2 * RW_C, RW_C), (3 * RW_C, 2 * RW_LORA))]
    small = mu_p + [w0, w_up, a0, a_up, k_k.reshape(1, RW_C), k_a.reshape(1, RW_C), r_k.reshape(1, RW_C)]
    qt, o0, g, h, bonus = pl.pallas_call(
        functools.partial(_rwkv_p1_kernel, nc=nblk),
        grid=(bsz, nblk),
        in_specs=[cur(n) for n in names] + [prev(n) for n in names] + [nxt(n) for n in names]
                 + [full(a) for a in small],
        out_specs=[pl.BlockSpec((1, 2, R, RW_C), lambda b, c: (b, 0, c, 0)),
                   pl.BlockSpec((1, 2, R, RW_C), lambda b, c: (b, 0, c, 0)),
                   pl.BlockSpec((1, 2, cpb, RW_HEAD, RW_C), lambda b, c: (b, 0, c, 0, 0)),
                   pl.BlockSpec((1, 2, cpb, RW_HEAD, RW_C), lambda b, c: (b, 0, c, 0, 0)),
                   pl.BlockSpec((1, R, RW_C), lambda b, c: (b, c, 0))],
        out_shape=[jax.ShapeDtypeStruct((bsz, 2, t, RW_C), F32),
                   jax.ShapeDtypeStruct((bsz, 2, t, RW_C), F32),
                   jax.ShapeDtypeStruct((bsz, 2, nc, RW_HEAD, RW_C), F32),
                   jax.ShapeDtypeStruct((bsz, 2, nc, RW_HEAD, RW_C), F32),
                   jax.ShapeDtypeStruct((bsz, t, RW_C), F32)],
        compiler_params=_params("parallel", "parallel"),
        name="rwkv_chunk_prep",
    )(*(srcs * 3), *small)

    cps = min(RW_SCAN_CHUNKS, nc)
    nsteps = nc // cps

    tblk = (1, 1, cps * L, RW_C)
    gblk = (1, 1, cps, RW_HEAD, RW_C)
    sblk = (1, 2, RW_HEAD, RW_C)
    fwd_t = pl.BlockSpec(tblk, lambda b, c: (b, 0, c, 0))
    bwd_t = pl.BlockSpec(tblk, lambda b, c: (b, 1, nsteps - 1 - c, 0))
    fwd_g = pl.BlockSpec(gblk, lambda b, c: (b, 0, c, 0, 0))
    bwd_g = pl.BlockSpec(gblk, lambda b, c: (b, 1, nsteps - 1 - c, 0, 0))
    o_f, o_b, s_fin = pl.pallas_call(
        functools.partial(_rwkv_p2_kernel, nsteps=nsteps, cps=cps),
        grid=(bsz, nsteps),
        in_specs=[fwd_t, fwd_t, fwd_g, fwd_g, bwd_t, bwd_t, bwd_g, bwd_g,
                  pl.BlockSpec(sblk, lambda b, c: (b, 0, 0, 0))],
        out_specs=[pl.BlockSpec((1, cps * L, RW_C), lambda b, c: (b, c, 0)),
                   pl.BlockSpec((1, cps * L, RW_C), lambda b, c: (b, nsteps - 1 - c, 0)),
                   pl.BlockSpec(sblk, lambda b, c: (b, 0, 0, 0))],
        out_shape=[jax.ShapeDtypeStruct((bsz, t, RW_C), F32),
                   jax.ShapeDtypeStruct((bsz, t, RW_C), F32),
                   jax.ShapeDtypeStruct((bsz, 2, RW_HEAD, RW_C), F32)],
        scratch_shapes=[pltpu.VMEM((2, RW_HEAD, RW_C), F32)],
        compiler_params=_params("parallel", "arbitrary"),
        name="rwkv_state_scan",
    )(qt, o0, g, h, qt, o0, g, h, s0)
    return (o_f, o_b), bonus, s_fin


CONV_HALO = 16


def _conv_kernel(cur_ref, prev_ref, next_ref, w_ref, b_ref, lnw_ref, lnb_ref, o_ref, u_ref, *, nt, tt):
    i = pl.program_id(1)

    def glu(x):
        x = x.astype(F32)
        return x[:, :CONV_CH] * _sigmoid(x[:, CONV_CH:])

    u_ref[0, 0:CONV_HALO, :] = jnp.where(i > 0, glu(prev_ref[0]), 0.0)
    u_ref[0, CONV_HALO:CONV_HALO + tt, :] = glu(cur_ref[0])
    u_ref[0, CONV_HALO + tt:, :] = jnp.where(i < nt - 1, glu(next_ref[0]), 0.0)
    span = tt + 2 * CONV_HALO - SUBLANE
    for s in range(1, SUBLANE):
        u_ref[s, 0:span, :] = u_ref[0, s:s + span, :]
    acc = jnp.zeros((tt, CONV_CH), F32) + b_ref[...]
    base = CONV_HALO - CONV_K // 2
    for j in range(CONV_K):
        s = (base + j) % SUBLANE
        start = base + j - s
        acc = acc + u_ref[s, start:start + tt, :] * w_ref[j:j + 1, :]
    mu = jnp.mean(acc, axis=-1, keepdims=True)
    xc = acc - mu
    var = jnp.mean(xc * xc, axis=-1, keepdims=True)
    un = (xc * lax.rsqrt(var + CONV_LN_EPS)) * lnw_ref[...] + lnb_ref[...]
    o_ref[0] = un * _sigmoid(un)


def _conv(p, conv_w, conv_b, ln_w, ln_b):
    bsz, t, _ = p.shape
    tt = min(512, t)
    nt = t // tt
    hb = tt // CONV_HALO
    nh = t // CONV_HALO
    blk = _cblk("cv")
    w = COL["cv"][1]
    vec = lambda a: a.reshape(1, CONV_CH)
    return pl.pallas_call(
        functools.partial(_conv_kernel, nt=nt, tt=tt),
        grid=(bsz, nt),
        in_specs=[pl.BlockSpec((1, tt, w), lambda b, i: (b, i, blk)),
                  pl.BlockSpec((1, CONV_HALO, w), lambda b, i: (b, jnp.maximum(i * hb - 1, 0), blk)),
                  pl.BlockSpec((1, CONV_HALO, w), lambda b, i: (b, jnp.minimum((i + 1) * hb, nh - 1), blk)),
                  pl.BlockSpec((CONV_K, CONV_CH), lambda b, i: (0, 0)),
                  pl.BlockSpec((1, CONV_CH), lambda b, i: (0, 0)),
                  pl.BlockSpec((1, CONV_CH), lambda b, i: (0, 0)),
                  pl.BlockSpec((1, CONV_CH), lambda b, i: (0, 0))],
        out_specs=pl.BlockSpec((1, tt, CONV_CH), lambda b, i: (b, i, 0)),
        out_shape=jax.ShapeDtypeStruct((bsz, t, CONV_CH), F32),
        scratch_shapes=[pltpu.VMEM((SUBLANE, tt + 2 * CONV_HALO, CONV_CH), F32)],
        compiler_params=_params("parallel", "parallel"),
        name="conv_module",
    )(p, p, p, conv_w, vec(conv_b), vec(ln_w), vec(ln_b))


def _swap_halves(x, hs):
    slabs = []
    for j in range(x.shape[-1] // LANE):
        s = x[:, LANE * j:LANE * (j + 1)]
        lane = lax.broadcasted_iota(jnp.int32, s.shape, 1)
        slabs.append(jnp.where((lane % (2 * hs)) < hs, pltpu.roll(s, LANE - hs, 1), pltpu.roll(s, hs, 1)))
    return slabs[0] if len(slabs) == 1 else jnp.concatenate(slabs, axis=-1)


def _rope(x, cos, sin, hs):
    reps = x.shape[-1] // LANE
    if reps > 1:
        cos = jnp.concatenate([cos] * reps, axis=-1)
        sin = jnp.concatenate([sin] * reps, axis=-1)
    return x * cos + _swap_halves(x, hs) * sin


def _rms(x, w):
    ms = jnp.mean(x * x, axis=-1, keepdims=True)
    return (x * lax.rsqrt(ms + NORM_EPS)) * w


def _prep_kernel(*refs, rope):
    if rope:
        (dq_ref, dkv_ref, kr_ref, wq_ref, wk_ref, c64_ref, s64_ref, c32_ref, s32_ref,
         qnw_ref, qup_ref, kvnw_ref, kvup_ref, kvupt_ref, qm_ref, kmt_ref, vm_ref, wqo_ref, wko_ref) = refs
    else:
        (dq_ref, dkv_ref, kr_ref, wq_ref, wk_ref,
         qnw_ref, qup_ref, kvnw_ref, kvup_ref, kvupt_ref, qm_ref, kmt_ref, vm_ref, wqo_ref, wko_ref) = refs
    hw = MLA_HEADS * MLA_NOPE
    q = jnp.dot(_rms(dq_ref[0].astype(F32), qnw_ref[...]).astype(BF16), qup_ref[...],
                preferred_element_type=F32)
    xkv = _rms(dkv_ref[0].astype(F32), kvnw_ref[...]).astype(BF16)
    vv = jnp.dot(xkv, kvup_ref[:, hw:], preferred_element_type=F32)
    knt = jnp.einsum('kd,td->kt', kvupt_ref[...], xkv, preferred_element_type=F32)
    qn, qr = q[:, :hw], q[:, hw:]
    kr = kr_ref[0].astype(F32)
    wq = wq_ref[0].astype(F32)
    wk = wk_ref[0].astype(F32)
    if rope:
        qr = _rope(qr, c32_ref[...], s32_ref[...], MLA_ROPE // 4)
        kr = _rope(kr, c32_ref[...], s32_ref[...], MLA_ROPE // 4)
        wq = _rope(wq, c64_ref[...], s64_ref[...], WA_HEAD // 4)
        wk = _rope(wk, c64_ref[...], s64_ref[...], WA_HEAD // 4)
    wqo_ref[0] = wq.astype(BF16)
    wko_ref[0] = wk.astype(BF16)
    qn = qn * MLA_QSCALE
    qr = qr * MLA_QSCALE
    lane = lax.broadcasted_iota(jnp.int32, (qn.shape[0], MLA_V), 1)
    ones_col = jnp.where(lane == 0, 1.0, 0.0)
    krt = kr.T[:MLA_NOPE]
    for h in range(MLA_HEADS):
        sl = slice(MLA_NOPE * h, MLA_NOPE * (h + 1))
        qm_ref[0, h] = jnp.concatenate([qn[:, sl], qr[:, sl]], axis=-1).astype(BF16)
        kmt_ref[0, h, 0] = jnp.concatenate([knt[sl], krt], axis=0).astype(BF16)
        vm_ref[0, h] = jnp.concatenate([vv[:, sl], ones_col], axis=-1).astype(BF16)


def _attn_prep(p32, p, tables, q_norm, q_up_p, kv_norm, kv_up_p, tk):
    bsz, t, _ = p.shape
    tm = tk
    rope = tables is not None

    def slab(name):
        blk = _cblk(name)
        return pl.BlockSpec((1, tm, COL[name][1]), lambda b, i: (b, i, blk))

    def full(a):
        nd = a.ndim
        return pl.BlockSpec(a.shape, lambda b, i: (0,) * nd)

    ins = [p, p, p, p32, p]
    in_specs = [slab(n) for n in ("dq", "dkv", "kr", "wq", "wk")]
    if rope:
        ins += list(tables)
        in_specs += [pl.BlockSpec((tm, LANE), lambda b, i: (i, 0))] * 4
    kv_up_knt = kv_up_p[:, :MLA_HEADS * MLA_NOPE].T
    small = [q_norm.reshape(1, -1), q_up_p, kv_norm.reshape(1, -1), kv_up_p, kv_up_knt]
    ins += small
    in_specs += [full(a) for a in small]
    return pl.pallas_call(
        functools.partial(_prep_kernel, rope=rope),
        grid=(bsz, t // tm),
        in_specs=in_specs,
        out_specs=[pl.BlockSpec((1, MLA_HEADS, tm, LANE), lambda b, i: (b, 0, i, 0)),
                   pl.BlockSpec((1, MLA_HEADS, 1, LANE, tm), lambda b, i: (b, 0, i, 0, 0)),
                   pl.BlockSpec((1, MLA_HEADS, tm, LANE), lambda b, i: (b, 0, i, 0)),
                   pl.BlockSpec((1, tm, WA_HEADS * WA_HEAD), lambda b, i: (b, i, 0)),
                   pl.BlockSpec((1, tm, WA_KV_HEADS * WA_HEAD), lambda b, i: (b, i, 0))],
        out_shape=[jax.ShapeDtypeStruct((bsz, MLA_HEADS, t, LANE), BF16),
                   jax.ShapeDtypeStruct((bsz, MLA_HEADS, t // tm, LANE, tm), BF16),
                   jax.ShapeDtypeStruct((bsz, MLA_HEADS, t, LANE), BF16),
                   jax.ShapeDtypeStruct((bsz, t, WA_HEADS * WA_HEAD), BF16),
                   jax.ShapeDtypeStruct((bsz, t, WA_KV_HEADS * WA_HEAD), BF16)],
        compiler_params=_params("parallel", "parallel"),
        name="attn_prep",
    )(*ins)


def _rope_tables(t):
    pos = jnp.arange(t)
    row = (pos // GRID_W).astype(F32)[:, None]
    col = (pos % GRID_W).astype(F32)[:, None]
    out = []
    for d in (WA_HEAD, MLA_ROPE):
        q4 = d // 4
        freqs = ROPE_BASE ** (-jnp.arange(q4, dtype=F32) / q4)
        ar, ac = row * freqs[None, :], col * freqs[None, :]
        cos = jnp.concatenate([jnp.cos(ar), jnp.cos(ar), jnp.cos(ac), jnp.cos(ac)], axis=-1)
        sin = jnp.concatenate([-jnp.sin(ar), jnp.sin(ar), -jnp.sin(ac), jnp.sin(ac)], axis=-1)
        out += [jnp.tile(cos, (1, LANE // d)), jnp.tile(sin, (1, LANE // d))]
    return tuple(out)


WA_QBLOCKS = 8


def _wattn_kernel(*refs, band, nbt):
    if band:
        q_ref, kp_ref, kc_ref, kn_ref, vp_ref, vc_ref, vn_ref, kx_ref, vx_ref, sink_ref, o_ref = refs
    else:
        q_ref, kx_ref, vx_ref, sink_ref, o_ref = refs
    n = pl.program_id(1)
    tq = q_ref.shape[1]
    nq = tq // BLOCK if band else 1
    rows = tq // nq
    q = q_ref[0]
    kx = kx_ref[0]
    vx = vx_ref[0]
    if band:
        kc, vc = kc_ref[0], vc_ref[0]
        kblocks = [kp_ref[0]] + [kc[BLOCK * j:BLOCK * (j + 1)] for j in range(nq)] + [kn_ref[0]]
        vblocks = [vp_ref[0]] + [vc[BLOCK * j:BLOCK * (j + 1)] for j in range(nq)] + [vn_ref[0]]
        nk = 3 * BLOCK + kx.shape[0]
        qi = lax.broadcasted_iota(jnp.int32, (rows, nk), 0)
        kj = lax.broadcasted_iota(jnp.int32, (rows, nk), 1)
        in_band = jnp.abs(kj - BLOCK - qi) <= WINDOW
        is_ctx = kj >= 3 * BLOCK

    chains = [(j, g) for j in range(nq) for g in range(WA_KV_HEADS)]

    def scores(ch):
        j, g = ch
        ks = slice(WA_HEAD * g, WA_HEAD * (g + 1))
        qg = jnp.concatenate([q[rows * j:rows * (j + 1), WA_HEAD * (WA_GROUP * g + i):WA_HEAD * (WA_GROUP * g + i + 1)]
                              for i in range(WA_GROUP)], axis=0)
        if band:
            kb = jnp.concatenate([kblocks[j][:, ks], kblocks[j + 1][:, ks], kblocks[j + 2][:, ks], kx[:, ks]], axis=0)
        else:
            kb = kx[:, ks]
        s = jnp.einsum('qd,kd->qk', qg, kb, preferred_element_type=F32) * WA_SCALE
        if band:
            blk = n * nq + j
            ok = in_band & ((kj >= BLOCK) | (blk > 0)) & ((kj < 2 * BLOCK) | (blk < nbt - 1))
            ok = ok | is_ctx
            s = jnp.where(jnp.concatenate([ok] * WA_GROUP, axis=0), s, -jnp.inf)
        return s

    ss = [scores(ch) for ch in chains]

    def probs(ch, s):
        g = ch[1]
        sk = jnp.concatenate([jnp.full((rows, 1), sink_ref[0, WA_GROUP * g + i], F32) for i in range(WA_GROUP)],
                             axis=0)
        m = jnp.maximum(jnp.max(s, axis=-1, keepdims=True), sk)
        e = jnp.exp(s - m)
        inv = 1.0 / (jnp.sum(e, axis=-1, keepdims=True) + jnp.exp(sk - m))
        return (e * inv).astype(BF16)

    ps = [probs(ch, s) for ch, s in zip(chains, ss)]

    def values(ch, p):
        j, g = ch
        ks = slice(WA_HEAD * g, WA_HEAD * (g + 1))
        if band:
            vb = jnp.concatenate([vblocks[j][:, ks], vblocks[j + 1][:, ks], vblocks[j + 2][:, ks], vx[:, ks]], axis=0)
        else:
            vb = vx[:, ks]
        return jnp.dot(p, vb, preferred_element_type=F32)

    ogs = [values(ch, p) for ch, p in zip(chains, ps)]
    for j in range(nq):
        heads = []
        for g in range(WA_KV_HEADS):
            og = ogs[chains.index((j, g))]
            heads += [og[rows * i:rows * (i + 1)] for i in range(WA_GROUP)]
        o_ref[0, rows * j:rows * (j + 1), :] = jnp.concatenate(heads, axis=-1)


def _window_attn(wq, wk, p, wkx, px, sink, band):
    bsz, t, _ = wq.shape
    cx = wkx.shape[1]
    tq = WA_QBLOCKS * BLOCK if band else t
    nsteps = t // tq
    nbt = t // BLOCK
    vblk = _cblk("wv")
    kvw = WA_KV_HEADS * WA_HEAD
    ins = [wq]
    in_specs = [pl.BlockSpec((1, tq, WA_HEADS * WA_HEAD), lambda b, n: (b, n, 0))]
    if band:
        prev = lambda n: jnp.maximum(n * WA_QBLOCKS - 1, 0)
        nxt = lambda n: jnp.minimum((n + 1) * WA_QBLOCKS, nbt - 1)
        for src, cblk in ((wk, 0), (p, vblk)):
            ins += [src] * 3
            in_specs += [pl.BlockSpec((1, BLOCK, kvw), functools.partial(lambda c, b, n: (b, prev(n), c), cblk)),
                         pl.BlockSpec((1, tq, kvw), functools.partial(lambda c, b, n: (b, n, c), cblk)),
                         pl.BlockSpec((1, BLOCK, kvw), functools.partial(lambda c, b, n: (b, nxt(n), c), cblk))]
    ins += [wkx, px, sink.reshape(1, WA_HEADS)]
    in_specs += [pl.BlockSpec((1, cx, kvw), lambda b, n: (b, 0, 0)),
                 pl.BlockSpec((1, cx, kvw), lambda b, n: (b, 0, vblk)),
                 pl.BlockSpec(memory_space=pltpu.SMEM)]
    return pl.pallas_call(
        functools.partial(_wattn_kernel, band=band, nbt=nbt),
        grid=(bsz, nsteps),
        in_specs=in_specs,
        out_specs=pl.BlockSpec((1, tq, WA_HEADS * WA_HEAD), lambda b, n: (b, n, 0)),
        out_shape=jax.ShapeDtypeStruct((bsz, t, WA_HEADS * WA_HEAD), F32),
        compiler_params=_params("parallel", "parallel"),
        name="window_attn" if band else "context_gqa",
    )(*ins)


MLA_HPS = 2


def _mla_kernel(*refs, nkt, extra):
    if extra:
        q_ref, kt_ref, v_ref, ktx_ref, vx_ref, o_ref = refs
    else:
        q_ref, kt_ref, v_ref, o_ref = refs
    tq = q_ref.shape[2]
    tk = kt_ref.shape[4]

    heads = range(MLA_HPS)

    def update(carry, kts, vs, staged):
        def score(hh):
            return jnp.dot(q_ref[0, hh], kts[hh], preferred_element_type=F32)

        def finish(hh, s):
            m, acc = carry[hh]
            m_new = jnp.maximum(m, jnp.max(s, axis=-1, keepdims=True))
            p = jnp.exp2(s - m_new).astype(BF16)
            return m_new, acc * jnp.exp2(m - m_new) + jnp.dot(p, vs[hh], preferred_element_type=F32)

        if staged:
            ss = [score(hh) for hh in heads]
            return tuple(finish(hh, ss[hh]) for hh in heads)
        return tuple(finish(hh, score(hh)) for hh in heads)

    def body(j, carry):
        start = pl.multiple_of(j * tk, tk)
        return update(carry, [kt_ref[0, hh, j] for hh in heads],
                      [v_ref[0, hh, pl.ds(start, tk), :] for hh in heads], staged=False)

    init = (jnp.full((tq, 1), -jnp.inf, F32), jnp.zeros((tq, LANE), F32))
    carry = lax.fori_loop(0, nkt, body, (init,) * MLA_HPS, unroll=True)
    if extra:
        carry = update(carry, [ktx_ref[0, hh, 0] for hh in heads], [vx_ref[0, hh] for hh in heads], staged=True)
    o_ref[0] = jnp.concatenate([carry[hh][1][:, :MLA_V] / carry[hh][1][:, MLA_V:MLA_V + 1] for hh in heads],
                               axis=-1)


def _mla_attn(qm, kmt, vm, kmtx=None, vmx=None):
    bsz, nh, t, _ = qm.shape
    nkt, tk = kmt.shape[2], kmt.shape[4]
    tkv = vm.shape[2]
    tq = min(1024, t)
    extra = kmtx is not None
    ins = [qm, kmt, vm]
    in_specs = [pl.BlockSpec((1, MLA_HPS, tq, LANE), lambda b, h, i: (b, h, i, 0)),
                pl.BlockSpec((1, MLA_HPS, nkt, LANE, tk), lambda b, h, i: (b, h, 0, 0, 0)),
                pl.BlockSpec((1, MLA_HPS, tkv, LANE), lambda b, h, i: (b, h, 0, 0))]
    if extra:
        cx = vmx.shape[2]
        ins += [kmtx, vmx]
        in_specs += [pl.BlockSpec((1, MLA_HPS, 1, LANE, cx), lambda b, h, i: (b, h, 0, 0, 0)),
                     pl.BlockSpec((1, MLA_HPS, cx, LANE), lambda b, h, i: (b, h, 0, 0))]
    return pl.pallas_call(
        functools.partial(_mla_kernel, nkt=nkt, extra=extra),
        grid=(bsz, nh // MLA_HPS, t // tq),
        in_specs=in_specs,
        out_specs=pl.BlockSpec((1, tq, MLA_HPS * MLA_V), lambda b, h, i: (b, i, h)),
        out_shape=jax.ShapeDtypeStruct((bsz, t, nh * MLA_V), F32),
        compiler_params=_params("parallel", "parallel", "parallel"),
        name="mla_attn",
    )(*ins)


def _merge_kernel(x_ref, g_ref, of_ref, ob_ref, bonus_ref, yb_ref, yc_ref, yd_ref, z_ref, gate_ref,
                  gnw_ref, gnb_ref, wb_ref, wo_ref, fnw_ref, out_ref, *, final):
    o = of_ref[0] + ob_ref[0]
    oc = o - _head_sum(o) * (1.0 / RW_HEAD)
    var = _head_sum(oc * oc) * (1.0 / RW_HEAD)
    ya = (oc * lax.rsqrt(var + RW_GN_EPS)) * gnw_ref[...] + gnb_ref[...] + bonus_ref[0]
    ys = (ya, yb_ref[0], yc_ref[0], yd_ref[0])
    m = None
    for i, y in enumerate(ys):
        zh = z_ref[0, :, BRANCH_W * i:BRANCH_W * (i + 1)].astype(F32)
        gh = gate_ref[0, :, D_MODEL * i:D_MODEL * (i + 1)].astype(F32)
        yz = (y * (zh * jnp.tanh(zh) + zh)).astype(BF16)
        term = (jnp.tanh(gh) + 1.0) * jnp.dot(yz, wb_ref[i], preferred_element_type=F32)
        m = term if m is None else m + term
    xn = x_ref[0] + g_ref[0] * jnp.dot(m.astype(BF16), wo_ref[...], preferred_element_type=F32)
    if final:
        xn = _rms(xn, fnw_ref[...])
    out_ref[0] = xn


def _merge(x, g, o, bonus, yb, yc, yd, p, gn_w, gn_b, w_branch, w_out, final_norm_w, final):
    bsz, t, d = x.shape
    tm = min(512, t)
    zblk, gblk = _cblk("z"), _cblk("gate")
    row = lambda w: pl.BlockSpec((1, tm, w), lambda b, i: (b, i, 0))
    return pl.pallas_call(
        functools.partial(_merge_kernel, final=final),
        grid=(bsz, t // tm),
        in_specs=[row(d),
                  pl.BlockSpec((1, 1, d), lambda b, i: (b, 0, 0)),
                  row(RW_C), row(RW_C), row(RW_C),
                  row(BRANCH_W), row(BRANCH_W), row(BRANCH_W),
                  pl.BlockSpec((1, tm, COL["z"][1]), lambda b, i: (b, i, zblk)),
                  pl.BlockSpec((1, tm, COL["gate"][1]), lambda b, i: (b, i, gblk)),
                  pl.BlockSpec((1, RW_C), lambda b, i: (0, 0)),
                  pl.BlockSpec((1, RW_C), lambda b, i: (0, 0)),
                  pl.BlockSpec((N_BRANCH, BRANCH_W, d), lambda b, i: (0, 0, 0)),
                  pl.BlockSpec((d, d), lambda b, i: (0, 0)),
                  pl.BlockSpec((1, d), lambda b, i: (0, 0))],
        out_specs=row(d),
        out_shape=jax.ShapeDtypeStruct((bsz, t, d), F32),
        compiler_params=_params("parallel", "parallel"),
        name="merge",
    )(x, g, o[0], o[1], bonus, yb, yc, yd, p, p, gn_w.reshape(1, RW_C), gn_b.reshape(1, RW_C),
      w_branch, w_out, final_norm_w.reshape(1, d))


MLA_TK = 1024


def _permute_q_up(q_up):
    r = q_up.shape[0]
    w = q_up.reshape(r, MLA_HEADS, MLA_NOPE + MLA_ROPE)
    nope = w[:, :, :MLA_NOPE].reshape(r, MLA_HEADS * MLA_NOPE)
    rope = jnp.pad(w[:, :, MLA_NOPE:], ((0, 0), (0, 0), (0, MLA_NOPE - MLA_ROPE))).reshape(r, MLA_HEADS * MLA_NOPE)
    return jnp.concatenate([nope, rope], axis=-1)


def _permute_kv_up(kv_up):
    r = kv_up.shape[0]
    w = kv_up.reshape(r, MLA_HEADS, MLA_NOPE + MLA_V)
    return jnp.concatenate([w[:, :, :MLA_NOPE].reshape(r, -1), w[:, :, MLA_NOPE:].reshape(r, -1)], axis=-1)


def kernel(x, c, ctx, c_ctx, norm_w, ada_w, ada_b, w_in, rwkv_mu, rwkv_w0, rwkv_w_up, rwkv_a0, rwkv_a_up, rwkv_k_k, rwkv_k_a, rwkv_r_k, rwkv_gn_w, rwkv_gn_b, conv_w, conv_b, conv_ln_w, conv_ln_b, attn_sink, mla_q_norm, mla_q_up, mla_kv_norm, mla_kv_up, w_branch, w_out, final_norm_w):
    bsz, t, d = x.shape
    n_ctx = ctx.shape[1]
    depth = norm_w.shape[0]
    assert t % MLA_TK == 0 and t % BLOCK == 0 and n_ctx % RW_CHUNK == 0 and n_ctx <= MLA_TK

    cvec = jnp.concatenate([c, c_ctx[None, :], jnp.zeros((SUBLANE - bsz - 1, d), F32)], axis=0)
    mod = _modulation(cvec, ada_w, ada_b)
    tables = _rope_tables(t)
    s_zero = jnp.zeros((bsz, 2, RW_HEAD, RW_C), F32)

    for l in range(depth):
        last = l == depth - 1
        mx = mod[l, :bsz][:, None, :]
        mc = jnp.broadcast_to(mod[l, bsz][None, None, :], (bsz, 1, 3 * d))
        sh_x, sc_x, g_x = mx[..., :d], mx[..., d:2 * d], mx[..., 2 * d:]
        sh_c, sc_c, g_c = mc[..., :d], mc[..., d:2 * d], mc[..., 2 * d:]
        w_p = _pack_cols(w_in[l]).astype(BF16)
        px32, px = _in_proj(x, norm_w[l], sc_x, sh_x, w_p)
        pc32, pc = _in_proj(ctx, norm_w[l], sc_c, sh_c, w_p)

        rw_args = (rwkv_mu[l], rwkv_w0[l], rwkv_w_up[l], rwkv_a0[l], rwkv_a_up[l], rwkv_k_k[l], rwkv_k_a[l],
                   rwkv_r_k[l].reshape(RW_C))
        o_c, bonus_c, s_ctx = _rwkv(pc32, pc, s_zero, *rw_args)
        o_x, bonus_x, _ = _rwkv(px32, px, s_ctx, *rw_args)

        conv_args = (conv_w[l], conv_b[l], conv_ln_w[l], conv_ln_b[l])
        yb_x = _conv(px, *conv_args)

        q_up_p = _permute_q_up(mla_q_up[l]).astype(BF16)
        kv_up_p = _permute_kv_up(mla_kv_up[l]).astype(BF16)
        prep_w = (mla_q_norm[l], q_up_p, mla_kv_norm[l], kv_up_p)
        qm_x, kmt_x, vm_x, wq_x, wk_x = _attn_prep(px32, px, tables, *prep_w, tk=MLA_TK)
        qm_c, kmt_c, vm_c, wq_c, wk_c = _attn_prep(pc32, pc, None, *prep_w, tk=n_ctx)

        yc_x = _window_attn(wq_x, wk_x, px, wk_c, pc, attn_sink[l], band=True)
        yd_x = _mla_attn(qm_x, kmt_x, vm_x, kmt_c, vm_c)

        merge_w = (rwkv_gn_w[l], rwkv_gn_b[l], w_branch[l].astype(BF16), (0.5 * w_out[l]).astype(BF16),
                   final_norm_w)
        x = _merge(x, g_x, o_x, bonus_x, yb_x, yc_x, yd_x, px, *merge_w, final=last)

        if not last:
            yb_c = _conv(pc, *conv_args)
            yc_c = _window_attn(wq_c, wk_c, pc, wk_c, pc, attn_sink[l], band=False)
            yd_c = _mla_attn(qm_c, kmt_c, vm_c)
            ctx = _merge(ctx, g_c, o_c, bonus_c, yb_c, yc_c, yd_c, pc, *merge_w, final=False)

    return x
```
